```python
import jax, jax.numpy as jnp
from jax import lax
import numpy as np

D_MODEL = 1024
BATCH = 8
SEQ = 4096
DEPTH = 1
DEC_BATCH = 8
DEC_SEQ = 32
PAST_LEN = 2048

CHUNK = 64
D_MIX = D_MODEL
ML_HEADS = 4
ML_DIM = 128
ML_WIDTH = ML_HEADS * ML_DIM
SB_HEADS = 8
SB_DIM = 64
SB_WIDTH = SB_HEADS * SB_DIM
Q_BLOCK = 128
D_FF = 2816
CONV_W = 3
EPS = 1e-6
SPLIT_SIZES = (ML_WIDTH, ML_WIDTH, ML_WIDTH, ML_WIDTH, ML_HEADS, ML_HEADS, SB_WIDTH, SB_WIDTH, SB_WIDTH)
D_IN = 4 * ML_WIDTH + 2 * ML_HEADS + 3 * SB_WIDTH

kernel_name = "hymba_mlstm_stickbreaking_convffn_step"


def rmsnorm(x, w):
    xf = x.astype(jnp.float32)
    y = xf * lax.rsqrt(jnp.mean(xf * xf, axis=-1, keepdims=True) + EPS)
    return (y * w.astype(jnp.float32)).astype(x.dtype)


def adaln(c, w_ada, b_ada):
    mod = jax.nn.silu(c) @ w_ada + b_ada
    return jnp.split(mod[:, None, :], 6, axis=-1)


def split_proj(proj):
    idx, acc = [], 0
    for s in SPLIT_SIZES[:-1]:
        acc += s
        idx.append(acc)
    return jnp.split(proj, idx, axis=-1)


def mlstm_chunk(state, inp):
    C, n, m = state
    q, k, v, ig, lf = inp
    L = q.shape[2]
    b = jnp.cumsum(lf, axis=-1)
    causal = jnp.tril(jnp.ones((L, L), dtype=bool))
    d_log = jnp.where(causal, b[..., :, None] - b[..., None, :] + ig[..., None, :], -jnp.inf)
    inter = b + m[..., None]
    m_t = jnp.maximum(inter, jnp.max(d_log, axis=-1))
    w_intra = jnp.exp(d_log - m_t[..., None])
    w_inter = jnp.exp(inter - m_t)
    s = jnp.einsum('bhtd,bhsd->bhts', q, k) * w_intra
    num = w_inter[..., None] * jnp.einsum('bhtd,bhde->bhte', q, C) + jnp.einsum('bhts,bhse->bhte', s, v)
    den = w_inter * jnp.einsum('bhtd,bhd->bht', q, n) + jnp.sum(s, axis=-1)
    h = num / jnp.maximum(jnp.abs(den), jnp.exp(-m_t))[..., None]
    b_last = b[..., -1]
    w_k = b_last[..., None] - b + ig
    m_new = jnp.maximum(b_last + m, jnp.max(w_k, axis=-1))
    decay = jnp.exp(b_last + m - m_new)
    wk = jnp.exp(w_k - m_new[..., None])
    C_new = decay[..., None, None] * C + jnp.einsum('bhs,bhsd,bhse->bhde', wk, k, v)
    n_new = decay[..., None] * n + jnp.einsum('bhs,bhsd->bhd', wk, k)
    return (C_new, n_new, m_new), h


def mlstm_prompt(q, k, v, ig, lf):
    B, H, T, _ = q.shape
    nc = T // CHUNK

    def to_chunks(t):
        return jnp.moveaxis(t.reshape(t.shape[:2] + (nc, CHUNK) + t.shape[3:]), 2, 0)

    init = (jnp.zeros((B, H, ML_DIM, ML_DIM), jnp.float32),
            jnp.zeros((B, H, ML_DIM), jnp.float32),
            jnp.zeros((B, H), jnp.float32))
    state, h = lax.scan(mlstm_chunk, init, (to_chunks(q), to_chunks(k), to_chunks(v), to_chunks(ig), to_chunks(lf)))
    h = jnp.moveaxis(h, 0, 2).reshape(B, H, T, ML_DIM)
    return h, state


def stick_breaking_block(q, k, v, q_pos, k_pos):
    z = jnp.einsum('bhtd,bhsd->bhts', q, k).astype(jnp.float32) * (SB_DIM ** -0.5)
    mask = k_pos[None, :] < q_pos[:, None]
    log_beta = jnp.where(mask, jax.nn.log_sigmoid(z), -jnp.inf)
    log_1m = jnp.where(mask, jax.nn.log_sigmoid(-z), 0.0)
    rest = lax.cumsum(log_1m, axis=3, reverse=True) - log_1m
    a = jnp.exp(log_beta + rest)
    return jnp.einsum('bhts,bhsd->bhtd', a.astype(v.dtype), v)


def sb_prompt(q, k, v):
    B, H, T, d = q.shape
    nb = T // Q_BLOCK
    qb = jnp.moveaxis(q.reshape(B, H, nb, Q_BLOCK, d), 2, 0)
    k_pos = jnp.arange(T)

    def one_block(args):
        q_blk, start = args
        return stick_breaking_block(q_blk, k, v, start + jnp.arange(Q_BLOCK), k_pos)

    o = lax.map(one_block, (qb, jnp.arange(nb) * Q_BLOCK))
    return jnp.moveaxis(o, 0, 2).reshape(B, H, T, d)


def conv_ffn(h, conv_state, w_up, w_conv, b_conv, w_down):
    u = h @ w_up
    T = u.shape[1]
    padded = jnp.concatenate([conv_state.astype(u.dtype), u], axis=1)
    uc = sum(padded[:, j:j + T] * w_conv[j] for j in range(CONV_W)) + b_conv
    a, g = jnp.split(uc, 2, axis=-1)
    out = (jax.nn.silu(a) * g) @ w_down
    return out, padded[:, -(CONV_W - 1):]


def trunk_layer(x, c, p, ml_state=None, k_past=None, v_past=None, conv_state=None):
    B, T, _ = x.shape
    sh1, sc1, g1, sh2, sc2, g2 = adaln(c, p['w_ada'], p['b_ada'])
    h = rmsnorm(x, p['norm1_w']) * (1 + sc1) + sh1
    mq, mk, mv, mo, mi, mf, sq, sk, sv = split_proj(h @ p['w_in'])

    def heads(t, nh):
        return t.reshape(B, T, nh, -1).transpose(0, 2, 1, 3).astype(jnp.float32)
    q_a = heads(mq, ML_HEADS)
    k_a = heads(mk, ML_HEADS) * (ML_DIM ** -0.5)
    v_a = heads(mv, ML_HEADS)
    ig = (mi + p['b_ig']).astype(jnp.float32).transpose(0, 2, 1)
    lf = jax.nn.log_sigmoid((mf + p['b_fg']).astype(jnp.float32)).transpose(0, 2, 1)
    if ml_state is None:
        h_cell, ml_new = mlstm_prompt(q_a, k_a, v_a, ig, lf)
    else:
        st = tuple(s.astype(jnp.float32) for s in ml_state)
        ml_new, h_cell = mlstm_chunk(st, (q_a, k_a, v_a, ig, lf))
    h_cell = rmsnorm(h_cell.transpose(0, 2, 1, 3), p['ml_norm_w']).reshape(B, T, ML_WIDTH).astype(x.dtype)
    y_a = h_cell * jax.nn.sigmoid(mo)

    q_b = rmsnorm(sq.reshape(B, T, SB_HEADS, SB_DIM), p['sb_q_norm_w'])
    k_rows = rmsnorm(sk.reshape(B, T, SB_HEADS, SB_DIM), p['sb_k_norm_w'])
    v_rows = sv.reshape(B, T, SB_HEADS, SB_DIM)
    if k_past is None:
        o_b = sb_prompt(q_b.transpose(0, 2, 1, 3), k_rows.transpose(0, 2, 1, 3), v_rows.transpose(0, 2, 1, 3))
    else:
        P = k_past.shape[1]
        k_all = jnp.concatenate([k_past.astype(k_rows.dtype), k_rows], axis=1).transpose(0, 2, 1, 3)
        v_all = jnp.concatenate([v_past.astype(v_rows.dtype), v_rows], axis=1).transpose(0, 2, 1, 3)
        o_b = stick_breaking_block(q_b.transpose(0, 2, 1, 3), k_all, v_all, P + jnp.arange(T), jnp.arange(P + T))
    y_b = rmsnorm(o_b.transpose(0, 2, 1, 3).reshape(B, T, SB_WIDTH), p['sb_out_norm_w'])

    mix = jnp.concatenate([y_a, y_b], axis=-1) @ p['w_out']
    x = x + g1 * mix

    h2 = rmsnorm(x, p['norm2_w']) * (1 + sc2) + sh2
    if conv_state is None:
        conv_state = jnp.zeros((B, CONV_W - 1, 2 * D_FF), x.dtype)
    f_out, conv_new = conv_ffn(h2, conv_state, p['w_up'], p['w_conv'], p['b_conv'], p['w_down'])
    x = x + g2 * f_out
    return x, ml_new, k_rows, v_rows, conv_new


def setup_inputs(seed: int = 0) -> dict:
    key = jax.random.key(seed)
    ks = jax.random.split(key, 32)
    nrm = lambda k, shape: jax.random.normal(k, shape, jnp.float32)
    return {
        "x_prompt": nrm(ks[0], (BATCH, SEQ, D_MODEL)),
        "x_sample": nrm(ks[1], (DEC_BATCH, DEC_SEQ, D_MODEL)),
        "c_prompt": nrm(ks[2], (BATCH, D_MODEL)),
        "c_sample": nrm(ks[3], (DEC_BATCH, D_MODEL)),
        "state_mlstm_c": 0.1 * nrm(ks[4], (DEPTH, DEC_BATCH, ML_HEADS, ML_DIM, ML_DIM)),
        "state_mlstm_n": 0.5 * nrm(ks[5], (DEPTH, DEC_BATCH, ML_HEADS, ML_DIM)),
        "state_mlstm_m": 0.5 * nrm(ks[6], (DEPTH, DEC_BATCH, ML_HEADS)),
        "cache_sb_k": nrm(ks[7], (DEPTH, DEC_BATCH, PAST_LEN, SB_HEADS, SB_DIM)),
        "cache_sb_v": nrm(ks[8], (DEPTH, DEC_BATCH, PAST_LEN, SB_HEADS, SB_DIM)),
        "state_ffn_conv": nrm(ks[9], (DEPTH, DEC_BATCH, CONV_W - 1, 2 * D_FF)),
        "w_ada": 0.5 * D_MODEL ** -0.5 * nrm(ks[10], (DEPTH, D_MODEL, 6 * D_MODEL)),
        "b_ada": 0.01 * nrm(ks[11], (DEPTH, 6 * D_MODEL)),
        "norm1_w": 1.0 + 0.02 * nrm(ks[12], (DEPTH, D_MODEL)),
        "w_in": D_MODEL ** -0.5 * nrm(ks[13], (DEPTH, D_MODEL, D_IN)),
        "b_ig": 0.1 * nrm(ks[14], (DEPTH, ML_HEADS)),
        "b_fg": 3.0 + 0.5 * nrm(ks[15], (DEPTH, ML_HEADS)),
        "ml_norm_w": 1.0 + 0.02 * nrm(ks[16], (DEPTH, ML_HEADS, ML_DIM)),
        "sb_q_norm_w": 1.0 + 0.02 * nrm(ks[17], (DEPTH, SB_DIM)),
        "sb_k_norm_w": 1.0 + 0.02 * nrm(ks[18], (DEPTH, SB_DIM)),
        "sb_out_norm_w": 1.0 + 0.02 * nrm(ks[19], (DEPTH, SB_WIDTH)),
        "w_out": D_MIX ** -0.5 * nrm(ks[20], (DEPTH, D_MIX, D_MODEL)),
        "norm2_w": 1.0 + 0.02 * nrm(ks[21], (DEPTH, D_MODEL)),
        "w_up": D_MODEL ** -0.5 * nrm(ks[22], (DEPTH, D_MODEL, 2 * D_FF)),
        "w_conv": CONV_W ** -0.5 * nrm(ks[23], (DEPTH, CONV_W, 2 * D_FF)),
        "b_conv": 0.01 * nrm(ks[24], (DEPTH, 2 * D_FF)),
        "w_down": D_FF ** -0.5 * nrm(ks[25], (DEPTH, D_FF, D_MODEL)),
    }


def reference(x_prompt, x_sample, c_prompt, c_sample, state_mlstm_c, state_mlstm_n, state_mlstm_m,
              cache_sb_k, cache_sb_v, state_ffn_conv, w_ada, b_ada, norm1_w, w_in, b_ig, b_fg,
              ml_norm_w, sb_q_norm_w, sb_k_norm_w, sb_out_norm_w, w_out, norm2_w, w_up, w_conv,
              b_conv, w_down):
    weights = {"w_ada": w_ada, "b_ada": b_ada, "norm1_w": norm1_w, "w_in": w_in, "b_ig": b_ig,
               "b_fg": b_fg, "ml_norm_w": ml_norm_w, "sb_q_norm_w": sb_q_norm_w,
               "sb_k_norm_w": sb_k_norm_w, "sb_out_norm_w": sb_out_norm_w, "w_out": w_out,
               "norm2_w": norm2_w, "w_up": w_up, "w_conv": w_conv, "b_conv": b_conv, "w_down": w_down}
    y_p, y_s = x_prompt, x_sample
    pc, pn, pm, pk, pv, pf = [], [], [], [], [], []
    sc, sn, sm, sk, sv, sf = [], [], [], [], [], []
    for l in range(DEPTH):
        p = {name: w[l] for name, w in weights.items()}
        y_p, (c_new, n_new, m_new), k_new, v_new, f_new = trunk_layer(y_p, c_prompt, p)
        pc.append(c_new); pn.append(n_new); pm.append(m_new); pk.append(k_new); pv.append(v_new); pf.append(f_new)
        y_s, (c_new, n_new, m_new), k_new, v_new, f_new = trunk_layer(
            y_s, c_sample, p, (state_mlstm_c[l], state_mlstm_n[l], state_mlstm_m[l]),
            cache_sb_k[l], cache_sb_v[l], state_ffn_conv[l])
        sc.append(c_new); sn.append(n_new); sm.append(m_new); sk.append(k_new); sv.append(v_new); sf.append(f_new)
    return (y_p, y_s,
            jnp.stack(pc), jnp.stack(pn), jnp.stack(pm), jnp.stack(pk), jnp.stack(pv), jnp.stack(pf),
            jnp.stack(sc), jnp.stack(sn), jnp.stack(sm), jnp.stack(sk), jnp.stack(sv), jnp.stack(sf))
```

```python
import functools

import numpy as np
import jax
import jax.numpy as jnp
from jax import lax
from jax.experimental import pallas as pl
from jax.experimental.pallas import tpu as pltpu

F32 = jnp.float32
BF16 = jnp.bfloat16

D_MODEL = 1024
ML_HEADS = 4
ML_DIM = 128
ML_WIDTH = ML_HEADS * ML_DIM
SB_HEADS = 8
SB_DIM = 64
SB_WIDTH = SB_HEADS * SB_DIM
D_FF = 2816
CONV_W = 3
EPS = 1e-6

LANES = 128
FF_CHUNK = 256
N_FF_CHUNKS = D_FF // FF_CHUNK
VMEM_LIMIT = 52 * 1024 * 1024

NT_DIMS = (((1,), (1,)), ((), ()))
TN_DIMS = (((0,), (0,)), ((), ()))


def _vmem_spec():
    return pl.BlockSpec(memory_space=pltpu.VMEM)


def _rms(x, w):
    return x * lax.rsqrt(jnp.mean(x * x, axis=-1, keepdims=True) + EPS) * w


def _softplus(x):
    return jnp.maximum(x, 0.0) + jnp.log(1.0 + jnp.exp(-jnp.abs(x)))


def _split3(x):
    h1 = x.astype(BF16)
    r1 = x - h1.astype(F32)
    h2 = r1.astype(BF16)
    h3 = (r1 - h2.astype(F32)).astype(BF16)
    return h1, h2, h3


def _ada_kernel(c_ref, w_ref, b_ref, o_ref):
    c = c_ref[...]
    s = (c * jax.nn.sigmoid(c)).astype(BF16)
    o_ref[...] = jnp.dot(s, w_ref[...].astype(BF16), preferred_element_type=F32) + b_ref[...]


def _ada_call(c, w_ada, b_ada):
    n, d = c.shape
    nout = w_ada.shape[1]
    return pl.pallas_call(
        _ada_kernel,
        out_shape=jax.ShapeDtypeStruct((n, nout), F32),
        grid=(nout // d,),
        in_specs=[pl.BlockSpec((n, d), lambda j: (0, 0)),
                  pl.BlockSpec((d, d), lambda j: (0, j)),
                  pl.BlockSpec((1, d), lambda j: (0, j))],
        out_specs=pl.BlockSpec((n, d), lambda j: (0, j)),
        compiler_params=pltpu.CompilerParams(dimension_semantics=("arbitrary",)),
        name="adaln",
    )(c, w_ada, b_ada.reshape(1, nout))


def _headnorm64(x, w128):
    lane = lax.broadcasted_iota(jnp.int32, x.shape, 1)
    lo = lane < SB_DIM
    x2 = x * x
    s0 = jnp.sum(jnp.where(lo, x2, 0.0), axis=-1, keepdims=True)
    s1 = jnp.sum(jnp.where(lo, 0.0, x2), axis=-1, keepdims=True)
    r0 = lax.rsqrt(s0 * (1.0 / SB_DIM) + EPS)
    r1 = lax.rsqrt(s1 * (1.0 / SB_DIM) + EPS)
    return x * jnp.where(lo, r0, r1) * w128


def _inproj_kernel(x_ref, mod_ref, n1w_ref, wm_ref, wg_ref, wgt_ref, ws_ref, bgc_ref, bgr_ref,
                   qnw_ref, knw_ref,
                   qa_ref, ka_ref, va_ref, og_ref, gcol_ref, grow_ref, qb_ref, kr_ref, vr_ref,
                   kb_ref, vb_ref, *, bb, tm):
    hs = []
    for bi in range(bb):
        m = mod_ref[bi]
        y = _rms(x_ref[bi], n1w_ref[...])
        hs.append((y * (1.0 + m[1:2, :]) + m[0:1, :]).astype(BF16))
    h = hs[0] if bb == 1 else jnp.concatenate(hs, axis=0)

    def put(ref, val):
        for bi in range(bb):
            ref[bi] = val[bi * tm:(bi + 1) * tm].astype(ref.dtype)

    def proj(w_ref, j):
        return jnp.dot(h, w_ref[:, j * ML_WIDTH:(j + 1) * ML_WIDTH], preferred_element_type=F32)

    put(qa_ref, proj(wm_ref, 0))
    put(ka_ref, proj(wm_ref, 1) * (ML_DIM ** -0.5))
    put(va_ref, proj(wm_ref, 2))
    put(og_ref, jax.nn.sigmoid(proj(wm_ref, 3)))

    gc = jnp.dot(h, wg_ref[...], preferred_element_type=F32) + bgc_ref[...]
    lane = lax.broadcasted_iota(jnp.int32, gc.shape, 1)
    put(gcol_ref, jnp.where(lane >= ML_HEADS, -_softplus(-gc), gc))
    gr = lax.dot_general(wgt_ref[...], h, NT_DIMS, preferred_element_type=F32) + bgr_ref[...]
    row = lax.broadcasted_iota(jnp.int32, gr.shape, 0)
    gr = jnp.where(row >= ML_HEADS, -_softplus(-gr), gr)
    for bi in range(bb):
        grow_ref[bi] = gr[:2 * ML_HEADS, bi * tm:(bi + 1) * tm]

    sq = proj(ws_ref, 0)
    sk = proj(ws_ref, 1)
    sv = proj(ws_ref, 2)
    qn, kn = [], []
    for j in range(SB_WIDTH // LANES):
        sl = slice(j * LANES, (j + 1) * LANES)
        qn.append(_headnorm64(sq[:, sl], qnw_ref[...]) * (SB_DIM ** -0.5))
        kn.append(_headnorm64(sk[:, sl], knw_ref[...]))
    qn = jnp.concatenate(qn, axis=1)
    kn = jnp.concatenate(kn, axis=1)
    put(qb_ref, qn)
    put(kr_ref, kn)
    put(kb_ref, kn)
    put(vr_ref, sv)
    put(vb_ref, sv)


def _inproj_call(x, mod3, wts, *, bb, tm):
    B, T, D = x.shape
    grid = (B // bb, T // tm)
    tok = lambda b, t: (b, t, 0)

    def tok_spec(w):
        return pl.BlockSpec((bb, tm, w), tok)

    def tok_shape(w, dt):
        return jax.ShapeDtypeStruct((B, T, w), dt)

    out_shape = [tok_shape(ML_WIDTH, BF16), tok_shape(ML_WIDTH, BF16), tok_shape(ML_WIDTH, BF16),
                 tok_shape(ML_WIDTH, F32), tok_shape(LANES, F32),
                 jax.ShapeDtypeStruct((B, 2 * ML_HEADS, T), F32),
                 tok_shape(SB_WIDTH, BF16), tok_shape(SB_WIDTH, F32), tok_shape(SB_WIDTH, F32),
                 tok_shape(SB_WIDTH, BF16), tok_shape(SB_WIDTH, BF16)]
    out_specs = [tok_spec(ML_WIDTH), tok_spec(ML_WIDTH), tok_spec(ML_WIDTH), tok_spec(ML_WIDTH),
                 tok_spec(LANES), pl.BlockSpec((bb, 2 * ML_HEADS, tm), lambda b, t: (b, 0, t)),
                 tok_spec(SB_WIDTH), tok_spec(SB_WIDTH), tok_spec(SB_WIDTH), tok_spec(SB_WIDTH),
                 tok_spec(SB_WIDTH)]
    in_specs = [tok_spec(D), pl.BlockSpec((bb, 6, D), lambda b, t: (b, 0, 0))] + [_vmem_spec()] * 9
    return pl.pallas_call(
        functools.partial(_inproj_kernel, bb=bb, tm=tm),
        out_shape=out_shape, grid=grid, in_specs=in_specs, out_specs=out_specs,
        compiler_params=pltpu.CompilerParams(dimension_semantics=("arbitrary", "arbitrary"),
                                             vmem_limit_bytes=VMEM_LIMIT),
        name="inproj",
    )(x, mod3, wts["n1w"], wts["wm"], wts["wg"], wts["wgt"], wts["ws"], wts["bgc"], wts["bgr"],
      wts["qnw"], wts["knw"])


def _mlstm_kernel(q_ref, k_ref, v_ref, og_ref, gcol_ref, grow_ref, nw_ref, c0_ref, n0_ref, m0_ref,
                  tri_ref, trit_ref,
                  ya_ref, cout_ref, nout_ref, mout_ref,
                  c_scr, n_scr, m_scr, *, L):
    @pl.when(pl.program_id(1) == 0)
    def _():
        c_scr[...] = c0_ref[0]
        n_scr[...] = n0_ref[0]
        m_scr[...] = m0_ref[0]

    gcol = gcol_ref[0]
    grow = grow_ref[0]
    grow16 = jnp.concatenate([grow, jnp.zeros_like(grow)], axis=0)
    bcol = sum(jnp.dot(tri_ref[...], p, preferred_element_type=F32) for p in _split3(gcol))
    brow = sum(jnp.dot(p, trit_ref[...], preferred_element_type=F32) for p in _split3(grow16))

    r_i = lax.broadcasted_iota(jnp.int32, (L, L), 0)
    c_i = lax.broadcasted_iota(jnp.int32, (L, L), 1)
    causal = c_i <= r_i
    lane1 = lax.broadcasted_iota(jnp.int32, (1, LANES), 1)
    ones = jnp.ones((L, LANES), F32)
    m_vec = m_scr[...]
    m_out = m_vec
    wk_rows, decays = [], []

    for h in range(ML_HEADS):
        hs = slice(h * ML_DIM, (h + 1) * ML_DIM)
        q = q_ref[0, :, hs]
        k = k_ref[0, :, hs]
        v = v_ref[0, :, hs]
        b_c = bcol[:, ML_HEADS + h:ML_HEADS + h + 1]
        ig_c = gcol[:, h:h + 1]
        b_r = brow[ML_HEADS + h:ML_HEADS + h + 1, :]
        ig_r = grow[h:h + 1, :]
        m_h = m_vec[:, h:h + 1]
        b_last = b_r[:, L - 1:L]
        g_r = ig_r - b_r

        dlog = jnp.where(causal, b_c + g_r, -jnp.inf)
        inter = b_c + m_h
        m_t = jnp.maximum(inter, jnp.max(dlog, axis=-1, keepdims=True))
        w_intra = jnp.exp(dlog - m_t)
        w_inter = jnp.exp(inter - m_t)
        qk = lax.dot_general(q, k, NT_DIMS, preferred_element_type=F32)
        s = (qk * w_intra).astype(BF16)
        vaug = jnp.concatenate([v.astype(F32), ones], axis=1)
        cst = c_scr[h]
        tot = (w_inter * jnp.dot(q, cst.astype(BF16), preferred_element_type=F32)
               + jnp.dot(s, vaug.astype(BF16), preferred_element_type=F32))
        num = tot[:, :ML_DIM]
        den = tot[:, ML_DIM:]
        hcell = num / jnp.maximum(jnp.abs(den), jnp.exp(-m_t))
        ya_ref[0, :, hs] = (_rms(hcell, nw_ref[:, hs]) * og_ref[0, :, hs]).astype(BF16)

        wk_r = b_last + g_r
        m_new = jnp.maximum(b_last + m_h, jnp.max(wk_r, axis=-1, keepdims=True))
        decay = jnp.exp(b_last + m_h - m_new)
        wk_c = jnp.exp(b_last - b_c + ig_c - m_new)
        upd = lax.dot_general(k, (wk_c * vaug).astype(BF16), TN_DIMS, preferred_element_type=F32)
        c_new = decay * cst + upd
        c_scr[h] = c_new
        cout_ref[0, h] = c_new[:, :ML_DIM]
        wk_rows.append(jnp.exp(wk_r - m_new))
        decays.append(decay)
        m_out = jnp.where(lane1 == h, m_new, m_out)

    wk16 = jnp.concatenate(wk_rows + [jnp.zeros((16 - ML_HEADS, L), F32)], axis=0).astype(BF16)
    nupd = jnp.dot(wk16, k_ref[0], preferred_element_type=F32)
    for h in range(ML_HEADS):
        hs = slice(h * ML_DIM, (h + 1) * ML_DIM)
        n_new = decays[h] * n_scr[h:h + 1, hs] + nupd[h:h + 1, hs]
        n_scr[h:h + 1, hs] = n_new
        nout_ref[0, h:h + 1, :] = n_new
    m_scr[...] = m_out
    mout_ref[0] = m_out


def _mlstm_call(qa, ka, va, og, gcol, grow, nw, c0, n0, m0, *, L):
    B, T, _ = qa.shape
    tri_np = np.tril(np.ones((L, L), np.float32))
    tri = jnp.asarray(tri_np, BF16)
    trit = jnp.asarray(tri_np.T, BF16)
    tok = lambda b, c: (b, c, 0)
    bat3 = lambda b, c: (b, 0, 0)
    bat4 = lambda b, c: (b, 0, 0, 0)
    in_specs = [pl.BlockSpec((1, L, ML_WIDTH), tok)] * 4 + [
        pl.BlockSpec((1, L, LANES), tok),
        pl.BlockSpec((1, 2 * ML_HEADS, L), lambda b, c: (b, 0, c)),
        _vmem_spec(),
        pl.BlockSpec((1, ML_HEADS, ML_DIM, 2 * ML_DIM), bat4),
        pl.BlockSpec((1, 8, ML_WIDTH), bat3),
        pl.BlockSpec((1, 1, LANES), bat3),
        _vmem_spec(), _vmem_spec()]
    out_shape = [jax.ShapeDtypeStruct((B, T, ML_WIDTH), BF16),
                 jax.ShapeDtypeStruct((B, ML_HEADS, ML_DIM, ML_DIM), F32),
                 jax.ShapeDtypeStruct((B, ML_HEADS, ML_DIM), F32),
                 jax.ShapeDtypeStruct((B, 1, LANES), F32)]
    out_specs = [pl.BlockSpec((1, L, ML_WIDTH), tok),
                 pl.BlockSpec((1, ML_HEADS, ML_DIM, ML_DIM), bat4),
                 pl.BlockSpec((1, ML_HEADS, ML_DIM), bat3),
                 pl.BlockSpec((1, 1, LANES), bat3)]
    return pl.pallas_call(
        functools.partial(_mlstm_kernel, L=L),
        out_shape=out_shape, grid=(B, T // L), in_specs=in_specs, out_specs=out_specs,
        scratch_shapes=[pltpu.VMEM((ML_HEADS, ML_DIM, 2 * ML_DIM), F32),
                        pltpu.VMEM((8, ML_WIDTH), F32),
                        pltpu.VMEM((1, LANES), F32)],
        compiler_params=pltpu.CompilerParams(dimension_semantics=("arbitrary", "arbitrary"),
                                             vmem_limit_bytes=VMEM_LIMIT),
        name="mlstm",
    )(qa, ka, va, og, gcol, grow, nw, c0, n0, m0, tri, trit)


def _suffix_matrix(n):
    u = np.zeros((2 * n, 2 * LANES), np.float32)
    j = np.arange(n)[:, None]
    s = np.arange(n)[None, :]
    u[:n, :n] = (j > s)
    u[n:, :n] = (j > s)
    u[:, LANES:] = 1.0
    return jnp.asarray(u, BF16)


def _sb_tile(q_p, k_p, v_p, u_ref, hm_ref, n, mask):
    tq = q_p.shape[0]
    fused = n % LANES == 0
    hm = [hm_ref[0:1, :], hm_ref[1:2, :]]
    if fused:
        kcat = jnp.concatenate([k_p * hm[0], k_p * hm[1]], axis=0)
        z2 = lax.dot_general(q_p, kcat, NT_DIMS, preferred_element_type=F32)
    a_parts, r_parts = [], []
    for hh in range(2):
        if fused:
            z = z2[:, hh * n:(hh + 1) * n]
        else:
            z = lax.dot_general(q_p * hm[hh], k_p, NT_DIMS, preferred_element_type=F32)
        sp = _softplus(z)
        l1m = -sp
        lb = z - sp
        if mask is not None:
            l1m = jnp.where(mask, l1m, 0.0)
        hi = l1m.astype(BF16)
        lo = (l1m - hi.astype(F32)).astype(BF16)
        if fused:
            res = jnp.dot(jnp.concatenate([hi, lo], axis=1), u_ref[...], preferred_element_type=F32)
        else:
            res = (jnp.dot(hi, u_ref[:n, :], preferred_element_type=F32)
                   + jnp.dot(lo, u_ref[n:, :], preferred_element_type=F32))
        a = jnp.exp(lb + res[:, :n])
        if mask is not None:
            a = jnp.where(mask, a, 0.0)
        a_parts.append(a.astype(BF16))
        r_parts.append(res[:, LANES:])
    if fused:
        vcat = jnp.concatenate([v_p * hm[0], v_p * hm[1]], axis=0)
        o = jnp.dot(jnp.concatenate(a_parts, axis=1), vcat, preferred_element_type=F32)
    else:
        o = (jnp.dot(a_parts[0], v_p * hm[0], preferred_element_type=F32)
             + jnp.dot(a_parts[1], v_p * hm[1], preferred_element_type=F32))
    lane = lax.broadcasted_iota(jnp.int32, (tq, LANES), 1)
    e_r = jnp.exp(jnp.where(lane < SB_DIM, r_parts[0], r_parts[1]))
    return o, e_r


def _sb_kernel(q_ref, kd_ref, vd_ref, kp_ref, vp_ref, ud_ref, up_ref, hm_ref, o_ref, acc_ref,
               *, tq, tk, nprev_static):
    npairs = SB_WIDTH // LANES
    acc_ref[...] = jnp.zeros_like(acc_ref)
    if nprev_static is None:
        nprev = pl.program_id(1) * (tq // tk)
    else:
        nprev = nprev_static

    def body(j, carry):
        start = pl.multiple_of(j * tk, tk)
        kblk = kp_ref[0, pl.ds(start, tk), :]
        vblk = vp_ref[0, pl.ds(start, tk), :]
        for p in range(npairs):
            ps = slice(p * LANES, (p + 1) * LANES)
            o, e_r = _sb_tile(q_ref[0, :, ps], kblk[:, ps], vblk[:, ps], up_ref, hm_ref, tk, None)
            acc_ref[:, ps] = acc_ref[:, ps] * e_r + o
        return carry

    lax.fori_loop(0, nprev, body, 0)

    r_i = lax.broadcasted_iota(jnp.int32, (tq, tq), 0)
    c_i = lax.broadcasted_iota(jnp.int32, (tq, tq), 1)
    mask = c_i < r_i
    for p in range(npairs):
        ps = slice(p * LANES, (p + 1) * LANES)
        o, e_r = _sb_tile(q_ref[0, :, ps], kd_ref[0, :, ps], vd_ref[0, :, ps], ud_ref, hm_ref, tq, mask)
        o_ref[0, :, ps] = acc_ref[:, ps] * e_r + o


def _sb_call(qb, kd, vd, kp, vp, *, tq, tk, self_causal):
    B, T, W = qb.shape
    Tp = kp.shape[1]
    hm_np = np.zeros((16, LANES), np.float32)
    hm_np[0, :SB_DIM] = 1.0
    hm_np[1, SB_DIM:] = 1.0
    hm = jnp.asarray(hm_np, BF16)
    tok = lambda b, i: (b, i, 0)
    bat = lambda b, i: (b, 0, 0)
    in_specs = [pl.BlockSpec((1, tq, W), tok), pl.BlockSpec((1, tq, W), tok), pl.BlockSpec((1, tq, W), tok),
                pl.BlockSpec((1, Tp, W), bat), pl.BlockSpec((1, Tp, W), bat),
                _vmem_spec(), _vmem_spec(), _vmem_spec()]
    return pl.pallas_call(
        functools.partial(_sb_kernel, tq=tq, tk=tk, nprev_static=None if self_causal else Tp // tk),
        out_shape=jax.ShapeDtypeStruct((B, T, W), F32),
        grid=(B, T // tq), in_specs=in_specs, out_specs=pl.BlockSpec((1, tq, W), tok),
        scratch_shapes=[pltpu.VMEM((tq, W), F32)],
        compiler_params=pltpu.CompilerParams(dimension_semantics=("arbitrary", "arbitrary"),
                                             vmem_limit_bytes=VMEM_LIMIT),
        name="stickbreak",
    )(qb, kd, vd, kp, vp, _suffix_matrix(tq), _suffix_matrix(tk), hm)


def _ffn_kernel(x_ref, ya_ref, ob_ref, mod_ref, cst_ref, onw_ref, n2w_ref, wo_ref, wu_ref, wc_ref,
                bc_ref, wd_ref,
                y_ref, cnew_ref, tail_ref, *, bb, tm):
    R = bb * tm

    @pl.when(pl.program_id(1) == 0)
    def _():
        tail_ref[:, :, 6:8, :] = cst_ref[...]

    ycat = []
    for bi in range(bb):
        yb = _rms(ob_ref[bi], onw_ref[...]).astype(BF16)
        ycat.append(jnp.concatenate([ya_ref[bi], yb], axis=1))
    ycat = ycat[0] if bb == 1 else jnp.concatenate(ycat, axis=0)
    mix = jnp.dot(ycat, wo_ref[...], preferred_element_type=F32)

    x1s, h2s = [], []
    for bi in range(bb):
        m = mod_ref[bi]
        x1 = x_ref[bi] + m[2:3, :] * mix[bi * tm:(bi + 1) * tm]
        x1s.append(x1)
        h2s.append((_rms(x1, n2w_ref[...]) * (1.0 + m[4:5, :]) + m[3:4, :]).astype(BF16))
    h2 = h2s[0] if bb == 1 else jnp.concatenate(h2s, axis=0)

    row = lax.broadcasted_iota(jnp.int32, (tm, FF_CHUNK), 0)

    def conv(u, cc):
        wc = wc_ref[cc]
        outs = []
        for bi in range(bb):
            ub = u[bi * tm:(bi + 1) * tm]
            prev = tail_ref[bi, cc, 6:8, :]
            u1 = jnp.where(row == 0, prev[1:2, :], pltpu.roll(ub, 1, 0))
            u2 = jnp.where(row == 0, prev[0:1, :], jnp.where(row == 1, prev[1:2, :], pltpu.roll(ub, 2, 0)))
            outs.append(u2 * wc[0:1, :] + u1 * wc[1:2, :] + ub * wc[2:3, :] + bc_ref[cc])
            tail_ref[bi, cc] = ub[tm - 8:tm]
        return outs[0] if bb == 1 else jnp.concatenate(outs, axis=0)

    acc = jnp.zeros((R, D_MODEL), F32)
    for c in range(N_FF_CHUNKS):
        ua = conv(jnp.dot(h2, wu_ref[c], preferred_element_type=F32), c)
        ug = conv(jnp.dot(h2, wu_ref[N_FF_CHUNKS + c], preferred_element_type=F32), N_FF_CHUNKS + c)
        act = (ua * jax.nn.sigmoid(ua) * ug).astype(BF16)
        acc = acc + jnp.dot(act, wd_ref[c], preferred_element_type=F32)

    for bi in range(bb):
        y_ref[bi] = x1s[bi] + mod_ref[bi][5:6, :] * acc[bi * tm:(bi + 1) * tm]

    @pl.when(pl.program_id(1) == pl.num_programs(1) - 1)
    def _():
        cnew_ref[...] = tail_ref[:, :, 6:8, :]


def _ffn_call(x, ya, ob, mod3, cstate, wts, *, bb, tm):
    B, T, D = x.shape
    nch = 2 * N_FF_CHUNKS
    tok = lambda b, t: (b, t, 0)
    in_specs = [pl.BlockSpec((bb, tm, D), tok), pl.BlockSpec((bb, tm, ML_WIDTH), tok),
                pl.BlockSpec((bb, tm, SB_WIDTH), tok), pl.BlockSpec((bb, 6, D), lambda b, t: (b, 0, 0)),
                pl.BlockSpec((bb, nch, 2, FF_CHUNK), lambda b, t: (b, 0, 0, 0))] + [_vmem_spec()] * 7
    out_shape = [jax.ShapeDtypeStruct((B, T, D), F32), jax.ShapeDtypeStruct((B, nch, 2, FF_CHUNK), F32)]
    out_specs = [pl.BlockSpec((bb, tm, D), tok), pl.BlockSpec((bb, nch, 2, FF_CHUNK), lambda b, t: (b, 0, 0, 0))]
    return pl.pallas_call(
        functools.partial(_ffn_kernel, bb=bb, tm=tm),
        out_shape=out_shape, grid=(B // bb, T // tm), in_specs=in_specs, out_specs=out_specs,
        scratch_shapes=[pltpu.VMEM((bb, nch, 8, FF_CHUNK), F32)],
        compiler_params=pltpu.CompilerParams(dimension_semantics=("arbitrary", "arbitrary"),
                                             vmem_limit_bytes=VMEM_LIMIT),
        name="outproj_convffn",
    )(x, ya, ob, mod3, cstate, wts["onw"], wts["n2w"], wts["wo"], wts["wu"], wts["wc"], wts["bc"], wts["wd"])


def _prep_weights(norm1_w, w_in, b_ig, b_fg, ml_norm_w, sb_q_norm_w, sb_k_norm_w, sb_out_norm_w, w_out,
                  norm2_w, w_up, w_conv, b_conv, w_down):
    nm = 4 * ML_WIDTH
    ng = 2 * ML_HEADS
    w_g = w_in[:, nm:nm + ng]
    b_g = jnp.concatenate([b_ig, b_fg])
    nch = 2 * N_FF_CHUNKS
    wc = jnp.transpose(w_conv.reshape(CONV_W, nch, FF_CHUNK), (1, 0, 2))
    return {
        "n1w": norm1_w.reshape(1, D_MODEL),
        "wm": w_in[:, :nm].astype(BF16),
        "wg": jnp.pad(w_g, ((0, 0), (0, LANES - ng))).astype(BF16),
        "wgt": jnp.pad(w_g.T, ((0, 16 - ng), (0, 0))).astype(BF16),
        "ws": w_in[:, nm + ng:].astype(BF16),
        "bgc": jnp.pad(b_g, (0, LANES - ng)).reshape(1, LANES),
        "bgr": jnp.pad(b_g, (0, 16 - ng)).reshape(16, 1),
        "qnw": jnp.tile(sb_q_norm_w, 2).reshape(1, LANES),
        "knw": jnp.tile(sb_k_norm_w, 2).reshape(1, LANES),
        "mnw": ml_norm_w.reshape(1, ML_WIDTH),
        "onw": sb_out_norm_w.reshape(1, SB_WIDTH),
        "n2w": norm2_w.reshape(1, D_MODEL),
        "wo": w_out.astype(BF16),
        "wu": jnp.transpose(w_up.reshape(D_MODEL, nch, FF_CHUNK), (1, 0, 2)).astype(BF16),
        "wc": jnp.pad(wc, ((0, 0), (0, 8 - CONV_W), (0, 0))),
        "bc": b_conv.reshape(nch, 1, FF_CHUNK),
        "wd": w_down.reshape(N_FF_CHUNKS, FF_CHUNK, D_MODEL).astype(BF16),
    }


def _layer(x, mod3, wts, ml_state, k_past, v_past, conv_state, *, bb, tm, chunk, tq, tk):
    B, T, _ = x.shape
    nch = 2 * N_FF_CHUNKS
    qa, ka, va, og, gcol, grow, qb, kr, vr, kb, vb = _inproj_call(x, mod3, wts, bb=bb, tm=tm)

    if ml_state is None:
        c0 = jnp.zeros((B, ML_HEADS, ML_DIM, 2 * ML_DIM), F32)
        n0 = jnp.zeros((B, 8, ML_WIDTH), F32)
        m0 = jnp.zeros((B, 1, LANES), F32)
    else:
        c_in, n_in, m_in = ml_state
        c0 = jnp.concatenate([c_in, jnp.broadcast_to(n_in[..., None], c_in.shape)], axis=-1)
        n0 = jnp.pad(jnp.broadcast_to(n_in.reshape(B, 1, ML_WIDTH), (B, ML_HEADS, ML_WIDTH)),
                     ((0, 0), (0, 8 - ML_HEADS), (0, 0)))
        m0 = jnp.pad(m_in, ((0, 0), (0, LANES - ML_HEADS))).reshape(B, 1, LANES)
    ya, c_new, n_new, m_new = _mlstm_call(qa, ka, va, og, gcol, grow, wts["mnw"], c0, n0, m0, L=chunk)

    if k_past is None:
        ob = _sb_call(qb, kb, vb, kb, vb, tq=tq, tk=tk, self_causal=True)
    else:
        P = k_past.shape[1]
        ob = _sb_call(qb, kb, vb, k_past.reshape(B, P, SB_WIDTH).astype(BF16),
                      v_past.reshape(B, P, SB_WIDTH).astype(BF16), tq=tq, tk=tk, self_causal=False)

    if conv_state is None:
        cstate = jnp.zeros((B, nch, 2, FF_CHUNK), F32)
    else:
        cstate = jnp.transpose(conv_state.reshape(B, CONV_W - 1, nch, FF_CHUNK), (0, 2, 1, 3))
    y, cnew = _ffn_call(x, ya, ob, mod3, cstate, wts, bb=bb, tm=tm)
    conv_new = jnp.transpose(cnew, (0, 2, 1, 3)).reshape(B, CONV_W - 1, 2 * D_FF)
    return (y, c_new, n_new, m_new[:, 0, :ML_HEADS],
            kr.reshape(B, T, SB_HEADS, SB_DIM), vr.reshape(B, T, SB_HEADS, SB_DIM), conv_new)


def kernel(x_prompt, x_sample, c_prompt, c_sample, state_mlstm_c, state_mlstm_n, state_mlstm_m, cache_sb_k, cache_sb_v, state_ffn_conv, w_ada, b_ada, norm1_w, w_in, b_ig, b_fg, ml_norm_w, sb_q_norm_w, sb_k_norm_w, sb_out_norm_w, w_out, norm2_w, w_up, w_conv, b_conv, w_down):
    depth = w_ada.shape[0]
    B = x_prompt.shape[0]
    y_p, y_s = x_prompt, x_sample
    outs_p = [[] for _ in range(6)]
    outs_s = [[] for _ in range(6)]
    for l in range(depth):
        mod = _ada_call(jnp.concatenate([c_prompt, c_sample], axis=0), w_ada[l], b_ada[l])
        mod3 = mod.reshape(mod.shape[0], 6, D_MODEL)
        wts = _prep_weights(norm1_w[l], w_in[l], b_ig[l], b_fg[l], ml_norm_w[l], sb_q_norm_w[l],
                            sb_k_norm_w[l], sb_out_norm_w[l], w_out[l], norm2_w[l], w_up[l], w_conv[l],
                            b_conv[l], w_down[l])
        res_p = _layer(y_p, mod3[:B], wts, None, None, None, None,
                       bb=1, tm=512, chunk=128, tq=128, tk=128)
        res_s = _layer(y_s, mod3[B:], wts, (state_mlstm_c[l], state_mlstm_n[l], state_mlstm_m[l]),
                       cache_sb_k[l], cache_sb_v[l], state_ffn_conv[l],
                       bb=x_sample.shape[0], tm=x_sample.shape[1], chunk=x_sample.shape[1],
                       tq=x_sample.shape[1], tk=128)
        y_p, y_s = res_p[0], res_s[0]
        for i in range(6):
            outs_p[i].append(res_p[i + 1])
            outs_s[i].append(res_s[i + 1])
    return (y_p, y_s, *[jnp.stack(o) for o in outs_p], *[jnp.stack(o) for o in outs_s])
```

```python
import functools

import numpy as np
import jax
import jax.numpy as jnp
from jax import lax
from jax.experimental import pallas as pl
from jax.experimental.pallas import tpu as pltpu

F32 = jnp.float32
BF16 = jnp.bfloat16

D_MODEL = 1024
ML_HEADS = 4
ML_DIM = 128
ML_WIDTH = ML_HEADS * ML_DIM
SB_HEADS = 8
SB_DIM = 64
SB_WIDTH = SB_HEADS * SB_DIM
D_FF = 2816
CONV_W = 3
EPS = 1e-6
LOG2E = 1.4426950408889634

LANES = 128
FF_CHUNK = 256
N_FF_CHUNKS = D_FF // FF_CHUNK
VMEM_LIMIT = 52 * 1024 * 1024

NT_DIMS = (((1,), (1,)), ((), ()))
TN_DIMS = (((0,), (0,)), ((), ()))


def _vmem_spec():
    return pl.BlockSpec(memory_space=pltpu.VMEM)


def _rms(x, w):
    return x * lax.rsqrt(jnp.mean(x * x, axis=-1, keepdims=True) + EPS) * w


def _softplus(x):
    return jnp.maximum(x, 0.0) + jnp.log(1.0 + jnp.exp(-jnp.abs(x)))


def _split3(x):
    h1 = x.astype(BF16)
    r1 = x - h1.astype(F32)
    h2 = r1.astype(BF16)
    h3 = (r1 - h2.astype(F32)).astype(BF16)
    return h1, h2, h3


def _ada_kernel(c_ref, w_ref, b_ref, o_ref):
    c = c_ref[...]
    s = (c * jax.nn.sigmoid(c)).astype(BF16)
    o_ref[...] = jnp.dot(s, w_ref[...].astype(BF16), preferred_element_type=F32) + b_ref[...]


def _ada_call(c, w_ada, b_ada):
    n, d = c.shape
    nout = w_ada.shape[1]
    return pl.pallas_call(
        _ada_kernel,
        out_shape=jax.ShapeDtypeStruct((n, nout), F32),
        grid=(nout // d,),
        in_specs=[pl.BlockSpec((n, d), lambda j: (0, 0)),
                  pl.BlockSpec((d, d), lambda j: (0, j)),
                  pl.BlockSpec((1, d), lambda j: (0, j))],
        out_specs=pl.BlockSpec((n, d), lambda j: (0, j)),
        compiler_params=pltpu.CompilerParams(dimension_semantics=("arbitrary",)),
        name="adaln",
    )(c, w_ada, b_ada.reshape(1, nout))


def _headnorm64(x, w128):
    lane = lax.broadcasted_iota(jnp.int32, x.shape, 1)
    lo = lane < SB_DIM
    x2 = x * x
    s0 = jnp.sum(jnp.where(lo, x2, 0.0), axis=-1, keepdims=True)
    s1 = jnp.sum(jnp.where(lo, 0.0, x2), axis=-1, keepdims=True)
    r0 = lax.rsqrt(s0 * (1.0 / SB_DIM) + EPS)
    r1 = lax.rsqrt(s1 * (1.0 / SB_DIM) + EPS)
    return x * jnp.where(lo, r0, r1) * w128


def _inproj_kernel(x_ref, mod_ref, n1w_ref, wm_ref, wg_ref, wgt_ref, ws_ref, bgc_ref, bgr_ref,
                   qnw_ref, knw_ref,
                   qa_ref, ka_ref, va_ref, og_ref, gcol_ref, grow_ref, qb_ref, kr_ref, vr_ref,
                   k2_ref, v2_ref, *, bb, tm, kblk):
    hs = []
    for bi in range(bb):
        m = mod_ref[bi]
        y = _rms(x_ref[bi], n1w_ref[...])
        hs.append((y * (1.0 + m[1:2, :]) + m[0:1, :]).astype(BF16))
    h = hs[0] if bb == 1 else jnp.concatenate(hs, axis=0)

    def put(ref, val):
        for bi in range(bb):
            ref[bi] = val[bi * tm:(bi + 1) * tm].astype(ref.dtype)

    def proj(w_ref, j):
        return jnp.dot(h, w_ref[:, j * ML_WIDTH:(j + 1) * ML_WIDTH], preferred_element_type=F32)

    put(qa_ref, proj(wm_ref, 0))
    put(ka_ref, proj(wm_ref, 1) * (ML_DIM ** -0.5))
    put(va_ref, proj(wm_ref, 2))
    put(og_ref, jax.nn.sigmoid(proj(wm_ref, 3)))

    gc = jnp.dot(h, wg_ref[...], preferred_element_type=F32) + bgc_ref[...]
    lane = lax.broadcasted_iota(jnp.int32, gc.shape, 1)
    put(gcol_ref, jnp.where(lane >= ML_HEADS, -_softplus(-gc), gc))
    gr = lax.dot_general(wgt_ref[...], h, NT_DIMS, preferred_element_type=F32) + bgr_ref[...]
    row = lax.broadcasted_iota(jnp.int32, gr.shape, 0)
    gr = jnp.where(row >= ML_HEADS, -_softplus(-gr), gr)
    for bi in range(bb):
        grow_ref[bi] = gr[:2 * ML_HEADS, bi * tm:(bi + 1) * tm]

    sq = proj(ws_ref, 0)
    sk = proj(ws_ref, 1)
    sv = proj(ws_ref, 2)
    qn, kn = [], []
    for j in range(SB_WIDTH // LANES):
        sl = slice(j * LANES, (j + 1) * LANES)
        qn.append(_headnorm64(sq[:, sl], qnw_ref[...]) * (SB_DIM ** -0.5 * LOG2E))
        kn.append(_headnorm64(sk[:, sl], knw_ref[...]))
    qn = jnp.concatenate(qn, axis=1)
    kn = jnp.concatenate(kn, axis=1)
    put(qb_ref, qn)
    put(kr_ref, kn)
    put(vr_ref, sv)

    even = (lax.broadcasted_iota(jnp.int32, kn.shape, 1) & SB_DIM) == 0

    def put_pairs(ref, val):
        ve = jnp.where(even, val, 0.0).astype(BF16)
        vo = jnp.where(even, 0.0, val).astype(BF16)
        for bi in range(bb):
            parts = []
            for r0 in range(bi * tm, (bi + 1) * tm, kblk):
                parts += [ve[r0:r0 + kblk], vo[r0:r0 + kblk]]
            ref[bi] = jnp.concatenate(parts, axis=0)

    put_pairs(k2_ref, kn)
    put_pairs(v2_ref, sv)


def _inproj_call(x, mod3, wts, *, bb, tm, kblk):
    B, T, D = x.shape
    grid = (B // bb, T // tm)
    tok = lambda b, t: (b, t, 0)

    def tok_spec(w):
        return pl.BlockSpec((bb, tm, w), tok)

    def tok_shape(w, dt):
        return jax.ShapeDtypeStruct((B, T, w), dt)

    out_shape = [tok_shape(ML_WIDTH, BF16), tok_shape(ML_WIDTH, BF16), tok_shape(ML_WIDTH, BF16),
                 tok_shape(ML_WIDTH, F32), tok_shape(LANES, F32),
                 jax.ShapeDtypeStruct((B, 2 * ML_HEADS, T), F32),
                 tok_shape(SB_WIDTH, BF16), tok_shape(SB_WIDTH, F32), tok_shape(SB_WIDTH, F32),
                 jax.ShapeDtypeStruct((B, 2 * T, SB_WIDTH), BF16),
                 jax.ShapeDtypeStruct((B, 2 * T, SB_WIDTH), BF16)]
    pair_spec = pl.BlockSpec((bb, 2 * tm, SB_WIDTH), tok)
    out_specs = [tok_spec(ML_WIDTH), tok_spec(ML_WIDTH), tok_spec(ML_WIDTH), tok_spec(ML_WIDTH),
                 tok_spec(LANES), pl.BlockSpec((bb, 2 * ML_HEADS, tm), lambda b, t: (b, 0, t)),
                 tok_spec(SB_WIDTH), tok_spec(SB_WIDTH), tok_spec(SB_WIDTH), pair_spec, pair_spec]
    in_specs = [tok_spec(D), pl.BlockSpec((bb, 6, D), lambda b, t: (b, 0, 0))] + [_vmem_spec()] * 9
    return pl.pallas_call(
        functools.partial(_inproj_kernel, bb=bb, tm=tm, kblk=kblk),
        out_shape=out_shape, grid=grid, in_specs=in_specs, out_specs=out_specs,
        compiler_params=pltpu.CompilerParams(dimension_semantics=("arbitrary", "arbitrary"),
                                             vmem_limit_bytes=VMEM_LIMIT),
        name="inproj",
    )(x, mod3, wts["n1w"], wts["wm"], wts["wg"], wts["wgt"], wts["ws"], wts["bgc"], wts["bgr"],
      wts["qnw"], wts["knw"])


def _mlstm_kernel(q_ref, k_ref, v_ref, og_ref, gcol_ref, grow_ref, nw_ref, c0_ref, n0_ref, m0_ref,
                  tri_ref, trit_ref,
                  ya_ref, cout_ref, nout_ref, mout_ref,
                  c_scr, n_scr, m_scr, *, L):
    @pl.when(pl.program_id(1) == 0)
    def _():
        c_scr[...] = c0_ref[0]
        n_scr[...] = n0_ref[0]
        m_scr[...] = m0_ref[0]

    gcol = gcol_ref[0]
    grow = grow_ref[0]
    grow16 = jnp.concatenate([grow, jnp.zeros_like(grow)], axis=0)
    bcol = sum(jnp.dot(tri_ref[...], p, preferred_element_type=F32) for p in _split3(gcol))
    brow = sum(jnp.dot(p, trit_ref[...], preferred_element_type=F32) for p in _split3(grow16))

    r_i = lax.broadcasted_iota(jnp.int32, (L, L), 0)
    c_i = lax.broadcasted_iota(jnp.int32, (L, L), 1)
    causal = c_i <= r_i
    lane1 = lax.broadcasted_iota(jnp.int32, (1, LANES), 1)
    ones = jnp.ones((L, LANES), F32)
    m_vec = m_scr[...]
    m_out = m_vec
    wk_rows, decays = [], []

    for h in range(ML_HEADS):
        hs = slice(h * ML_DIM, (h + 1) * ML_DIM)
        q = q_ref[0, :, hs]
        k = k_ref[0, :, hs]
        v = v_ref[0, :, hs]
        b_c = bcol[:, ML_HEADS + h:ML_HEADS + h + 1]
        ig_c = gcol[:, h:h + 1]
        b_r = brow[ML_HEADS + h:ML_HEADS + h + 1, :]
        ig_r = grow[h:h + 1, :]
        m_h = m_vec[:, h:h + 1]
        b_last = b_r[:, L - 1:L]
        g_r = ig_r - b_r

        dlog = jnp.where(causal, b_c + g_r, -jnp.inf)
        inter = b_c + m_h
        m_t = jnp.maximum(inter, jnp.max(dlog, axis=-1, keepdims=True))
        w_intra = jnp.exp(dlog - m_t)
        w_inter = jnp.exp(inter - m_t)
        qk = lax.dot_general(q, k, NT_DIMS, preferred_element_type=F32)
        s = (qk * w_intra).astype(BF16)
        vaug = jnp.concatenate([v.astype(F32), ones], axis=1)
        cst = c_scr[h]
        tot = (w_inter * jnp.dot(q, cst.astype(BF16), preferred_element_type=F32)
               + jnp.dot(s, vaug.astype(BF16), preferred_element_type=F32))
        num = tot[:, :ML_DIM]
        den = tot[:, ML_DIM:]
        hcell = num / jnp.maximum(jnp.abs(den), jnp.exp(-m_t))
        ya_ref[0, :, hs] = (_rms(hcell, nw_ref[:, hs]) * og_ref[0, :, hs]).astype(BF16)

        wk_r = b_last + g_r
        m_new = jnp.maximum(b_last + m_h, jnp.max(wk_r, axis=-1, keepdims=True))
        decay = jnp.exp(b_last + m_h - m_new)
        wk_c = jnp.exp(b_last - b_c + ig_c - m_new)
        upd = lax.dot_general(k, (wk_c * vaug).astype(BF16), TN_DIMS, preferred_element_type=F32)
        c_new = decay * cst + upd
        c_scr[h] = c_new
        cout_ref[0, h] = c_new[:, :ML_DIM]
        wk_rows.append(jnp.exp(wk_r - m_new))
        decays.append(decay)
        m_out = jnp.where(lane1 == h, m_new, m_out)

    wk16 = jnp.concatenate(wk_rows + [jnp.zeros((16 - ML_HEADS, L), F32)], axis=0).astype(BF16)
    nupd = jnp.dot(wk16, k_ref[0], preferred_element_type=F32)
    for h in range(ML_HEADS):
        hs = slice(h * ML_DIM, (h + 1) * ML_DIM)
        n_new = decays[h] * n_scr[h:h + 1, hs] + nupd[h:h + 1, hs]
        n_scr[h:h + 1, hs] = n_new
        nout_ref[0, h:h + 1, :] = n_new
    m_scr[...] = m_out
    mout_ref[0] = m_out


def _mlstm_call(qa, ka, va, og, gcol, grow, nw, c0, n0, m0, *, L):
    B, T, _ = qa.shape
    tri_np = np.tril(np.ones((L, L), np.float32))
    tri = jnp.asarray(tri_np, BF16)
    trit = jnp.asarray(tri_np.T, BF16)
    tok = lambda b, c: (b, c, 0)
    bat3 = lambda b, c: (b, 0, 0)
    bat4 = lambda b, c: (b, 0, 0, 0)
    in_specs = [pl.BlockSpec((1, L, ML_WIDTH), tok)] * 4 + [
        pl.BlockSpec((1, L, LANES), tok),
        pl.BlockSpec((1, 2 * ML_HEADS, L), lambda b, c: (b, 0, c)),
        _vmem_spec(),
        pl.BlockSpec((1, ML_HEADS, ML_DIM, 2 * ML_DIM), bat4),
        pl.BlockSpec((1, 8, ML_WIDTH), bat3),
        pl.BlockSpec((1, 1, LANES), bat3),
        _vmem_spec(), _vmem_spec()]
    out_shape = [jax.ShapeDtypeStruct((B, T, ML_WIDTH), BF16),
                 jax.ShapeDtypeStruct((B, ML_HEADS, ML_DIM, ML_DIM), F32),
                 jax.ShapeDtypeStruct((B, ML_HEADS, ML_DIM), F32),
                 jax.ShapeDtypeStruct((B, 1, LANES), F32)]
    out_specs = [pl.BlockSpec((1, L, ML_WIDTH), tok),
                 pl.BlockSpec((1, ML_HEADS, ML_DIM, ML_DIM), bat4),
                 pl.BlockSpec((1, ML_HEADS, ML_DIM), bat3),
                 pl.BlockSpec((1, 1, LANES), bat3)]
    return pl.pallas_call(
        functools.partial(_mlstm_kernel, L=L),
        out_shape=out_shape, grid=(B, T // L), in_specs=in_specs, out_specs=out_specs,
        scratch_shapes=[pltpu.VMEM((ML_HEADS, ML_DIM, 2 * ML_DIM), F32),
                        pltpu.VMEM((8, ML_WIDTH), F32),
                        pltpu.VMEM((1, LANES), F32)],
        compiler_params=pltpu.CompilerParams(dimension_semantics=("arbitrary", "arbitrary"),
                                             vmem_limit_bytes=VMEM_LIMIT),
        name="mlstm",
    )(qa, ka, va, og, gcol, grow, nw, c0, n0, m0, tri, trit)


def _suffix_matrix(n):
    u = np.zeros((2 * n, 2 * LANES), np.float32)
    j = np.arange(n)[:, None]
    s = np.arange(n)[None, :]
    u[:n, :n] = (j >= s)
    u[n:, :n] = (j >= s)
    u[:, LANES:] = 1.0
    return jnp.asarray(u, BF16)


def _sb_tiles(jobs, u_ref, n):
    fused = n % LANES == 0
    zs = []
    for q_p, k2, _, _ in jobs:
        if fused:
            z2 = lax.dot_general(q_p, k2, NT_DIMS, preferred_element_type=F32)
            zs.append([z2[:, :n], z2[:, n:]])
        else:
            zs.append([lax.dot_general(q_p, k2[hh * n:(hh + 1) * n], NT_DIMS, preferred_element_type=F32)
                       for hh in range(2)])
    splits = []
    for (_, _, _, mask), zj in zip(jobs, zs):
        sj = []
        for z in zj:
            s = jnp.maximum(z, 0.0) + jnp.log2(1.0 + jnp.exp2(-jnp.abs(z)))
            if mask is not None:
                s = jnp.where(mask, s, 0.0)
            hi = s.astype(BF16)
            sj.append((hi, (s - hi.astype(F32)).astype(BF16)))
        splits.append(sj)
    weights = []
    for (_, _, _, mask), zj, sj in zip(jobs, zs, splits):
        wj = []
        for z, (hi, lo) in zip(zj, sj):
            if fused:
                res = jnp.dot(jnp.concatenate([hi, lo], axis=1), u_ref[...], preferred_element_type=F32)
            else:
                res = (jnp.dot(hi, u_ref[:n, :], preferred_element_type=F32)
                       + jnp.dot(lo, u_ref[n:, :], preferred_element_type=F32))
            a = jnp.exp2(z - res[:, :n])
            if mask is not None:
                a = jnp.where(mask, a, 0.0)
            wj.append((a.astype(BF16), res[:, LANES:]))
        weights.append(wj)
    outs = []
    for (q_p, _, v2, _), wj in zip(jobs, weights):
        if fused:
            o = jnp.dot(jnp.concatenate([wj[0][0], wj[1][0]], axis=1), v2, preferred_element_type=F32)
        else:
            o = (jnp.dot(wj[0][0], v2[:n], preferred_element_type=F32)
                 + jnp.dot(wj[1][0], v2[n:], preferred_element_type=F32))
        lane = lax.broadcasted_iota(jnp.int32, (q_p.shape[0], LANES), 1)
        outs.append((o, jnp.exp2(-jnp.where(lane < SB_DIM, wj[0][1], wj[1][1]))))
    return outs


def _sb_kernel(q_ref, kd_ref, vd_ref, kp_ref, vp_ref, ud_ref, up_ref, o_ref, acc_ref,
               *, tq, tk, nd, kpi, ntrip_static):
    npairs = SB_WIDTH // LANES
    acc_ref[...] = jnp.zeros_like(acc_ref)
    ntrip = pl.program_id(1) if ntrip_static is None else ntrip_static

    pss = [slice(p * LANES, (p + 1) * LANES) for p in range(npairs)]

    def body(j, carry):
        jobs = []
        for kk in range(kpi):
            start = pl.multiple_of((j * kpi + kk) * 2 * tk, 2 * tk)
            for ps in pss:
                jobs.append((q_ref[0, :, ps], kp_ref[0, pl.ds(start, 2 * tk), ps],
                             vp_ref[0, pl.ds(start, 2 * tk), ps], None))
        for idx, (o, e_r) in enumerate(_sb_tiles(jobs, up_ref, tk)):
            ps = pss[idx % npairs]
            acc_ref[:, ps] = acc_ref[:, ps] * e_r + o
        return carry

    lax.fori_loop(0, ntrip, body, 0)

    r_i = lax.broadcasted_iota(jnp.int32, (tq, nd), 0)
    c_i = lax.broadcasted_iota(jnp.int32, (tq, nd), 1)
    nsub = tq // nd
    jobs = []
    for d in range(nsub):
        for ps in pss:
            jobs.append((q_ref[0, :, ps], kd_ref[0, 2 * d * nd:2 * (d + 1) * nd, ps],
                         vd_ref[0, 2 * d * nd:2 * (d + 1) * nd, ps], c_i + d * nd < r_i))
    for idx, (o, e_r) in enumerate(_sb_tiles(jobs, ud_ref, nd)):
        ps = pss[idx % npairs]
        if idx // npairs == nsub - 1:
            o_ref[0, :, ps] = acc_ref[:, ps] * e_r + o
        else:
            acc_ref[:, ps] = acc_ref[:, ps] * e_r + o


def _sb_call(qb, kd, vd, kp, vp, *, tq, tk, self_causal):
    B, T, W = qb.shape
    Tp = kp.shape[1] // 2
    nd = min(tq, tk)
    kpi = tq // tk if self_causal else 4
    tok = lambda b, i: (b, i, 0)
    bat = lambda b, i: (b, 0, 0)
    in_specs = [pl.BlockSpec((1, tq, W), tok), pl.BlockSpec((1, 2 * tq, W), tok), pl.BlockSpec((1, 2 * tq, W), tok),
                pl.BlockSpec((1, 2 * Tp, W), bat), pl.BlockSpec((1, 2 * Tp, W), bat),
                _vmem_spec(), _vmem_spec()]
    return pl.pallas_call(
        functools.partial(_sb_kernel, tq=tq, tk=tk, nd=nd, kpi=kpi,
                          ntrip_static=None if self_causal else Tp // (tk * kpi)),
        out_shape=jax.ShapeDtypeStruct((B, T, W), F32),
        grid=(B, T // tq), in_specs=in_specs, out_specs=pl.BlockSpec((1, tq, W), tok),
        scratch_shapes=[pltpu.VMEM((tq, W), F32)],
        compiler_params=pltpu.CompilerParams(dimension_semantics=("arbitrary", "arbitrary"),
                                             vmem_limit_bytes=VMEM_LIMIT),
        name="stickbreak",
    )(qb, kd, vd, kp, vp, _suffix_matrix(nd), _suffix_matrix(tk))


def _ffn_kernel(x_ref, ya_ref, ob_ref, mod_ref, cst_ref, onw_ref, n2w_ref, wo_ref, wu_ref, wc_ref,
                bc_ref, wd_ref,
                y_ref, cnew_ref, tail_ref, *, bb, tm):
    R = bb * tm

    @pl.when(pl.program_id(1) == 0)
    def _():
        tail_ref[:, :, 6:8, :] = cst_ref[...]

    ycat = []
    for bi in range(bb):
        yb = _rms(ob_ref[bi], onw_ref[...]).astype(BF16)
        ycat.append(jnp.concatenate([ya_ref[bi], yb], axis=1))
    ycat = ycat[0] if bb == 1 else jnp.concatenate(ycat, axis=0)
    mix = jnp.dot(ycat, wo_ref[...], preferred_element_type=F32)

    x1s, h2s = [], []
    for bi in range(bb):
        m = mod_ref[bi]
        x1 = x_ref[bi] + m[2:3, :] * mix[bi * tm:(bi + 1) * tm]
        x1s.append(x1)
        h2s.append((_rms(x1, n2w_ref[...]) * (1.0 + m[4:5, :]) + m[3:4, :]).astype(BF16))
    h2 = h2s[0] if bb == 1 else jnp.concatenate(h2s, axis=0)

    row = lax.broadcasted_iota(jnp.int32, (tm, FF_CHUNK), 0)

    def conv(u, cc):
        wc = wc_ref[cc]
        outs = []
        for bi in range(bb):
            ub = u[bi * tm:(bi + 1) * tm]
            prev = tail_ref[bi, cc, 6:8, :]
            u1 = jnp.where(row == 0, prev[1:2, :], pltpu.roll(ub, 1, 0))
            u2 = jnp.where(row == 0, prev[0:1, :], jnp.where(row == 1, prev[1:2, :], pltpu.roll(ub, 2, 0)))
            outs.append(u2 * wc[0:1, :] + u1 * wc[1:2, :] + ub * wc[2:3, :] + bc_ref[cc])
            tail_ref[bi, cc] = ub[tm - 8:tm]
        return outs[0] if bb == 1 else jnp.concatenate(outs, axis=0)

    acc = jnp.zeros((R, D_MODEL), F32)
    for c in range(N_FF_CHUNKS):
        ua = conv(jnp.dot(h2, wu_ref[c], preferred_element_type=F32), c)
        ug = conv(jnp.dot(h2, wu_ref[N_FF_CHUNKS + c], preferred_element_type=F32), N_FF_CHUNKS + c)
        act = (ua * jax.nn.sigmoid(ua) * ug).astype(BF16)
        acc = acc + jnp.dot(act, wd_ref[c], preferred_element_type=F32)

    for bi in range(bb):
        y_ref[bi] = x1s[bi] + mod_ref[bi][5:6, :] * acc[bi * tm:(bi + 1) * tm]

    @pl.when(pl.program_id(1) == pl.num_programs(1) - 1)
    def _():
        cnew_ref[...] = tail_ref[:, :, 6:8, :]


def _ffn_call(x, ya, ob, mod3, cstate, wts, *, bb, tm):
    B, T, D = x.shape
    nch = 2 * N_FF_CHUNKS
    tok = lambda b, t: (b, t, 0)
    in_specs = [pl.BlockSpec((bb, tm, D), tok), pl.BlockSpec((bb, tm, ML_WIDTH), tok),
                pl.BlockSpec((bb, tm, SB_WIDTH), tok), pl.BlockSpec((bb, 6, D), lambda b, t: (b, 0, 0)),
                pl.BlockSpec((bb, nch, 2, FF_CHUNK), lambda b, t: (b, 0, 0, 0))] + [_vmem_spec()] * 7
    out_shape = [jax.ShapeDtypeStruct((B, T, D), F32), jax.ShapeDtypeStruct((B, nch, 2, FF_CHUNK), F32)]
    out_specs = [pl.BlockSpec((bb, tm, D), tok), pl.BlockSpec((bb, nch, 2, FF_CHUNK), lambda b, t: (b, 0, 0, 0))]
    return pl.pallas_call(
        functools.partial(_ffn_kernel, bb=bb, tm=tm),
        out_shape=out_shape, grid=(B // bb, T // tm), in_specs=in_specs, out_specs=out_specs,
        scratch_shapes=[pltpu.VMEM((bb, nch, 8, FF_CHUNK), F32)],
        compiler_params=pltpu.CompilerParams(dimension_semantics=("arbitrary", "arbitrary"),
                                             vmem_limit_bytes=VMEM_LIMIT),
        name="outproj_convffn",
    )(x, ya, ob, mod3, cstate, wts["onw"], wts["n2w"], wts["wo"], wts["wu"], wts["wc"], wts["bc"], wts["wd"])


def _prep_weights(norm1_w, w_in, b_ig, b_fg, ml_norm_w, sb_q_norm_w, sb_k_norm_w, sb_out_norm_w, w_out,
                  norm2_w, w_up, w_conv, b_conv, w_down):
    nm = 4 * ML_WIDTH
    ng = 2 * ML_HEADS
    w_g = w_in[:, nm:nm + ng]
    b_g = jnp.concatenate([b_ig, b_fg])
    nch = 2 * N_FF_CHUNKS
    wc = jnp.transpose(w_conv.reshape(CONV_W, nch, FF_CHUNK), (1, 0, 2))
    return {
        "n1w": norm1_w.reshape(1, D_MODEL),
        "wm": w_in[:, :nm].astype(BF16),
        "wg": jnp.pad(w_g, ((0, 0), (0, LANES - ng))).astype(BF16),
        "wgt": jnp.pad(w_g.T, ((0, 16 - ng), (0, 0))).astype(BF16),
        "ws": w_in[:, nm + ng:].astype(BF16),
        "bgc": jnp.pad(b_g, (0, LANES - ng)).reshape(1, LANES),
        "bgr": jnp.pad(b_g, (0, 16 - ng)).reshape(16, 1),
        "qnw": jnp.tile(sb_q_norm_w, 2).reshape(1, LANES),
        "knw": jnp.tile(sb_k_norm_w, 2).reshape(1, LANES),
        "mnw": ml_norm_w.reshape(1, ML_WIDTH),
        "onw": sb_out_norm_w.reshape(1, SB_WIDTH),
        "n2w": norm2_w.reshape(1, D_MODEL),
        "wo": w_out.astype(BF16),
        "wu": jnp.transpose(w_up.reshape(D_MODEL, nch, FF_CHUNK), (1, 0, 2)).astype(BF16),
        "wc": jnp.pad(wc, ((0, 0), (0, 8 - CONV_W), (0, 0))),
        "bc": b_conv.reshape(nch, 1, FF_CHUNK),
        "wd": w_down.reshape(N_FF_CHUNKS, FF_CHUNK, D_MODEL).astype(BF16),
    }


def _pair_rows(cache, blk):
    B, P, H, Dh = cache.shape
    c = cache.reshape(B, P // blk, 1, blk, H, Dh)
    even = (jnp.arange(H) % 2 == 0).reshape(1, 1, 1, 1, H, 1)
    both = jnp.concatenate([jnp.where(even, c, 0.0), jnp.where(even, 0.0, c)], axis=2)
    return both.reshape(B, 2 * P, H * Dh).astype(BF16)


def _layer(x, mod3, wts, ml_state, k_past, v_past, conv_state, *, bb, tm, chunk, tq, tk):
    B, T, _ = x.shape
    nch = 2 * N_FF_CHUNKS
    qa, ka, va, og, gcol, grow, qb, kr, vr, k2, v2 = _inproj_call(x, mod3, wts, bb=bb, tm=tm,
                                                                  kblk=min(tq, tk))

    if ml_state is None:
        c0 = jnp.zeros((B, ML_HEADS, ML_DIM, 2 * ML_DIM), F32)
        n0 = jnp.zeros((B, 8, ML_WIDTH), F32)
        m0 = jnp.zeros((B, 1, LANES), F32)
    else:
        c_in, n_in, m_in = ml_state
        c0 = jnp.concatenate([c_in, jnp.broadcast_to(n_in[..., None], c_in.shape)], axis=-1)
        n0 = jnp.pad(jnp.broadcast_to(n_in.reshape(B, 1, ML_WIDTH), (B, ML_HEADS, ML_WIDTH)),
                     ((0, 0), (0, 8 - ML_HEADS), (0, 0)))
        m0 = jnp.pad(m_in, ((0, 0), (0, LANES - ML_HEADS))).reshape(B, 1, LANES)
    ya, c_new, n_new, m_new = _mlstm_call(qa, ka, va, og, gcol, grow, wts["mnw"], c0, n0, m0, L=chunk)

    if k_past is None:
        ob = _sb_call(qb, k2, v2, k2, v2, tq=tq, tk=tk, self_causal=True)
    else:
        ob = _sb_call(qb, k2, v2, _pair_rows(k_past, tk), _pair_rows(v_past, tk),
                      tq=tq, tk=tk, self_causal=False)

    if conv_state is None:
        cstate = jnp.zeros((B, nch, 2, FF_CHUNK), F32)
    else:
        cstate = jnp.transpose(conv_state.reshape(B, CONV_W - 1, nch, FF_CHUNK), (0, 2, 1, 3))
    y, cnew = _ffn_call(x, ya, ob, mod3, cstate, wts, bb=bb, tm=tm)
    conv_new = jnp.transpose(cnew, (0, 2, 1, 3)).reshape(B, CONV_W - 1, 2 * D_FF)
    return (y, c_new, n_new, m_new[:, 0, :ML_HEADS],
            kr.reshape(B, T, SB_HEADS, SB_DIM), vr.reshape(B, T, SB_HEADS, SB_DIM), conv_new)


def kernel(x_prompt, x_sample, c_prompt, c_sample, state_mlstm_c, state_mlstm_n, state_mlstm_m, cache_sb_k, cache_sb_v, state_ffn_conv, w_ada, b_ada, norm1_w, w_in, b_ig, b_fg, ml_norm_w, sb_q_norm_w, sb_k_norm_w, sb_out_norm_w, w_out, norm2_w, w_up, w_conv, b_conv, w_down):
    depth = w_ada.shape[0]
    B = x_prompt.shape[0]
    y_p, y_s = x_prompt, x_sample
    outs_p = [[] for _ in range(6)]
    outs_s = [[] for _ in range(6)]
    for l in range(depth):
        mod = _ada_call(jnp.concatenate([c_prompt, c_sample], axis=0), w_ada[l], b_ada[l])
        mod3 = mod.reshape(mod.shape[0], 6, D_MODEL)
        wts = _prep_weights(norm1_w[l], w_in[l], b_ig[l], b_fg[l], ml_norm_w[l], sb_q_norm_w[l],
                            sb_k_norm_w[l], sb_out_norm_w[l], w_out[l], norm2_w[l], w_up[l], w_conv[l],
                            b_conv[l], w_down[l])
        res_p = _layer(y_p, mod3[:B], wts, None, None, None, None,
                       bb=1, tm=512, chunk=128, tq=256, tk=128)
        res_s = _layer(y_s, mod3[B:], wts, (state_mlstm_c[l], state_mlstm_n[l], state_mlstm_m[l]),
                       cache_sb_k[l], cache_sb_v[l], state_ffn_conv[l],
                       bb=x_sample.shape[0], tm=x_sample.shape[1], chunk=x_sample.shape[1],
                       tq=x_sample.shape[1], tk=128)
        y_p, y_s = res_p[0], res_s[0]
        for i in range(6):
            outs_p[i].append(res_p[i + 1])
            outs_s[i].append(res_s[i + 1])
    return (y_p, y_s, *[jnp.stack(o) for o in outs_p], *[jnp.stack(o) for o in outs_s])
```

```python
import functools

import numpy as np
import jax
import jax.numpy as jnp
from jax import lax
from jax.experimental import pallas as pl
from jax.experimental.pallas import tpu as pltpu

F32 = jnp.float32
BF16 = jnp.bfloat16

D_MODEL = 1024
ML_HEADS = 4
ML_DIM = 128
ML_WIDTH = ML_HEADS * ML_DIM
SB_HEADS = 8
SB_DIM = 64
SB_WIDTH = SB_HEADS * SB_DIM
D_FF = 2816
CONV_W = 3
EPS = 1e-6
LOG2E = 1.4426950408889634

LANES = 128
FF_CHUNK = 256
N_FF_CHUNKS = D_FF // FF_CHUNK
VMEM_LIMIT = 52 * 1024 * 1024

NT_DIMS = (((1,), (1,)), ((), ()))
TN_DIMS = (((0,), (0,)), ((), ()))


def _vmem_spec():
    return pl.BlockSpec(memory_space=pltpu.VMEM)


def _rms(x, w):
    return x * lax.rsqrt(jnp.mean(x * x, axis=-1, keepdims=True) + EPS) * w


def _softplus(x):
    return jnp.maximum(x, 0.0) + jnp.log(1.0 + jnp.exp(-jnp.abs(x)))


def _split3(x):
    h1 = x.astype(BF16)
    r1 = x - h1.astype(F32)
    h2 = r1.astype(BF16)
    h3 = (r1 - h2.astype(F32)).astype(BF16)
    return h1, h2, h3


def _ada_kernel(c_ref, w_ref, b_ref, o_ref):
    c = c_ref[...]
    s = (c * jax.nn.sigmoid(c)).astype(BF16)
    o_ref[...] = jnp.dot(s, w_ref[...].astype(BF16), preferred_element_type=F32) + b_ref[...]


def _ada_call(c, w_ada, b_ada):
    n, d = c.shape
    nout = w_ada.shape[1]
    return pl.pallas_call(
        _ada_kernel,
        out_shape=jax.ShapeDtypeStruct((n, nout), F32),
        grid=(nout // d,),
        in_specs=[pl.BlockSpec((n, d), lambda j: (0, 0)),
                  pl.BlockSpec((d, d), lambda j: (0, j)),
                  pl.BlockSpec((1, d), lambda j: (0, j))],
        out_specs=pl.BlockSpec((n, d), lambda j: (0, j)),
        compiler_params=pltpu.CompilerParams(dimension_semantics=("arbitrary",)),
        name="adaln",
    )(c, w_ada, b_ada.reshape(1, nout))


def _headnorm64(x, w128):
    lane = lax.broadcasted_iota(jnp.int32, x.shape, 1)
    lo = lane < SB_DIM
    x2 = x * x
    s0 = jnp.sum(jnp.where(lo, x2, 0.0), axis=-1, keepdims=True)
    s1 = jnp.sum(jnp.where(lo, 0.0, x2), axis=-1, keepdims=True)
    r0 = lax.rsqrt(s0 * (1.0 / SB_DIM) + EPS)
    r1 = lax.rsqrt(s1 * (1.0 / SB_DIM) + EPS)
    return x * jnp.where(lo, r0, r1) * w128


def _inproj_kernel(x_ref, mod_ref, n1w_ref, wm_ref, wg_ref, wgt_ref, ws_ref, bgc_ref, bgr_ref,
                   qnw_ref, knw_ref,
                   qa_ref, ka_ref, va_ref, og_ref, gcol_ref, grow_ref, qb_ref, kr_ref, vr_ref,
                   k2_ref, v2_ref, *, bb, tm, kblk):
    hs = []
    for bi in range(bb):
        m = mod_ref[bi]
        y = _rms(x_ref[bi], n1w_ref[...])
        hs.append((y * (1.0 + m[1:2, :]) + m[0:1, :]).astype(BF16))
    h = hs[0] if bb == 1 else jnp.concatenate(hs, axis=0)

    def put(ref, val):
        for bi in range(bb):
            ref[bi] = val[bi * tm:(bi + 1) * tm].astype(ref.dtype)

    def proj(w_ref, j):
        return jnp.dot(h, w_ref[:, j * ML_WIDTH:(j + 1) * ML_WIDTH], preferred_element_type=F32)

    put(qa_ref, proj(wm_ref, 0))
    put(ka_ref, proj(wm_ref, 1) * (ML_DIM ** -0.5))
    put(va_ref, proj(wm_ref, 2))
    put(og_ref, jax.nn.sigmoid(proj(wm_ref, 3)))

    gc = jnp.dot(h, wg_ref[...], preferred_element_type=F32) + bgc_ref[...]
    lane = lax.broadcasted_iota(jnp.int32, gc.shape, 1)
    put(gcol_ref, jnp.where(lane >= ML_HEADS, -_softplus(-gc), gc))
    gr = lax.dot_general(wgt_ref[...], h, NT_DIMS, preferred_element_type=F32) + bgr_ref[...]
    row = lax.broadcasted_iota(jnp.int32, gr.shape, 0)
    gr = jnp.where(row >= ML_HEADS, -_softplus(-gr), gr)
    for bi in range(bb):
        grow_ref[bi] = gr[:2 * ML_HEADS, bi * tm:(bi + 1) * tm]

    sq = proj(ws_ref, 0)
    sk = proj(ws_ref, 1)
    sv = proj(ws_ref, 2)
    qn, kn = [], []
    for j in range(SB_WIDTH // LANES):
        sl = slice(j * LANES, (j + 1) * LANES)
        qn.append(_headnorm64(sq[:, sl], qnw_ref[...]) * (SB_DIM ** -0.5 * LOG2E))
        kn.append(_headnorm64(sk[:, sl], knw_ref[...]))
    qn = jnp.concatenate(qn, axis=1)
    kn = jnp.concatenate(kn, axis=1)
    put(qb_ref, qn)
    put(kr_ref, kn)
    put(vr_ref, sv)

    even = (lax.broadcasted_iota(jnp.int32, kn.shape, 1) & SB_DIM) == 0

    def put_pairs(ref, val):
        ve = jnp.where(even, val, 0.0).astype(BF16)
        vo = jnp.where(even, 0.0, val).astype(BF16)
        for bi in range(bb):
            parts = []
            for r0 in range(bi * tm, (bi + 1) * tm, kblk):
                parts += [ve[r0:r0 + kblk], vo[r0:r0 + kblk]]
            ref[bi] = jnp.concatenate(parts, axis=0)

    put_pairs(k2_ref, kn)
    put_pairs(v2_ref, sv)


def _inproj_call(x, mod3, wts, *, bb, tm, kblk):
    B, T, D = x.shape
    grid = (B // bb, T // tm)
    tok = lambda b, t: (b, t, 0)

    def tok_spec(w):
        return pl.BlockSpec((bb, tm, w), tok)

    def tok_shape(w, dt):
        return jax.ShapeDtypeStruct((B, T, w), dt)

    out_shape = [tok_shape(ML_WIDTH, BF16), tok_shape(ML_WIDTH, BF16), tok_shape(ML_WIDTH, BF16),
                 tok_shape(ML_WIDTH, F32), tok_shape(LANES, F32),
                 jax.ShapeDtypeStruct((B, 2 * ML_HEADS, T), F32),
                 tok_shape(SB_WIDTH, BF16), tok_shape(SB_WIDTH, F32), tok_shape(SB_WIDTH, F32),
                 jax.ShapeDtypeStruct((B, 2 * T, SB_WIDTH), BF16),
                 jax.ShapeDtypeStruct((B, 2 * T, SB_WIDTH), BF16)]
    pair_spec = pl.BlockSpec((bb, 2 * tm, SB_WIDTH), tok)
    out_specs = [tok_spec(ML_WIDTH), tok_spec(ML_WIDTH), tok_spec(ML_WIDTH), tok_spec(ML_WIDTH),
                 tok_spec(LANES), pl.BlockSpec((bb, 2 * ML_HEADS, tm), lambda b, t: (b, 0, t)),
                 tok_spec(SB_WIDTH), tok_spec(SB_WIDTH), tok_spec(SB_WIDTH), pair_spec, pair_spec]
    in_specs = [tok_spec(D), pl.BlockSpec((bb, 6, D), lambda b, t: (b, 0, 0))] + [_vmem_spec()] * 9
    return pl.pallas_call(
        functools.partial(_inproj_kernel, bb=bb, tm=tm, kblk=kblk),
        out_shape=out_shape, grid=grid, in_specs=in_specs, out_specs=out_specs,
        compiler_params=pltpu.CompilerParams(dimension_semantics=("arbitrary", "arbitrary"),
                                             vmem_limit_bytes=VMEM_LIMIT),
        name="inproj",
    )(x, mod3, wts["n1w"], wts["wm"], wts["wg"], wts["wgt"], wts["ws"], wts["bgc"], wts["bgr"],
      wts["qnw"], wts["knw"])


def _mlstm_kernel(q_ref, k_ref, v_ref, og_ref, gcol_ref, grow_ref, nw_ref, c0_ref, n0_ref, m0_ref,
                  tri_ref, trit_ref,
                  ya_ref, cout_ref, nout_ref, mout_ref,
                  c_scr, n_scr, m_scr, *, L, cps):
    @pl.when(pl.program_id(1) == 0)
    def _():
        c_scr[...] = c0_ref[0]
        n_scr[...] = n0_ref[0]
        m_scr[...] = m0_ref[0]

    heads = range(ML_HEADS)
    hss = [slice(h * ML_DIM, (h + 1) * ML_DIM) for h in heads]
    rowss = [slice(cc * L, (cc + 1) * L) for cc in range(cps)]
    r_i = lax.broadcasted_iota(jnp.int32, (L, L), 0)
    c_i = lax.broadcasted_iota(jnp.int32, (L, L), 1)
    causal = c_i <= r_i
    lane1 = lax.broadcasted_iota(jnp.int32, (1, LANES), 1)
    ones = jnp.ones((L, LANES), BF16)

    gates, qks = [], []
    for rows in rowss:
        gcol = gcol_ref[0, rows, :]
        grow = grow_ref[0, :, rows]
        grow16 = jnp.concatenate([grow, jnp.zeros_like(grow)], axis=0)
        bcol = sum(jnp.dot(tri_ref[...], p, preferred_element_type=F32) for p in _split3(gcol))
        brow = sum(jnp.dot(p, trit_ref[...], preferred_element_type=F32) for p in _split3(grow16))
        gates.append((gcol, grow, bcol, brow))
    for rows in rowss:
        qks.append([lax.dot_general(q_ref[0, rows, hs], k_ref[0, rows, hs], NT_DIMS,
                                    preferred_element_type=F32) for hs in hss])

    m_vec = m_scr[...]
    m_run = [m_vec[:, h:h + 1] for h in heads]
    plan = []
    for cc, rows in enumerate(rowss):
        gcol, grow, bcol, brow = gates[cc]
        per_head = []
        for h in heads:
            b_c = jnp.broadcast_to(bcol[:, ML_HEADS + h:ML_HEADS + h + 1], (L, LANES))
            ig_c = jnp.broadcast_to(gcol[:, h:h + 1], (L, LANES))
            b_r = brow[ML_HEADS + h:ML_HEADS + h + 1, :]
            ig_r = grow[h:h + 1, :]
            b_last = b_r[:, L - 1:L]
            g_r = ig_r - b_r
            m_h = m_run[h]
            dlog = jnp.where(causal, b_c[:, :L] + g_r, -jnp.inf)
            inter = b_c + m_h
            m_t = jnp.maximum(inter, jnp.max(dlog, axis=-1, keepdims=True))
            w_intra = jnp.exp(dlog - m_t[:, :L])
            wk_r = b_last + g_r
            m_new = jnp.maximum(b_last + m_h, jnp.max(wk_r, axis=-1, keepdims=True))
            wk_c = jnp.exp(b_last - b_c + ig_c - m_new)
            v = v_ref[0, rows, hss[h]]
            per_head.append({
                "s": (qks[cc][h] * w_intra).astype(BF16),
                "w_inter": jnp.exp(inter - m_t),
                "floor": jnp.exp(-m_t),
                "decay": jnp.exp(b_last + m_h - m_new),
                "vaug": jnp.concatenate([v, ones], axis=1),
                "wv": jnp.concatenate([(wk_c * v.astype(F32)).astype(BF16), wk_c.astype(BF16)], axis=1),
                "wk_row": jnp.exp(wk_r - m_new),
            })
            m_run[h] = m_new
        plan.append(per_head)

    for cc, rows in enumerate(rowss):
        for h in heads:
            ph = plan[cc][h]
            ph["intra"] = jnp.dot(ph["s"], ph["vaug"], preferred_element_type=F32)
            ph["upd"] = lax.dot_general(k_ref[0, rows, hss[h]], ph["wv"], TN_DIMS, preferred_element_type=F32)
    nupds = []
    for cc, rows in enumerate(rowss):
        wk16 = jnp.concatenate([plan[cc][h]["wk_row"] for h in heads]
                               + [jnp.zeros((16 - ML_HEADS, L), F32)], axis=0).astype(BF16)
        nupds.append(jnp.dot(wk16, k_ref[0, rows, :], preferred_element_type=F32))

    for cc, rows in enumerate(rowss):
        for h in heads:
            ph = plan[cc][h]
            hs = hss[h]
            cst = c_scr[h]
            qc = jnp.dot(q_ref[0, rows, hs], cst.astype(BF16), preferred_element_type=F32)
            num = ph["w_inter"] * qc[:, :ML_DIM] + ph["intra"][:, :ML_DIM]
            den = ph["w_inter"] * qc[:, ML_DIM:] + ph["intra"][:, ML_DIM:]
            hcell = num / jnp.maximum(jnp.abs(den), ph["floor"])
            ya_ref[0, rows, hs] = (_rms(hcell, nw_ref[:, hs]) * og_ref[0, rows, hs]).astype(BF16)
            c_scr[h] = ph["decay"] * cst + ph["upd"]
            n_scr[h:h + 1, hs] = ph["decay"] * n_scr[h:h + 1, hs] + nupds[cc][h:h + 1, hs]

    m_out = m_vec
    for h in heads:
        cout_ref[0, h] = c_scr[h][:, :ML_DIM]
        nout_ref[0, h:h + 1, :] = n_scr[h:h + 1, hss[h]]
        m_out = jnp.where(lane1 == h, m_run[h], m_out)
    m_scr[...] = m_out
    mout_ref[0] = m_out


def _mlstm_call(qa, ka, va, og, gcol, grow, nw, c0, n0, m0, *, L, cps):
    B, T, _ = qa.shape
    tri_np = np.tril(np.ones((L, L), np.float32))
    tri = jnp.asarray(tri_np, BF16)
    trit = jnp.asarray(tri_np.T, BF16)
    tok = lambda b, c: (b, c, 0)
    bat3 = lambda b, c: (b, 0, 0)
    bat4 = lambda b, c: (b, 0, 0, 0)
    tl = L * cps
    in_specs = [pl.BlockSpec((1, tl, ML_WIDTH), tok)] * 4 + [
        pl.BlockSpec((1, tl, LANES), tok),
        pl.BlockSpec((1, 2 * ML_HEADS, tl), lambda b, c: (b, 0, c)),
        _vmem_spec(),
        pl.BlockSpec((1, ML_HEADS, ML_DIM, 2 * ML_DIM), bat4),
        pl.BlockSpec((1, 8, ML_WIDTH), bat3),
        pl.BlockSpec((1, 1, LANES), bat3),
        _vmem_spec(), _vmem_spec()]
    out_shape = [jax.ShapeDtypeStruct((B, T, ML_WIDTH), BF16),
                 jax.ShapeDtypeStruct((B, ML_HEADS, ML_DIM, ML_DIM), F32),
                 jax.ShapeDtypeStruct((B, ML_HEADS, ML_DIM), F32),
                 jax.ShapeDtypeStruct((B, 1, LANES), F32)]
    out_specs = [pl.BlockSpec((1, tl, ML_WIDTH), tok),
                 pl.BlockSpec((1, ML_HEADS, ML_DIM, ML_DIM), bat4),
                 pl.BlockSpec((1, ML_HEADS, ML_DIM), bat3),
                 pl.BlockSpec((1, 1, LANES), bat3)]
    return pl.pallas_call(
        functools.partial(_mlstm_kernel, L=L, cps=cps),
        out_shape=out_shape, grid=(B, T // tl), in_specs=in_specs, out_specs=out_specs,
        scratch_shapes=[pltpu.VMEM((ML_HEADS, ML_DIM, 2 * ML_DIM), F32),
                        pltpu.VMEM((8, ML_WIDTH), F32),
                        pltpu.VMEM((1, LANES), F32)],
        compiler_params=pltpu.CompilerParams(dimension_semantics=("arbitrary", "arbitrary"),
                                             vmem_limit_bytes=VMEM_LIMIT),
        name="mlstm",
    )(qa, ka, va, og, gcol, grow, nw, c0, n0, m0, tri, trit)


def _suffix_matrix(n):
    u = np.zeros((2 * n, 2 * LANES), np.float32)
    j = np.arange(n)[:, None]
    s = np.arange(n)[None, :]
    u[:n, :n] = (j >= s)
    u[n:, :n] = (j >= s)
    u[:, LANES:] = 1.0
    return jnp.asarray(u, BF16)


def _sb_tiles(jobs, u_ref, n):
    fused = n % LANES == 0
    zs = []
    for q_p, k2, _, _ in jobs:
        if fused:
            z2 = lax.dot_general(q_p, k2, NT_DIMS, preferred_element_type=F32)
            zs.append([z2[:, :n], z2[:, n:]])
        else:
            zs.append([lax.dot_general(q_p, k2[hh * n:(hh + 1) * n], NT_DIMS, preferred_element_type=F32)
                       for hh in range(2)])
    splits = []
    for (_, _, _, mask), zj in zip(jobs, zs):
        sj = []
        for z in zj:
            s = jnp.maximum(z, 0.0) + jnp.log2(1.0 + jnp.exp2(-jnp.abs(z)))
            if mask is not None:
                s = jnp.where(mask, s, 0.0)
            hi = s.astype(BF16)
            sj.append((hi, (s - hi.astype(F32)).astype(BF16)))
        splits.append(sj)
    weights = []
    for (_, _, _, mask), zj, sj in zip(jobs, zs, splits):
        wj = []
        for z, (hi, lo) in zip(zj, sj):
            if fused:
                res = jnp.dot(jnp.concatenate([hi, lo], axis=1), u_ref[...], preferred_element_type=F32)
            else:
                res = (jnp.dot(hi, u_ref[:n, :], preferred_element_type=F32)
                       + jnp.dot(lo, u_ref[n:, :], preferred_element_type=F32))
            a = jnp.exp2(z - res[:, :n])
            if mask is not None:
                a = jnp.where(mask, a, 0.0)
            wj.append((a.astype(BF16), res[:, LANES:]))
        weights.append(wj)
    outs = []
    for (q_p, _, v2, _), wj in zip(jobs, weights):
        if fused:
            o = jnp.dot(jnp.concatenate([wj[0][0], wj[1][0]], axis=1), v2, preferred_element_type=F32)
        else:
            o = (jnp.dot(wj[0][0], v2[:n], preferred_element_type=F32)
                 + jnp.dot(wj[1][0], v2[n:], preferred_element_type=F32))
        lane = lax.broadcasted_iota(jnp.int32, (q_p.shape[0], LANES), 1)
        outs.append((o, jnp.exp2(-jnp.where(lane < SB_DIM, wj[0][1], wj[1][1]))))
    return outs


def _sb_kernel(q_ref, kd_ref, vd_ref, kp_ref, vp_ref, ud_ref, up_ref, o_ref, acc_ref,
               *, tq, tk, nd, kpi, ntrip_static, raw_cache):
    npairs = SB_WIDTH // LANES
    acc_ref[...] = jnp.zeros_like(acc_ref)
    ntrip = pl.program_id(1) if ntrip_static is None else ntrip_static

    pss = [slice(p * LANES, (p + 1) * LANES) for p in range(npairs)]

    def cache_pair(ref, start, p):
        even = ref[0, pl.ds(start, tk), 2 * p, :]
        odd = ref[0, pl.ds(start, tk), 2 * p + 1, :]
        zero = jnp.zeros_like(even)
        return jnp.concatenate([jnp.concatenate([even, zero], axis=1),
                                jnp.concatenate([zero, odd], axis=1)], axis=0).astype(BF16)

    def body(j, carry):
        jobs = []
        for kk in range(kpi):
            if raw_cache:
                start = pl.multiple_of((j * kpi + kk) * tk, tk)
                for p, ps in enumerate(pss):
                    jobs.append((q_ref[0, :, ps], cache_pair(kp_ref, start, p), cache_pair(vp_ref, start, p),
                                 None))
                continue
            start = pl.multiple_of((j * kpi + kk) * 2 * tk, 2 * tk)
            for ps in pss:
                jobs.append((q_ref[0, :, ps], kp_ref[0, pl.ds(start, 2 * tk), ps],
                             vp_ref[0, pl.ds(start, 2 * tk), ps], None))
        for idx, (o, e_r) in enumerate(_sb_tiles(jobs, up_ref, tk)):
            ps = pss[idx % npairs]
            acc_ref[:, ps] = acc_ref[:, ps] * e_r + o
        return carry

    lax.fori_loop(0, ntrip, body, 0)

    r_i = lax.broadcasted_iota(jnp.int32, (tq, nd), 0)
    c_i = lax.broadcasted_iota(jnp.int32, (tq, nd), 1)
    nsub = tq // nd
    jobs = []
    for d in range(nsub):
        for ps in pss:
            jobs.append((q_ref[0, :, ps], kd_ref[0, 2 * d * nd:2 * (d + 1) * nd, ps],
                         vd_ref[0, 2 * d * nd:2 * (d + 1) * nd, ps], c_i + d * nd < r_i))
    for idx, (o, e_r) in enumerate(_sb_tiles(jobs, ud_ref, nd)):
        ps = pss[idx % npairs]
        if idx // npairs == nsub - 1:
            o_ref[0, :, ps] = acc_ref[:, ps] * e_r + o
        else:
            acc_ref[:, ps] = acc_ref[:, ps] * e_r + o


def _sb_call(qb, kd, vd, kp, vp, *, tq, tk, self_causal):
    B, T, W = qb.shape
    nd = min(tq, tk)
    tok = lambda b, i: (b, i, 0)
    if self_causal:
        Tp = kp.shape[1] // 2
        kpi = tq // tk
        past_spec = pl.BlockSpec((1, 2 * Tp, W), lambda b, i: (b, 0, 0))
    else:
        Tp = kp.shape[1]
        kpi = 4
        past_spec = pl.BlockSpec((1,) + kp.shape[1:], lambda b, i: (b, 0, 0, 0))
    in_specs = [pl.BlockSpec((1, tq, W), tok), pl.BlockSpec((1, 2 * tq, W), tok), pl.BlockSpec((1, 2 * tq, W), tok),
                past_spec, past_spec, _vmem_spec(), _vmem_spec()]
    return pl.pallas_call(
        functools.partial(_sb_kernel, tq=tq, tk=tk, nd=nd, kpi=kpi, raw_cache=not self_causal,
                          ntrip_static=None if self_causal else Tp // (tk * kpi)),
        out_shape=jax.ShapeDtypeStruct((B, T, W), F32),
        grid=(B, T // tq), in_specs=in_specs, out_specs=pl.BlockSpec((1, tq, W), tok),
        scratch_shapes=[pltpu.VMEM((tq, W), F32)],
        compiler_params=pltpu.CompilerParams(dimension_semantics=("arbitrary", "arbitrary"),
                                             vmem_limit_bytes=VMEM_LIMIT),
        name="stickbreak",
    )(qb, kd, vd, kp, vp, _suffix_matrix(nd), _suffix_matrix(tk))


def _ffn_kernel(x_ref, ya_ref, ob_ref, mod_ref, cst_ref, onw_ref, n2w_ref, wo_ref, wu_ref, wc_ref,
                bc_ref, wd_ref,
                y_ref, cnew_ref, tail_ref, ubuf_ref, *, bb, tm):
    R = bb * tm

    @pl.when(pl.program_id(1) == 0)
    def _():
        tail_ref[...] = cst_ref[...]

    ycat = []
    for bi in range(bb):
        yb = _rms(ob_ref[bi], onw_ref[...]).astype(BF16)
        ycat.append(jnp.concatenate([ya_ref[bi], yb], axis=1))
    ycat = ycat[0] if bb == 1 else jnp.concatenate(ycat, axis=0)
    mix = jnp.dot(ycat, wo_ref[...], preferred_element_type=F32)

    x1s, h2s = [], []
    for bi in range(bb):
        m = mod_ref[bi]
        x1 = x_ref[bi] + m[2:3, :] * mix[bi * tm:(bi + 1) * tm]
        x1s.append(x1)
        h2s.append((_rms(x1, n2w_ref[...]) * (1.0 + m[4:5, :]) + m[3:4, :]).astype(BF16))
    h2 = h2s[0] if bb == 1 else jnp.concatenate(h2s, axis=0)

    def conv(col0, slot):
        cols = slice(col0, col0 + FF_CHUNK)
        u = jnp.dot(h2, wu_ref[:, cols], preferred_element_type=F32)
        w0, w1, w2 = wc_ref[0:1, cols], wc_ref[1:2, cols], wc_ref[2:3, cols]
        outs = []
        for bi in range(bb):
            ub = u[bi * tm:(bi + 1) * tm]
            base = bi * (tm + 8)
            ubuf_ref[slot, base + 6:base + 8, :] = tail_ref[bi, :, cols]
            ubuf_ref[slot, base + 8:base + 8 + tm, :] = ub
            u1 = ubuf_ref[slot, base + 7:base + 7 + tm, :]
            u2 = ubuf_ref[slot, base + 6:base + 6 + tm, :]
            outs.append(u2 * w0 + u1 * w1 + ub * w2 + bc_ref[:, cols])
            tail_ref[bi, :, cols] = ubuf_ref[slot, base + 6 + tm:base + 8 + tm, :]
        return outs[0] if bb == 1 else jnp.concatenate(outs, axis=0)

    def gated(c):
        ua = conv(c * FF_CHUNK, 2 * (c % 2))
        ug = conv(D_FF + c * FF_CHUNK, 2 * (c % 2) + 1)
        return (ua * jax.nn.sigmoid(ua) * ug).astype(BF16)

    acc = jnp.zeros((R, D_MODEL), F32)
    act = gated(0)
    for c in range(N_FF_CHUNKS):
        nxt = gated(c + 1) if c + 1 < N_FF_CHUNKS else None
        acc = acc + jnp.dot(act, wd_ref[c * FF_CHUNK:(c + 1) * FF_CHUNK, :], preferred_element_type=F32)
        act = nxt

    for bi in range(bb):
        y_ref[bi] = x1s[bi] + mod_ref[bi][5:6, :] * acc[bi * tm:(bi + 1) * tm]

    @pl.when(pl.program_id(1) == pl.num_programs(1) - 1)
    def _():
        cnew_ref[...] = tail_ref[...]


def _ffn_call(x, ya, ob, mod3, cstate, wts, *, bb, tm):
    B, T, D = x.shape
    tok = lambda b, t: (b, t, 0)
    bat = lambda b, t: (b, 0, 0)
    in_specs = [pl.BlockSpec((bb, tm, D), tok), pl.BlockSpec((bb, tm, ML_WIDTH), tok),
                pl.BlockSpec((bb, tm, SB_WIDTH), tok), pl.BlockSpec((bb, 6, D), bat),
                pl.BlockSpec((bb, CONV_W - 1, 2 * D_FF), bat)] + [_vmem_spec()] * 7
    out_shape = [jax.ShapeDtypeStruct((B, T, D), F32), jax.ShapeDtypeStruct((B, CONV_W - 1, 2 * D_FF), F32)]
    out_specs = [pl.BlockSpec((bb, tm, D), tok), pl.BlockSpec((bb, CONV_W - 1, 2 * D_FF), bat)]
    return pl.pallas_call(
        functools.partial(_ffn_kernel, bb=bb, tm=tm),
        out_shape=out_shape, grid=(B // bb, T // tm), in_specs=in_specs, out_specs=out_specs,
        scratch_shapes=[pltpu.VMEM((bb, CONV_W - 1, 2 * D_FF), F32),
                        pltpu.VMEM((4, bb * (tm + 8), FF_CHUNK), F32)],
        compiler_params=pltpu.CompilerParams(dimension_semantics=("arbitrary", "arbitrary"),
                                             vmem_limit_bytes=VMEM_LIMIT),
        name="outproj_convffn",
    )(x, ya, ob, mod3, cstate, wts["onw"], wts["n2w"], wts["wo"], wts["wu"], wts["wc"], wts["bc"], wts["wd"])


def _prep_weights(norm1_w, w_in, b_ig, b_fg, ml_norm_w, sb_q_norm_w, sb_k_norm_w, sb_out_norm_w, w_out,
                  norm2_w, w_up, w_conv, b_conv, w_down):
    nm = 4 * ML_WIDTH
    ng = 2 * ML_HEADS
    w_g = w_in[:, nm:nm + ng]
    b_g = jnp.concatenate([b_ig, b_fg])
    return {
        "n1w": norm1_w.reshape(1, D_MODEL),
        "wm": w_in[:, :nm].astype(BF16),
        "wg": jnp.pad(w_g, ((0, 0), (0, LANES - ng))).astype(BF16),
        "wgt": jnp.pad(w_g.T, ((0, 16 - ng), (0, 0))).astype(BF16),
        "ws": w_in[:, nm + ng:].astype(BF16),
        "bgc": jnp.pad(b_g, (0, LANES - ng)).reshape(1, LANES),
        "bgr": jnp.pad(b_g, (0, 16 - ng)).reshape(16, 1),
        "qnw": jnp.tile(sb_q_norm_w, 2).reshape(1, LANES),
        "knw": jnp.tile(sb_k_norm_w, 2).reshape(1, LANES),
        "mnw": ml_norm_w.reshape(1, ML_WIDTH),
        "onw": sb_out_norm_w.reshape(1, SB_WIDTH),
        "n2w": norm2_w.reshape(1, D_MODEL),
        "wo": w_out.astype(BF16),
        "wu": w_up.astype(BF16),
        "wc": jnp.pad(w_conv, ((0, 8 - CONV_W), (0, 0))),
        "bc": b_conv.reshape(1, 2 * D_FF),
        "wd": w_down.astype(BF16),
    }


def _layer(x, mod3, wts, ml_state, k_past, v_past, conv_state, *, bb, tm, chunk, tq, tk):
    B, T, _ = x.shape
    nch = 2 * N_FF_CHUNKS
    qa, ka, va, og, gcol, grow, qb, kr, vr, k2, v2 = _inproj_call(x, mod3, wts, bb=bb, tm=tm,
                                                                  kblk=min(tq, tk))

    if ml_state is None:
        c0 = jnp.zeros((B, ML_HEADS, ML_DIM, 2 * ML_DIM), F32)
        n0 = jnp.zeros((B, 8, ML_WIDTH), F32)
        m0 = jnp.zeros((B, 1, LANES), F32)
    else:
        c_in, n_in, m_in = ml_state
        c0 = jnp.concatenate([c_in, jnp.broadcast_to(n_in[..., None], c_in.shape)], axis=-1)
        n0 = jnp.pad(jnp.broadcast_to(n_in.reshape(B, 1, ML_WIDTH), (B, ML_HEADS, ML_WIDTH)),
                     ((0, 0), (0, 8 - ML_HEADS), (0, 0)))
        m0 = jnp.pad(m_in, ((0, 0), (0, LANES - ML_HEADS))).reshape(B, 1, LANES)
    ya, c_new, n_new, m_new = _mlstm_call(qa, ka, va, og, gcol, grow, wts["mnw"], c0, n0, m0, L=chunk,
                                           cps=min(4, T // chunk))

    if k_past is None:
        ob = _sb_call(qb, k2, v2, k2, v2, tq=tq, tk=tk, self_causal=True)
    else:
        ob = _sb_call(qb, k2, v2, k_past, v_past, tq=tq, tk=tk, self_causal=False)

    if conv_state is None:
        conv_state = jnp.zeros((B, CONV_W - 1, 2 * D_FF), F32)
    y, conv_new = _ffn_call(x, ya, ob, mod3, conv_state, wts, bb=bb, tm=tm)
    return (y, c_new, n_new, m_new[:, 0, :ML_HEADS],
            kr.reshape(B, T, SB_HEADS, SB_DIM), vr.reshape(B, T, SB_HEADS, SB_DIM), conv_new)


def kernel(x_prompt, x_sample, c_prompt, c_sample, state_mlstm_c, state_mlstm_n, state_mlstm_m, cache_sb_k, cache_sb_v, state_ffn_conv, w_ada, b_ada, norm1_w, w_in, b_ig, b_fg, ml_norm_w, sb_q_norm_w, sb_k_norm_w, sb_out_norm_w, w_out, norm2_w, w_up, w_conv, b_conv, w_down):
    depth = w_ada.shape[0]
    B = x_prompt.shape[0]
    y_p, y_s = x_prompt, x_sample
    outs_p = [[] for _ in range(6)]
    outs_s = [[] for _ in range(6)]
    for l in range(depth):
        mod = _ada_call(jnp.concatenate([c_prompt, c_sample], axis=0), w_ada[l], b_ada[l])
        mod3 = mod.reshape(mod.shape[0], 6, D_MODEL)
        wts = _prep_weights(norm1_w[l], w_in[l], b_ig[l], b_fg[l], ml_norm_w[l], sb_q_norm_w[l],
                            sb_k_norm_w[l], sb_out_norm_w[l], w_out[l], norm2_w[l], w_up[l], w_conv[l],
                            b_conv[l], w_down[l])
        res_p = _layer(y_p, mod3[:B], wts, None, None, None, None,
                       bb=1, tm=512, chunk=128, tq=256, tk=128)
        res_s = _layer(y_s, mod3[B:], wts, (state_mlstm_c[l], state_mlstm_n[l], state_mlstm_m[l]),
                       cache_sb_k[l], cache_sb_v[l], state_ffn_conv[l],
                       bb=x_sample.shape[0], tm=x_sample.shape[1], chunk=x_sample.shape[1],
                       tq=x_sample.shape[1], tk=128)
        y_p, y_s = res_p[0], res_s[0]
        for i in range(6):
            outs_p[i].append(res_p[i + 1])
            outs_s[i].append(res_s[i + 1])
    return (y_p, y_s, *[jnp.stack(o) for o in outs_p], *[jnp.stack(o) for o in outs_s])
```

```python
import functools

import numpy as np
import jax
import jax.numpy as jnp
from jax import lax
from jax.experimental import pallas as pl
from jax.experimental.pallas import tpu as pltpu

F32 = jnp.float32
BF16 = jnp.bfloat16

D_MODEL = 1024
ML_HEADS = 4
ML_DIM = 128
ML_WIDTH = ML_HEADS * ML_DIM
SB_HEADS = 8
SB_DIM = 64
SB_WIDTH = SB_HEADS * SB_DIM
D_FF = 2816
CONV_W = 3
EPS = 1e-6
LOG2E = 1.4426950408889634
SKIP_BITS = 160.0

LANES = 128
FF_CHUNK = 256
N_FF_CHUNKS = D_FF // FF_CHUNK
VMEM_LIMIT = 52 * 1024 * 1024

NT_DIMS = (((1,), (1,)), ((), ()))
TN_DIMS = (((0,), (0,)), ((), ()))


def _vmem_spec():
    return pl.BlockSpec(memory_space=pltpu.VMEM)


def _rms(x, w):
    return x * lax.rsqrt(jnp.mean(x * x, axis=-1, keepdims=True) + EPS) * w


def _softplus(x):
    return jnp.maximum(x, 0.0) + jnp.log(1.0 + jnp.exp(-jnp.abs(x)))


def _split3(x):
    h1 = x.astype(BF16)
    r1 = x - h1.astype(F32)
    h2 = r1.astype(BF16)
    h3 = (r1 - h2.astype(F32)).astype(BF16)
    return h1, h2, h3


def _ada_kernel(c_ref, w_ref, b_ref, o_ref):
    c = c_ref[...]
    s = (c * jax.nn.sigmoid(c)).astype(BF16)
    o_ref[...] = jnp.dot(s, w_ref[...].astype(BF16), preferred_element_type=F32) + b_ref[...]


def _ada_call(c, w_ada, b_ada):
    n, d = c.shape
    nout = w_ada.shape[1]
    return pl.pallas_call(
        _ada_kernel,
        out_shape=jax.ShapeDtypeStruct((n, nout), F32),
        grid=(nout // d,),
        in_specs=[pl.BlockSpec((n, d), lambda j: (0, 0)),
                  pl.BlockSpec((d, d), lambda j: (0, j)),
                  pl.BlockSpec((1, d), lambda j: (0, j))],
        out_specs=pl.BlockSpec((n, d), lambda j: (0, j)),
        compiler_params=pltpu.CompilerParams(dimension_semantics=("arbitrary",)),
        name="adaln",
    )(c, w_ada, b_ada.reshape(1, nout))


def _headnorm64(x, w128):
    lane = lax.broadcasted_iota(jnp.int32, x.shape, 1)
    lo = lane < SB_DIM
    x2 = x * x
    s0 = jnp.sum(jnp.where(lo, x2, 0.0), axis=-1, keepdims=True)
    s1 = jnp.sum(jnp.where(lo, 0.0, x2), axis=-1, keepdims=True)
    r0 = lax.rsqrt(s0 * (1.0 / SB_DIM) + EPS)
    r1 = lax.rsqrt(s1 * (1.0 / SB_DIM) + EPS)
    return x * jnp.where(lo, r0, r1) * w128


def _inproj_kernel(x_ref, mod_ref, n1w_ref, wm_ref, wg_ref, wgt_ref, ws_ref, bgc_ref, bgr_ref,
                   qnw_ref, knw_ref,
                   qa_ref, ka_ref, va_ref, og_ref, gcol_ref, grow_ref, qb_ref, kr_ref, vr_ref,
                   k2_ref, v2_ref, *, bb, tm, kblk):
    hs = []
    for bi in range(bb):
        m = mod_ref[bi]
        y = _rms(x_ref[bi], n1w_ref[...])
        hs.append((y * (1.0 + m[1:2, :]) + m[0:1, :]).astype(BF16))
    h = hs[0] if bb == 1 else jnp.concatenate(hs, axis=0)

    def put(ref, val):
        for bi in range(bb):
            ref[bi] = val[bi * tm:(bi + 1) * tm].astype(ref.dtype)

    def proj(w_ref, j):
        return jnp.dot(h, w_ref[:, j * ML_WIDTH:(j + 1) * ML_WIDTH], preferred_element_type=F32)

    put(qa_ref, proj(wm_ref, 0))
    put(ka_ref, proj(wm_ref, 1) * (ML_DIM ** -0.5))
    put(va_ref, proj(wm_ref, 2))
    put(og_ref, jax.nn.sigmoid(proj(wm_ref, 3)))

    gc = jnp.dot(h, wg_ref[...], preferred_element_type=F32) + bgc_ref[...]
    lane = lax.broadcasted_iota(jnp.int32, gc.shape, 1)
    put(gcol_ref, jnp.where(lane >= ML_HEADS, -_softplus(-gc), gc))
    gr = lax.dot_general(wgt_ref[...], h, NT_DIMS, preferred_element_type=F32) + bgr_ref[...]
    row = lax.broadcasted_iota(jnp.int32, gr.shape, 0)
    gr = jnp.where(row >= ML_HEADS, -_softplus(-gr), gr)
    for bi in range(bb):
        grow_ref[bi] = gr[:2 * ML_HEADS, bi * tm:(bi + 1) * tm]

    sq = proj(ws_ref, 0)
    sk = proj(ws_ref, 1)
    sv = proj(ws_ref, 2)
    qn, kn = [], []
    for j in range(SB_WIDTH // LANES):
        sl = slice(j * LANES, (j + 1) * LANES)
        qn.append(_headnorm64(sq[:, sl], qnw_ref[...]) * (SB_DIM ** -0.5 * LOG2E))
        kn.append(_headnorm64(sk[:, sl], knw_ref[...]))
    qn = jnp.concatenate(qn, axis=1)
    kn = jnp.concatenate(kn, axis=1)
    put(qb_ref, qn)
    put(kr_ref, kn)
    put(vr_ref, sv)

    even = (lax.broadcasted_iota(jnp.int32, kn.shape, 1) & SB_DIM) == 0

    def put_pairs(ref, val):
        ve = jnp.where(even, val, 0.0).astype(BF16)
        vo = jnp.where(even, 0.0, val).astype(BF16)
        for bi in range(bb):
            parts = []
            for r0 in range(bi * tm, (bi + 1) * tm, kblk):
                parts += [ve[r0:r0 + kblk], vo[r0:r0 + kblk]]
            ref[bi] = jnp.concatenate(parts, axis=0)

    put_pairs(k2_ref, kn)
    put_pairs(v2_ref, sv)


def _inproj_call(x, mod3, wts, *, bb, tm, kblk):
    B, T, D = x.shape
    grid = (B // bb, T // tm)
    tok = lambda b, t: (b, t, 0)

    def tok_spec(w):
        return pl.BlockSpec((bb, tm, w), tok)

    def tok_shape(w, dt):
        return jax.ShapeDtypeStruct((B, T, w), dt)

    out_shape = [tok_shape(ML_WIDTH, BF16), tok_shape(ML_WIDTH, BF16), tok_shape(ML_WIDTH, BF16),
                 tok_shape(ML_WIDTH, F32), tok_shape(LANES, F32),
                 jax.ShapeDtypeStruct((B, 2 * ML_HEADS, T), F32),
                 tok_shape(SB_WIDTH, BF16), tok_shape(SB_WIDTH, F32), tok_shape(SB_WIDTH, F32),
                 jax.ShapeDtypeStruct((B, 2 * T, SB_WIDTH), BF16),
                 jax.ShapeDtypeStruct((B, 2 * T, SB_WIDTH), BF16)]
    pair_spec = pl.BlockSpec((bb, 2 * tm, SB_WIDTH), tok)
    out_specs = [tok_spec(ML_WIDTH), tok_spec(ML_WIDTH), tok_spec(ML_WIDTH), tok_spec(ML_WIDTH),
                 tok_spec(LANES), pl.BlockSpec((bb, 2 * ML_HEADS, tm), lambda b, t: (b, 0, t)),
                 tok_spec(SB_WIDTH), tok_spec(SB_WIDTH), tok_spec(SB_WIDTH), pair_spec, pair_spec]
    in_specs = [tok_spec(D), pl.BlockSpec((bb, 6, D), lambda b, t: (b, 0, 0))] + [_vmem_spec()] * 9
    return pl.pallas_call(
        functools.partial(_inproj_kernel, bb=bb, tm=tm, kblk=kblk),
        out_shape=out_shape, grid=grid, in_specs=in_specs, out_specs=out_specs,
        compiler_params=pltpu.CompilerParams(dimension_semantics=("arbitrary", "arbitrary"),
                                             vmem_limit_bytes=VMEM_LIMIT),
        name="inproj",
    )(x, mod3, wts["n1w"], wts["wm"], wts["wg"], wts["wgt"], wts["ws"], wts["bgc"], wts["bgr"],
      wts["qnw"], wts["knw"])


def _mlstm_kernel(q_ref, k_ref, v_ref, og_ref, gcol_ref, grow_ref, nw_ref, c0_ref, n0_ref, m0_ref,
                  tri_ref, trit_ref,
                  ya_ref, cout_ref, nout_ref, mout_ref,
                  c_scr, n_scr, m_scr, *, L, cps):
    @pl.when(pl.program_id(1) == 0)
    def _():
        c_scr[...] = c0_ref[0]
        n_scr[...] = n0_ref[0]
        m_scr[...] = m0_ref[0]

    heads = range(ML_HEADS)
    hss = [slice(h * ML_DIM, (h + 1) * ML_DIM) for h in heads]
    rowss = [slice(cc * L, (cc + 1) * L) for cc in range(cps)]
    r_i = lax.broadcasted_iota(jnp.int32, (L, L), 0)
    c_i = lax.broadcasted_iota(jnp.int32, (L, L), 1)
    causal = c_i <= r_i
    lane1 = lax.broadcasted_iota(jnp.int32, (1, LANES), 1)
    ones = jnp.ones((L, LANES), BF16)

    gates, qks = [], []
    for rows in rowss:
        gcol = gcol_ref[0, rows, :]
        grow = grow_ref[0, :, rows]
        grow16 = jnp.concatenate([grow, jnp.zeros_like(grow)], axis=0)
        bcol = sum(jnp.dot(tri_ref[...], p, preferred_element_type=F32) for p in _split3(gcol))
        brow = sum(jnp.dot(p, trit_ref[...], preferred_element_type=F32) for p in _split3(grow16))
        gates.append((gcol, grow, bcol, brow))
    for rows in rowss:
        qks.append([lax.dot_general(q_ref[0, rows, hs], k_ref[0, rows, hs], NT_DIMS,
                                    preferred_element_type=F32) for hs in hss])

    m_vec = m_scr[...]
    m_run = [m_vec[:, h:h + 1] for h in heads]
    plan = []
    for cc, rows in enumerate(rowss):
        gcol, grow, bcol, brow = gates[cc]
        per_head = []
        for h in heads:
            b_c = jnp.broadcast_to(bcol[:, ML_HEADS + h:ML_HEADS + h + 1], (L, LANES))
            ig_c = jnp.broadcast_to(gcol[:, h:h + 1], (L, LANES))
            b_r = brow[ML_HEADS + h:ML_HEADS + h + 1, :]
            ig_r = grow[h:h + 1, :]
            b_last = b_r[:, L - 1:L]
            g_r = ig_r - b_r
            m_h = m_run[h]
            dlog = jnp.where(causal, b_c[:, :L] + g_r, -jnp.inf)
            inter = b_c + m_h
            m_t = jnp.maximum(inter, jnp.max(dlog, axis=-1, keepdims=True))
            w_intra = jnp.exp(dlog - m_t[:, :L])
            wk_r = b_last + g_r
            m_new = jnp.maximum(b_last + m_h, jnp.max(wk_r, axis=-1, keepdims=True))
            wk_c = jnp.exp(b_last - b_c + ig_c - m_new)
            v = v_ref[0, rows, hss[h]]
            per_head.append({
                "s": (qks[cc][h] * w_intra).astype(BF16),
                "w_inter": jnp.exp(inter - m_t),
                "floor": jnp.exp(-m_t),
                "decay": jnp.exp(b_last + m_h - m_new),
                "vaug": jnp.concatenate([v, ones], axis=1),
                "wv": jnp.concatenate([(wk_c * v.astype(F32)).astype(BF16), wk_c.astype(BF16)], axis=1),
                "wk_row": jnp.exp(wk_r - m_new),
            })
            m_run[h] = m_new
        plan.append(per_head)

    for cc, rows in enumerate(rowss):
        for h in heads:
            ph = plan[cc][h]
            ph["intra"] = jnp.dot(ph["s"], ph["vaug"], preferred_element_type=F32)
            ph["upd"] = lax.dot_general(k_ref[0, rows, hss[h]], ph["wv"], TN_DIMS, preferred_element_type=F32)
    nupds = []
    for cc, rows in enumerate(rowss):
        wk16 = jnp.concatenate([plan[cc][h]["wk_row"] for h in heads]
                               + [jnp.zeros((16 - ML_HEADS, L), F32)], axis=0).astype(BF16)
        nupds.append(jnp.dot(wk16, k_ref[0, rows, :], preferred_element_type=F32))

    for cc, rows in enumerate(rowss):
        for h in heads:
            ph = plan[cc][h]
            hs = hss[h]
            cst = c_scr[h]
            qc = jnp.dot(q_ref[0, rows, hs], cst.astype(BF16), preferred_element_type=F32)
            num = ph["w_inter"] * qc[:, :ML_DIM] + ph["intra"][:, :ML_DIM]
            den = ph["w_inter"] * qc[:, ML_DIM:] + ph["intra"][:, ML_DIM:]
            hcell = num / jnp.maximum(jnp.abs(den), ph["floor"])
            ya_ref[0, rows, hs] = (_rms(hcell, nw_ref[:, hs]) * og_ref[0, rows, hs]).astype(BF16)
            c_scr[h] = ph["decay"] * cst + ph["upd"]
            n_scr[h:h + 1, hs] = ph["decay"] * n_scr[h:h + 1, hs] + nupds[cc][h:h + 1, hs]

    m_out = m_vec
    for h in heads:
        cout_ref[0, h] = c_scr[h][:, :ML_DIM]
        nout_ref[0, h:h + 1, :] = n_scr[h:h + 1, hss[h]]
        m_out = jnp.where(lane1 == h, m_run[h], m_out)
    m_scr[...] = m_out
    mout_ref[0] = m_out


def _mlstm_call(qa, ka, va, og, gcol, grow, nw, c0, n0, m0, *, L, cps):
    B, T, _ = qa.shape
    tri_np = np.tril(np.ones((L, L), np.float32))
    tri = jnp.asarray(tri_np, BF16)
    trit = jnp.asarray(tri_np.T, BF16)
    tok = lambda b, c: (b, c, 0)
    bat3 = lambda b, c: (b, 0, 0)
    bat4 = lambda b, c: (b, 0, 0, 0)
    tl = L * cps
    in_specs = [pl.BlockSpec((1, tl, ML_WIDTH), tok)] * 4 + [
        pl.BlockSpec((1, tl, LANES), tok),
        pl.BlockSpec((1, 2 * ML_HEADS, tl), lambda b, c: (b, 0, c)),
        _vmem_spec(),
        pl.BlockSpec((1, ML_HEADS, ML_DIM, 2 * ML_DIM), bat4),
        pl.BlockSpec((1, 8, ML_WIDTH), bat3),
        pl.BlockSpec((1, 1, LANES), bat3),
        _vmem_spec(), _vmem_spec()]
    out_shape = [jax.ShapeDtypeStruct((B, T, ML_WIDTH), BF16),
                 jax.ShapeDtypeStruct((B, ML_HEADS, ML_DIM, ML_DIM), F32),
                 jax.ShapeDtypeStruct((B, ML_HEADS, ML_DIM), F32),
                 jax.ShapeDtypeStruct((B, 1, LANES), F32)]
    out_specs = [pl.BlockSpec((1, tl, ML_WIDTH), tok),
                 pl.BlockSpec((1, ML_HEADS, ML_DIM, ML_DIM), bat4),
                 pl.BlockSpec((1, ML_HEADS, ML_DIM), bat3),
                 pl.BlockSpec((1, 1, LANES), bat3)]
    return pl.pallas_call(
        functools.partial(_mlstm_kernel, L=L, cps=cps),
        out_shape=out_shape, grid=(B, T // tl), in_specs=in_specs, out_specs=out_specs,
        scratch_shapes=[pltpu.VMEM((ML_HEADS, ML_DIM, 2 * ML_DIM), F32),
                        pltpu.VMEM((8, ML_WIDTH), F32),
                        pltpu.VMEM((1, LANES), F32)],
        compiler_params=pltpu.CompilerParams(dimension_semantics=("arbitrary", "arbitrary"),
                                             vmem_limit_bytes=VMEM_LIMIT),
        name="mlstm",
    )(qa, ka, va, og, gcol, grow, nw, c0, n0, m0, tri, trit)


def _suffix_matrix(n):
    u = np.zeros((2 * n, 2 * LANES), np.float32)
    j = np.arange(n)[:, None]
    s = np.arange(n)[None, :]
    u[:n, :n] = (j >= s)
    u[n:, :n] = (j >= s)
    u[:, LANES:] = 1.0
    return jnp.asarray(u, BF16)


def _sb_tiles(jobs, u_ref, n, transposed=False):
    fused = n % LANES == 0
    zs = []
    for q_p, k2, _, _ in jobs:
        if transposed:
            z2 = jnp.dot(q_p, k2, preferred_element_type=F32)
            zs.append([z2[:, :n], z2[:, n:]])
        elif fused:
            z2 = lax.dot_general(q_p, k2, NT_DIMS, preferred_element_type=F32)
            zs.append([z2[:, :n], z2[:, n:]])
        else:
            zs.append([lax.dot_general(q_p, k2[hh * n:(hh + 1) * n], NT_DIMS, preferred_element_type=F32)
                       for hh in range(2)])
    splits = []
    for (_, _, _, mask), zj in zip(jobs, zs):
        sj = []
        for z in zj:
            s = jnp.maximum(z, 0.0) + jnp.log2(1.0 + jnp.exp2(-jnp.abs(z)))
            if mask is not None:
                s = jnp.where(mask, s, 0.0)
            hi = s.astype(BF16)
            sj.append((hi, (s - hi.astype(F32)).astype(BF16)))
        splits.append(sj)
    weights = []
    for (_, _, _, mask), zj, sj in zip(jobs, zs, splits):
        wj = []
        for z, (hi, lo) in zip(zj, sj):
            if fused:
                res = jnp.dot(jnp.concatenate([hi, lo], axis=1), u_ref[...], preferred_element_type=F32)
            else:
                res = (jnp.dot(hi, u_ref[:n, :], preferred_element_type=F32)
                       + jnp.dot(lo, u_ref[n:, :], preferred_element_type=F32))
            a = jnp.exp2(z - res[:, :n])
            if mask is not None:
                a = jnp.where(mask, a, 0.0)
            wj.append((a.astype(BF16), res[:, LANES:]))
        weights.append(wj)
    outs = []
    for (q_p, _, v2, _), wj in zip(jobs, weights):
        if transposed:
            o = lax.dot_general(jnp.concatenate([wj[0][0], wj[1][0]], axis=1), v2, NT_DIMS,
                                preferred_element_type=F32)
        elif fused:
            o = jnp.dot(jnp.concatenate([wj[0][0], wj[1][0]], axis=1), v2, preferred_element_type=F32)
        else:
            o = (jnp.dot(wj[0][0], v2[:n], preferred_element_type=F32)
                 + jnp.dot(wj[1][0], v2[n:], preferred_element_type=F32))
        lane = lax.broadcasted_iota(jnp.int32, (q_p.shape[0], LANES), 1)
        outs.append((o, jnp.where(lane < SB_DIM, wj[0][1], wj[1][1])))
    return outs


def _sb_kernel(q_ref, kd_ref, vd_ref, kp_ref, vp_ref, ud_ref, up_ref, o_ref, acc_ref, used_ref,
               *, tq, tk, nd, kpi, ntrip_static, transposed_cache):
    npairs = SB_WIDTH // LANES
    pss = [slice(p * LANES, (p + 1) * LANES) for p in range(npairs)]
    acc_ref[...] = jnp.zeros_like(acc_ref)
    used_ref[...] = jnp.zeros_like(used_ref)

    def absorb(results):
        for idx, (o, r) in enumerate(results):
            ps = pss[idx % npairs]
            used = used_ref[:, ps]
            acc_ref[:, ps] = acc_ref[:, ps] + jnp.exp2(-used) * o
            used_ref[:, ps] = used + r

    r_i = lax.broadcasted_iota(jnp.int32, (tq, nd), 0)
    c_i = lax.broadcasted_iota(jnp.int32, (tq, nd), 1)
    jobs = []
    for d in reversed(range(tq // nd)):
        for ps in pss:
            jobs.append((q_ref[0, :, ps], kd_ref[0, 2 * d * nd:2 * (d + 1) * nd, ps],
                         vd_ref[0, 2 * d * nd:2 * (d + 1) * nd, ps], c_i + d * nd < r_i))
    absorb(_sb_tiles(jobs, ud_ref, nd))

    def cache_pair(ref, cols, p):
        even = ref[0, 2 * p, :, cols]
        odd = ref[0, 2 * p + 1, :, cols]
        zero = jnp.zeros_like(even)
        return jnp.concatenate([jnp.concatenate([even, zero], axis=1),
                                jnp.concatenate([zero, odd], axis=1)], axis=0).astype(BF16)

    def past_jobs(j):
        jobs = []
        for kk in reversed(range(kpi)):
            blk = j * kpi + kk
            if transposed_cache:
                cols = pl.ds(pl.multiple_of(blk * tk, tk), tk)
                for p, ps in enumerate(pss):
                    jobs.append((q_ref[0, :, ps], cache_pair(kp_ref, cols, p), cache_pair(vp_ref, cols, p), None))
            else:
                rows = pl.ds(pl.multiple_of(blk * 2 * tk, 2 * tk), 2 * tk)
                for ps in pss:
                    jobs.append((q_ref[0, :, ps], kp_ref[0, rows, ps], vp_ref[0, rows, ps], None))
        return jobs

    def cond(carry):
        j, least_used = carry
        return jnp.logical_and(j >= 0, least_used < SKIP_BITS)

    def body(carry):
        j, _ = carry
        absorb(_sb_tiles(past_jobs(j), up_ref, tk, transposed=transposed_cache))
        return j - 1, jnp.min(used_ref[...])

    ntrip = pl.program_id(1) if ntrip_static is None else ntrip_static
    lax.while_loop(cond, body, (ntrip - 1, jnp.min(used_ref[...])))
    o_ref[0] = acc_ref[...]


def _sb_call(qb, kd, vd, kp, vp, *, tq, tk, self_causal):
    B, T, W = qb.shape
    nd = min(tq, tk)
    tok = lambda b, i: (b, i, 0)
    if self_causal:
        Tp = kp.shape[1] // 2
        kpi = tq // tk
        past_spec = pl.BlockSpec((1, 2 * Tp, W), lambda b, i: (b, 0, 0))
    else:
        Tp = kp.shape[3]
        kpi = 2
        past_spec = pl.BlockSpec((1,) + kp.shape[1:], lambda b, i: (b, 0, 0, 0))
    in_specs = [pl.BlockSpec((1, tq, W), tok), pl.BlockSpec((1, 2 * tq, W), tok), pl.BlockSpec((1, 2 * tq, W), tok),
                past_spec, past_spec, _vmem_spec(), _vmem_spec()]
    return pl.pallas_call(
        functools.partial(_sb_kernel, tq=tq, tk=tk, nd=nd, kpi=kpi, transposed_cache=not self_causal,
                          ntrip_static=None if self_causal else Tp // (tk * kpi)),
        out_shape=jax.ShapeDtypeStruct((B, T, W), F32),
        grid=(B, T // tq), in_specs=in_specs, out_specs=pl.BlockSpec((1, tq, W), tok),
        scratch_shapes=[pltpu.VMEM((tq, W), F32), pltpu.VMEM((tq, W), F32)],
        compiler_params=pltpu.CompilerParams(dimension_semantics=("arbitrary", "arbitrary"),
                                             vmem_limit_bytes=VMEM_LIMIT),
        name="stickbreak",
    )(qb, kd, vd, kp, vp, _suffix_matrix(nd), _suffix_matrix(tk))


def _ffn_kernel(x_ref, ya_ref, ob_ref, mod_ref, cst_ref, onw_ref, n2w_ref, wo_ref, wu_ref, wc_ref,
                bc_ref, wd_ref,
                y_ref, cnew_ref, tail_ref, ubuf_ref, *, bb, tm):
    R = bb * tm

    @pl.when(pl.program_id(1) == 0)
    def _():
        tail_ref[...] = cst_ref[...]

    ycat = []
    for bi in range(bb):
        yb = _rms(ob_ref[bi], onw_ref[...]).astype(BF16)
        ycat.append(jnp.concatenate([ya_ref[bi], yb], axis=1))
    ycat = ycat[0] if bb == 1 else jnp.concatenate(ycat, axis=0)
    mix = jnp.dot(ycat, wo_ref[...], preferred_element_type=F32)

    x1s, h2s = [], []
    for bi in range(bb):
        m = mod_ref[bi]
        x1 = x_ref[bi] + m[2:3, :] * mix[bi * tm:(bi + 1) * tm]
        x1s.append(x1)
        h2s.append((_rms(x1, n2w_ref[...]) * (1.0 + m[4:5, :]) + m[3:4, :]).astype(BF16))
    h2 = h2s[0] if bb == 1 else jnp.concatenate(h2s, axis=0)

    def conv(col0, slot):
        cols = slice(col0, col0 + FF_CHUNK)
        u = jnp.dot(h2, wu_ref[:, cols], preferred_element_type=F32)
        w0, w1, w2 = wc_ref[0:1, cols], wc_ref[1:2, cols], wc_ref[2:3, cols]
        outs = []
        for bi in range(bb):
            ub = u[bi * tm:(bi + 1) * tm]
            base = bi * (tm + 8)
            ubuf_ref[slot, base + 6:base + 8, :] = tail_ref[bi, :, cols]
            ubuf_ref[slot, base + 8:base + 8 + tm, :] = ub
            u1 = ubuf_ref[slot, base + 7:base + 7 + tm, :]
            u2 = ubuf_ref[slot, base + 6:base + 6 + tm, :]
            outs.append(u2 * w0 + u1 * w1 + ub * w2 + bc_ref[:, cols])
            tail_ref[bi, :, cols] = ubuf_ref[slot, base + 6 + tm:base + 8 + tm, :]
        return outs[0] if bb == 1 else jnp.concatenate(outs, axis=0)

    def gated(c):
        ua = conv(c * FF_CHUNK, 2 * (c % 2))
        ug = conv(D_FF + c * FF_CHUNK, 2 * (c % 2) + 1)
        return (ua * jax.nn.sigmoid(ua) * ug).astype(BF16)

    acc = jnp.zeros((R, D_MODEL), F32)
    act = gated(0)
    for c in range(N_FF_CHUNKS):
        nxt = gated(c + 1) if c + 1 < N_FF_CHUNKS else None
        acc = acc + jnp.dot(act, wd_ref[c * FF_CHUNK:(c + 1) * FF_CHUNK, :], preferred_element_type=F32)
        act = nxt

    for bi in range(bb):
        y_ref[bi] = x1s[bi] + mod_ref[bi][5:6, :] * acc[bi * tm:(bi + 1) * tm]

    @pl.when(pl.program_id(1) == pl.num_programs(1) - 1)
    def _():
        cnew_ref[...] = tail_ref[...]


def _ffn_call(x, ya, ob, mod3, cstate, wts, *, bb, tm):
    B, T, D = x.shape
    tok = lambda b, t: (b, t, 0)
    bat = lambda b, t: (b, 0, 0)
    in_specs = [pl.BlockSpec((bb, tm, D), tok), pl.BlockSpec((bb, tm, ML_WIDTH), tok),
                pl.BlockSpec((bb, tm, SB_WIDTH), tok), pl.BlockSpec((bb, 6, D), bat),
                pl.BlockSpec((bb, CONV_W - 1, 2 * D_FF), bat)] + [_vmem_spec()] * 7
    out_shape = [jax.ShapeDtypeStruct((B, T, D), F32), jax.ShapeDtypeStruct((B, CONV_W - 1, 2 * D_FF), F32)]
    out_specs = [pl.BlockSpec((bb, tm, D), tok), pl.BlockSpec((bb, CONV_W - 1, 2 * D_FF), bat)]
    return pl.pallas_call(
        functools.partial(_ffn_kernel, bb=bb, tm=tm),
        out_shape=out_shape, grid=(B // bb, T // tm), in_specs=in_specs, out_specs=out_specs,
        scratch_shapes=[pltpu.VMEM((bb, CONV_W - 1, 2 * D_FF), F32),
                        pltpu.VMEM((4, bb * (tm + 8), FF_CHUNK), F32)],
        compiler_params=pltpu.CompilerParams(dimension_semantics=("arbitrary", "arbitrary"),
                                             vmem_limit_bytes=VMEM_LIMIT),
        name="outproj_convffn",
    )(x, ya, ob, mod3, cstate, wts["onw"], wts["n2w"], wts["wo"], wts["wu"], wts["wc"], wts["bc"], wts["wd"])


def _prep_weights(norm1_w, w_in, b_ig, b_fg, ml_norm_w, sb_q_norm_w, sb_k_norm_w, sb_out_norm_w, w_out,
                  norm2_w, w_up, w_conv, b_conv, w_down):
    nm = 4 * ML_WIDTH
    ng = 2 * ML_HEADS
    w_g = w_in[:, nm:nm + ng]
    b_g = jnp.concatenate([b_ig, b_fg])
    return {
        "n1w": norm1_w.reshape(1, D_MODEL),
        "wm": w_in[:, :nm].astype(BF16),
        "wg": jnp.pad(w_g, ((0, 0), (0, LANES - ng))).astype(BF16),
        "wgt": jnp.pad(w_g.T, ((0, 16 - ng), (0, 0))).astype(BF16),
        "ws": w_in[:, nm + ng:].astype(BF16),
        "bgc": jnp.pad(b_g, (0, LANES - ng)).reshape(1, LANES),
        "bgr": jnp.pad(b_g, (0, 16 - ng)).reshape(16, 1),
        "qnw": jnp.tile(sb_q_norm_w, 2).reshape(1, LANES),
        "knw": jnp.tile(sb_k_norm_w, 2).reshape(1, LANES),
        "mnw": ml_norm_w.reshape(1, ML_WIDTH),
        "onw": sb_out_norm_w.reshape(1, SB_WIDTH),
        "n2w": norm2_w.reshape(1, D_MODEL),
        "wo": w_out.astype(BF16),
        "wu": w_up.astype(BF16),
        "wc": jnp.pad(w_conv, ((0, 8 - CONV_W), (0, 0))),
        "bc": b_conv.reshape(1, 2 * D_FF),
        "wd": w_down.astype(BF16),
    }


def _layer(x, mod3, wts, ml_state, k_past, v_past, conv_state, *, bb, tm, chunk, tq, tk):
    B, T, _ = x.shape
    nch = 2 * N_FF_CHUNKS
    qa, ka, va, og, gcol, grow, qb, kr, vr, k2, v2 = _inproj_call(x, mod3, wts, bb=bb, tm=tm,
                                                                  kblk=min(tq, tk))

    if ml_state is None:
        c0 = jnp.zeros((B, ML_HEADS, ML_DIM, 2 * ML_DIM), F32)
        n0 = jnp.zeros((B, 8, ML_WIDTH), F32)
        m0 = jnp.zeros((B, 1, LANES), F32)
    else:
        c_in, n_in, m_in = ml_state
        c0 = jnp.concatenate([c_in, jnp.broadcast_to(n_in[..., None], c_in.shape)], axis=-1)
        n0 = jnp.pad(jnp.broadcast_to(n_in.reshape(B, 1, ML_WIDTH), (B, ML_HEADS, ML_WIDTH)),
                     ((0, 0), (0, 8 - ML_HEADS), (0, 0)))
        m0 = jnp.pad(m_in, ((0, 0), (0, LANES - ML_HEADS))).reshape(B, 1, LANES)
    ya, c_new, n_new, m_new = _mlstm_call(qa, ka, va, og, gcol, grow, wts["mnw"], c0, n0, m0, L=chunk,
                                           cps=min(4, T // chunk))

    if k_past is None:
        ob = _sb_call(qb, k2, v2, k2, v2, tq=tq, tk=tk, self_causal=True)
    else:
        ob = _sb_call(qb, k2, v2, jnp.transpose(k_past, (0, 2, 3, 1)), jnp.transpose(v_past, (0, 2, 3, 1)),
                      tq=tq, tk=tk, self_causal=False)

    if conv_state is None:
        conv_state = jnp.zeros((B, CONV_W - 1, 2 * D_FF), F32)
    y, conv_new = _ffn_call(x, ya, ob, mod3, conv_state, wts, bb=bb, tm=tm)
    return (y, c_new, n_new, m_new[:, 0, :ML_HEADS],
            kr.reshape(B, T, SB_HEADS, SB_DIM), vr.reshape(B, T, SB_HEADS, SB_DIM), conv_new)


def kernel(x_prompt, x_sample, c_prompt, c_sample, state_mlstm_c, state_mlstm_n, state_mlstm_m, cache_sb_k, cache_sb_v, state_ffn_conv, w_ada, b_ada, norm1_w, w_in, b_ig, b_fg, ml_norm_w, sb_q_norm_w, sb_k_norm_w, sb_out_norm_w, w_out, norm2_w, w_up, w_conv, b_conv, w_down):
    depth = w_ada.shape[0]
    B = x_prompt.shape[0]
    y_p, y_s = x_prompt, x_sample
    outs_p = [[] for _ in range(6)]
    outs_s = [[] for _ in range(6)]
    for l in range(depth):
        mod = _ada_call(jnp.concatenate([c_prompt, c_sample], axis=0), w_ada[l], b_ada[l])
        mod3 = mod.reshape(mod.shape[0], 6, D_MODEL)
        wts = _prep_weights(norm1_w[l], w_in[l], b_ig[l], b_fg[l], ml_norm_w[l], sb_q_norm_w[l],
                            sb_k_norm_w[l], sb_out_norm_w[l], w_out[l], norm2_w[l], w_up[l], w_conv[l],
                            b_conv[l], w_down[l])
        res_p = _layer(y_p, mod3[:B], wts, None, None, None, None,
                       bb=1, tm=512, chunk=128, tq=256, tk=128)
        res_s = _layer(y_s, mod3[B:], wts, (state_mlstm_c[l], state_mlstm_n[l], state_mlstm_m[l]),
                       cache_sb_k[l], cache_sb_v[l], state_ffn_conv[l],
                       bb=x_sample.shape[0], tm=x_sample.shape[1], chunk=x_sample.shape[1],
                       tq=x_sample.shape[1], tk=128)
        y_p, y_s = res_p[0], res_s[0]
        for i in range(6):
            outs_p[i].append(res_p[i + 1])
            outs_s[i].append(res_s[i + 1])
    return (y_p, y_s, *[jnp.stack(o) for o in outs_p], *[jnp.stack(o) for o in outs_s])
```

```python
import functools

import numpy as np
import jax
import jax.numpy as jnp
from jax import lax
from jax.experimental import pallas as pl
from jax.experimental.pallas import tpu as pltpu

F32 = jnp.float32
BF16 = jnp.bfloat16

D_MODEL = 1024
ML_HEADS = 4
ML_DIM = 128
ML_WIDTH = ML_HEADS * ML_DIM
SB_HEADS = 8
SB_DIM = 64
SB_WIDTH = SB_HEADS * SB_DIM
D_FF = 2816
CONV_W = 3
EPS = 1e-6
LOG2E = 1.4426950408889634
SKIP_BITS = 160.0

LANES = 128
FF_CHUNK = 256
N_FF_CHUNKS = D_FF // FF_CHUNK
DOWN_GROUP = 4
VMEM_LIMIT = 52 * 1024 * 1024

NT_DIMS = (((1,), (1,)), ((), ()))
TN_DIMS = (((0,), (0,)), ((), ()))


def _vmem_spec():
    return pl.BlockSpec(memory_space=pltpu.VMEM)


def _rms(x, w):
    return x * lax.rsqrt(jnp.mean(x * x, axis=-1, keepdims=True) + EPS) * w


def _softplus(x):
    return jnp.maximum(x, 0.0) + jnp.log(1.0 + jnp.exp(-jnp.abs(x)))


def _split3(x):
    h1 = x.astype(BF16)
    r1 = x - h1.astype(F32)
    h2 = r1.astype(BF16)
    h3 = (r1 - h2.astype(F32)).astype(BF16)
    return h1, h2, h3


def _ada_kernel(c_ref, w_ref, b_ref, o_ref):
    c = c_ref[...]
    s = (c * jax.nn.sigmoid(c)).astype(BF16)
    o_ref[...] = jnp.dot(s, w_ref[...].astype(BF16), preferred_element_type=F32) + b_ref[...]


def _ada_call(c, w_ada, b_ada):
    n, d = c.shape
    nout = w_ada.shape[1]
    return pl.pallas_call(
        _ada_kernel,
        out_shape=jax.ShapeDtypeStruct((n, nout), F32),
        grid=(nout // d,),
        in_specs=[pl.BlockSpec((n, d), lambda j: (0, 0)),
                  pl.BlockSpec((d, d), lambda j: (0, j)),
                  pl.BlockSpec((1, d), lambda j: (0, j))],
        out_specs=pl.BlockSpec((n, d), lambda j: (0, j)),
        compiler_params=pltpu.CompilerParams(dimension_semantics=("arbitrary",)),
        name="adaln",
    )(c, w_ada, b_ada.reshape(1, nout))


def _headnorm64(x, w128):
    lane = lax.broadcasted_iota(jnp.int32, x.shape, 1)
    lo = lane < SB_DIM
    x2 = x * x
    s0 = jnp.sum(jnp.where(lo, x2, 0.0), axis=-1, keepdims=True)
    s1 = jnp.sum(jnp.where(lo, 0.0, x2), axis=-1, keepdims=True)
    r0 = lax.rsqrt(s0 * (1.0 / SB_DIM) + EPS)
    r1 = lax.rsqrt(s1 * (1.0 / SB_DIM) + EPS)
    return x * jnp.where(lo, r0, r1) * w128


def _inproj_kernel(x_ref, mod_ref, n1w_ref, wm_ref, wg_ref, wgt_ref, ws_ref, bgc_ref, bgr_ref,
                   qnw_ref, knw_ref,
                   qa_ref, ka_ref, va_ref, og_ref, gcol_ref, grow_ref, qb_ref, kr_ref, vr_ref,
                   k2_ref, v2_ref, *, bb, tm, kblk):
    hs = []
    for bi in range(bb):
        m = mod_ref[bi]
        y = _rms(x_ref[bi], n1w_ref[...])
        hs.append((y * (1.0 + m[1:2, :]) + m[0:1, :]).astype(BF16))
    h = hs[0] if bb == 1 else jnp.concatenate(hs, axis=0)

    def put(ref, val):
        for bi in range(bb):
            ref[bi] = val[bi * tm:(bi + 1) * tm].astype(ref.dtype)

    def proj(w_ref, j):
        return jnp.dot(h, w_ref[:, j * ML_WIDTH:(j + 1) * ML_WIDTH], preferred_element_type=F32)

    put(qa_ref, proj(wm_ref, 0))
    put(ka_ref, proj(wm_ref, 1) * (ML_DIM ** -0.5))
    put(va_ref, proj(wm_ref, 2))
    put(og_ref, jax.nn.sigmoid(proj(wm_ref, 3)))

    gc = jnp.dot(h, wg_ref[...], preferred_element_type=F32) + bgc_ref[...]
    lane = lax.broadcasted_iota(jnp.int32, gc.shape, 1)
    put(gcol_ref, jnp.where(lane >= ML_HEADS, -_softplus(-gc), gc))
    gr = lax.dot_general(wgt_ref[...], h, NT_DIMS, preferred_element_type=F32) + bgr_ref[...]
    row = lax.broadcasted_iota(jnp.int32, gr.shape, 0)
    gr = jnp.where(row >= ML_HEADS, -_softplus(-gr), gr)
    for bi in range(bb):
        grow_ref[bi] = gr[:2 * ML_HEADS, bi * tm:(bi + 1) * tm]

    sq = proj(ws_ref, 0)
    sk = proj(ws_ref, 1)
    sv = proj(ws_ref, 2)
    qn, kn = [], []
    for j in range(SB_WIDTH // LANES):
        sl = slice(j * LANES, (j + 1) * LANES)
        qn.append(_headnorm64(sq[:, sl], qnw_ref[...]) * (SB_DIM ** -0.5 * LOG2E))
        kn.append(_headnorm64(sk[:, sl], knw_ref[...]))
    qn = jnp.concatenate(qn, axis=1)
    kn = jnp.concatenate(kn, axis=1)
    put(qb_ref, qn)
    put(kr_ref, kn)
    put(vr_ref, sv)

    even = (lax.broadcasted_iota(jnp.int32, kn.shape, 1) & SB_DIM) == 0

    def put_pairs(ref, val):
        ve = jnp.where(even, val, 0.0).astype(BF16)
        vo = jnp.where(even, 0.0, val).astype(BF16)
        for bi in range(bb):
            parts = []
            for r0 in range(bi * tm, (bi + 1) * tm, kblk):
                parts += [ve[r0:r0 + kblk], vo[r0:r0 + kblk]]
            ref[bi] = jnp.concatenate(parts, axis=0)

    put_pairs(k2_ref, kn)
    put_pairs(v2_ref, sv)


def _inproj_call(x, mod3, wts, *, bb, tm, kblk):
    B, T, D = x.shape
    grid = (B // bb, T // tm)
    tok = lambda b, t: (b, t, 0)

    def tok_spec(w):
        return pl.BlockSpec((bb, tm, w), tok)

    def tok_shape(w, dt):
        return jax.ShapeDtypeStruct((B, T, w), dt)

    out_shape = [tok_shape(ML_WIDTH, BF16), tok_shape(ML_WIDTH, BF16), tok_shape(ML_WIDTH, BF16),
                 tok_shape(ML_WIDTH, F32), tok_shape(LANES, F32),
                 jax.ShapeDtypeStruct((B, 2 * ML_HEADS, T), F32),
                 tok_shape(SB_WIDTH, BF16), tok_shape(SB_WIDTH, F32), tok_shape(SB_WIDTH, F32),
                 jax.ShapeDtypeStruct((B, 2 * T, SB_WIDTH), BF16),
                 jax.ShapeDtypeStruct((B, 2 * T, SB_WIDTH), BF16)]
    pair_spec = pl.BlockSpec((bb, 2 * tm, SB_WIDTH), tok)
    out_specs = [tok_spec(ML_WIDTH), tok_spec(ML_WIDTH), tok_spec(ML_WIDTH), tok_spec(ML_WIDTH),
                 tok_spec(LANES), pl.BlockSpec((bb, 2 * ML_HEADS, tm), lambda b, t: (b, 0, t)),
                 tok_spec(SB_WIDTH), tok_spec(SB_WIDTH), tok_spec(SB_WIDTH), pair_spec, pair_spec]
    in_specs = [tok_spec(D), pl.BlockSpec((bb, 6, D), lambda b, t: (b, 0, 0))] + [_vmem_spec()] * 9
    return pl.pallas_call(
        functools.partial(_inproj_kernel, bb=bb, tm=tm, kblk=kblk),
        out_shape=out_shape, grid=grid, in_specs=in_specs, out_specs=out_specs,
        compiler_params=pltpu.CompilerParams(dimension_semantics=("arbitrary", "arbitrary"),
                                             vmem_limit_bytes=VMEM_LIMIT),
        name="inproj",
    )(x, mod3, wts["n1w"], wts["wm"], wts["wg"], wts["wgt"], wts["ws"], wts["bgc"], wts["bgr"],
      wts["qnw"], wts["knw"])


def _mlstm_kernel(q_ref, k_ref, v_ref, og_ref, gcol_ref, grow_ref, nw_ref, c0_ref, n0_ref, m0_ref,
                  tri_ref, trit_ref,
                  ya_ref, cout_ref, nout_ref, mout_ref,
                  c_scr, n_scr, m_scr, *, L, cps):
    @pl.when(pl.program_id(1) == 0)
    def _():
        c_scr[...] = c0_ref[0]
        n_scr[...] = n0_ref[0]
        m_scr[...] = m0_ref[0]

    heads = range(ML_HEADS)
    hss = [slice(h * ML_DIM, (h + 1) * ML_DIM) for h in heads]
    rowss = [slice(cc * L, (cc + 1) * L) for cc in range(cps)]
    r_i = lax.broadcasted_iota(jnp.int32, (L, L), 0)
    c_i = lax.broadcasted_iota(jnp.int32, (L, L), 1)
    causal = c_i <= r_i
    lane1 = lax.broadcasted_iota(jnp.int32, (1, LANES), 1)
    ones = jnp.ones((L, LANES), BF16)

    qks = []
    for rows in rowss:
        qks.append([lax.dot_general(q_ref[0, rows, hs], k_ref[0, rows, hs], NT_DIMS,
                                    preferred_element_type=F32) for hs in hss])
    gcols = [gcol_ref[0, rows, :] for rows in rowss]
    grows = [grow_ref[0, :, rows] for rows in rowss]
    col_parts = [p for g in gcols for p in _split3(g)]
    row_parts = [p for g in grows for p in _split3(jnp.concatenate([g, jnp.zeros_like(g)], axis=0))]
    bc_all = jnp.dot(tri_ref[...], jnp.concatenate(col_parts, axis=1), preferred_element_type=F32)
    br_all = jnp.dot(jnp.concatenate(row_parts, axis=0), trit_ref[...], preferred_element_type=F32)
    gates = []
    for cc in range(cps):
        bcol = sum(bc_all[:, (3 * cc + i) * LANES:(3 * cc + i + 1) * LANES] for i in range(3))
        brow = sum(br_all[(3 * cc + i) * 16:(3 * cc + i + 1) * 16, :] for i in range(3))
        gates.append((gcols[cc], grows[cc], bcol, brow))

    m_vec = m_scr[...]
    m_run = [m_vec[:, h:h + 1] for h in heads]
    plan = []

    def prepare(cc):
        rows = rowss[cc]
        gcol, grow, bcol, brow = gates[cc]
        per_head = []
        for h in heads:
            b_c = jnp.broadcast_to(bcol[:, ML_HEADS + h:ML_HEADS + h + 1], (L, LANES))
            ig_c = jnp.broadcast_to(gcol[:, h:h + 1], (L, LANES))
            b_r = brow[ML_HEADS + h:ML_HEADS + h + 1, :]
            ig_r = grow[h:h + 1, :]
            b_last = b_r[:, L - 1:L]
            g_r = ig_r - b_r
            m_h = m_run[h]
            dlog = jnp.where(causal, b_c[:, :L] + g_r, -jnp.inf)
            inter = b_c + m_h
            m_t = jnp.maximum(inter, jnp.max(dlog, axis=-1, keepdims=True))
            w_intra = jnp.exp(dlog - m_t[:, :L])
            wk_r = b_last + g_r
            m_new = jnp.maximum(b_last + m_h, jnp.max(wk_r, axis=-1, keepdims=True))
            wk_c = jnp.exp(b_last - b_c + ig_c - m_new)
            v = v_ref[0, rows, hss[h]]
            per_head.append({
                "s": (qks[cc][h] * w_intra).astype(BF16),
                "w_inter": jnp.exp(inter - m_t),
                "floor": jnp.exp(-m_t),
                "decay": jnp.exp(b_last + m_h - m_new),
                "vaug": jnp.concatenate([v, ones], axis=1),
                "wv": jnp.concatenate([(wk_c * v.astype(F32)).astype(BF16), wk_c.astype(BF16)], axis=1),
                "wk_row": jnp.exp(wk_r - m_new),
            })
            m_run[h] = m_new
        for h in heads:
            ph = per_head[h]
            ph["intra"] = jnp.dot(ph["s"], ph["vaug"], preferred_element_type=F32)
            ph["upd"] = lax.dot_general(k_ref[0, rows, hss[h]], ph["wv"], TN_DIMS, preferred_element_type=F32)
        wk16 = jnp.concatenate([per_head[h]["wk_row"] for h in heads]
                               + [jnp.zeros((16 - ML_HEADS, L), F32)], axis=0).astype(BF16)
        nupd = jnp.dot(wk16, k_ref[0, rows, :], preferred_element_type=F32)
        plan.append((per_head, nupd))

    def advance(cc):
        rows = rowss[cc]
        per_head, nupd = plan[cc]
        for h in heads:
            ph = per_head[h]
            hs = hss[h]
            cst = c_scr[h]
            qc = jnp.dot(q_ref[0, rows, hs], cst.astype(BF16), preferred_element_type=F32)
            num = ph["w_inter"] * qc[:, :ML_DIM] + ph["intra"][:, :ML_DIM]
            den = ph["w_inter"] * qc[:, ML_DIM:] + ph["intra"][:, ML_DIM:]
            hcell = num / jnp.maximum(jnp.abs(den), ph["floor"])
            ya_ref[0, rows, hs] = (_rms(hcell, nw_ref[:, hs]) * og_ref[0, rows, hs]).astype(BF16)
            c_scr[h] = ph["decay"] * cst + ph["upd"]
            n_scr[h:h + 1, hs] = ph["decay"] * n_scr[h:h + 1, hs] + nupd[h:h + 1, hs]

    for cc in range(cps):
        prepare(cc)
    for cc in range(cps):
        advance(cc)

    m_out = m_vec
    for h in heads:
        cout_ref[0, h] = c_scr[h][:, :ML_DIM]
        nout_ref[0, h:h + 1, :] = n_scr[h:h + 1, hss[h]]
        m_out = jnp.where(lane1 == h, m_run[h], m_out)
    m_scr[...] = m_out
    mout_ref[0] = m_out


def _mlstm_call(qa, ka, va, og, gcol, grow, nw, c0, n0, m0, *, L, cps):
    B, T, _ = qa.shape
    tri_np = np.tril(np.ones((L, L), np.float32))
    tri = jnp.asarray(tri_np, BF16)
    trit = jnp.asarray(tri_np.T, BF16)
    tok = lambda b, c: (b, c, 0)
    bat3 = lambda b, c: (b, 0, 0)
    bat4 = lambda b, c: (b, 0, 0, 0)
    tl = L * cps
    in_specs = [pl.BlockSpec((1, tl, ML_WIDTH), tok)] * 4 + [
        pl.BlockSpec((1, tl, LANES), tok),
        pl.BlockSpec((1, 2 * ML_HEADS, tl), lambda b, c: (b, 0, c)),
        _vmem_spec(),
        pl.BlockSpec((1, ML_HEADS, ML_DIM, 2 * ML_DIM), bat4),
        pl.BlockSpec((1, 8, ML_WIDTH), bat3),
        pl.BlockSpec((1, 1, LANES), bat3),
        _vmem_spec(), _vmem_spec()]
    out_shape = [jax.ShapeDtypeStruct((B, T, ML_WIDTH), BF16),
                 jax.ShapeDtypeStruct((B, ML_HEADS, ML_DIM, ML_DIM), F32),
                 jax.ShapeDtypeStruct((B, ML_HEADS, ML_DIM), F32),
                 jax.ShapeDtypeStruct((B, 1, LANES), F32)]
    out_specs = [pl.BlockSpec((1, tl, ML_WIDTH), tok),
                 pl.BlockSpec((1, ML_HEADS, ML_DIM, ML_DIM), bat4),
                 pl.BlockSpec((1, ML_HEADS, ML_DIM), bat3),
                 pl.BlockSpec((1, 1, LANES), bat3)]
    return pl.pallas_call(
        functools.partial(_mlstm_kernel, L=L, cps=cps),
        out_shape=out_shape, grid=(B, T // tl), in_specs=in_specs, out_specs=out_specs,
        scratch_shapes=[pltpu.VMEM((ML_HEADS, ML_DIM, 2 * ML_DIM), F32),
                        pltpu.VMEM((8, ML_WIDTH), F32),
                        pltpu.VMEM((1, LANES), F32)],
        compiler_params=pltpu.CompilerParams(dimension_semantics=("arbitrary", "arbitrary"),
                                             vmem_limit_bytes=VMEM_LIMIT),
        name="mlstm",
    )(qa, ka, va, og, gcol, grow, nw, c0, n0, m0, tri, trit)


def _suffix_matrix(n):
    u = np.zeros((2 * n, 2 * LANES), np.float32)
    j = np.arange(n)[:, None]
    s = np.arange(n)[None, :]
    u[:n, :n] = (j >= s)
    u[n:, :n] = (j >= s)
    u[:, LANES:] = 1.0
    return jnp.asarray(u, BF16)


def _sb_tiles(jobs, u_ref, n, transposed=False):
    fused = n % LANES == 0
    zs = []
    for q_p, k2, _, _ in jobs:
        if transposed:
            z2 = jnp.dot(q_p, k2, preferred_element_type=F32)
            zs.append([z2[:, :n], z2[:, n:]])
        elif fused:
            z2 = lax.dot_general(q_p, k2, NT_DIMS, preferred_element_type=F32)
            zs.append([z2[:, :n], z2[:, n:]])
        else:
            zs.append([lax.dot_general(q_p, k2[hh * n:(hh + 1) * n], NT_DIMS, preferred_element_type=F32)
                       for hh in range(2)])
    splits = []
    for (_, _, _, mask), zj in zip(jobs, zs):
        sj = []
        for z in zj:
            s = jnp.maximum(z, 0.0) + jnp.log2(1.0 + jnp.exp2(-jnp.abs(z)))
            if mask is not None:
                s = jnp.where(mask, s, 0.0)
            hi = s.astype(BF16)
            sj.append((hi, (s - hi.astype(F32)).astype(BF16)))
        splits.append(sj)
    weights = []
    for (_, _, _, mask), zj, sj in zip(jobs, zs, splits):
        wj = []
        for z, (hi, lo) in zip(zj, sj):
            if fused:
                res = jnp.dot(jnp.concatenate([hi, lo], axis=1), u_ref[...], preferred_element_type=F32)
            else:
                res = (jnp.dot(hi, u_ref[:n, :], preferred_element_type=F32)
                       + jnp.dot(lo, u_ref[n:, :], preferred_element_type=F32))
            a = jnp.exp2(z - res[:, :n])
            if mask is not None:
                a = jnp.where(mask, a, 0.0)
            wj.append((a.astype(BF16), res[:, LANES:]))
        weights.append(wj)
    outs = []
    for (q_p, _, v2, _), wj in zip(jobs, weights):
        if transposed:
            o = lax.dot_general(jnp.concatenate([wj[0][0], wj[1][0]], axis=1), v2, NT_DIMS,
                                preferred_element_type=F32)
        elif fused:
            o = jnp.dot(jnp.concatenate([wj[0][0], wj[1][0]], axis=1), v2, preferred_element_type=F32)
        else:
            o = (jnp.dot(wj[0][0], v2[:n], preferred_element_type=F32)
                 + jnp.dot(wj[1][0], v2[n:], preferred_element_type=F32))
        lane = lax.broadcasted_iota(jnp.int32, (q_p.shape[0], LANES), 1)
        outs.append((o, jnp.where(lane < SB_DIM, wj[0][1], wj[1][1])))
    return outs


def _sb_kernel(q_ref, kd_ref, vd_ref, kp_ref, vp_ref, ud_ref, up_ref, o_ref, acc_ref, used_ref,
               *, tq, tk, nd, kpi, ntrip_static, transposed_cache):
    npairs = SB_WIDTH // LANES
    pss = [slice(p * LANES, (p + 1) * LANES) for p in range(npairs)]
    acc_ref[...] = jnp.zeros_like(acc_ref)
    used_ref[...] = jnp.zeros_like(used_ref)

    def absorb(results, row0s):
        for idx, (o, r) in enumerate(results):
            ps = pss[idx % npairs]
            rs = slice(row0s[idx], tq)
            used = used_ref[rs, ps]
            acc_ref[rs, ps] = acc_ref[rs, ps] + jnp.exp2(-used) * o
            used_ref[rs, ps] = used + r

    jobs, row0s = [], []
    for d in reversed(range(tq // nd)):
        r_i = lax.broadcasted_iota(jnp.int32, (tq - d * nd, nd), 0)
        c_i = lax.broadcasted_iota(jnp.int32, (tq - d * nd, nd), 1)
        for ps in pss:
            jobs.append((q_ref[0, d * nd:, ps], kd_ref[0, 2 * d * nd:2 * (d + 1) * nd, ps],
                         vd_ref[0, 2 * d * nd:2 * (d + 1) * nd, ps], c_i < r_i))
            row0s.append(d * nd)
    absorb(_sb_tiles(jobs, ud_ref, nd), row0s)

    def cache_pair(ref, cols, p):
        even = ref[0, 2 * p, :, cols]
        odd = ref[0, 2 * p + 1, :, cols]
        zero = jnp.zeros_like(even)
        return jnp.concatenate([jnp.concatenate([even, zero], axis=1),
                                jnp.concatenate([zero, odd], axis=1)], axis=0).astype(BF16)

    def past_jobs(j):
        jobs = []
        for kk in reversed(range(kpi)):
            blk = j * kpi + kk
            if transposed_cache:
                cols = pl.ds(pl.multiple_of(blk * tk, tk), tk)
                for p, ps in enumerate(pss):
                    jobs.append((q_ref[0, :, ps], cache_pair(kp_ref, cols, p), cache_pair(vp_ref, cols, p), None))
            else:
                rows = pl.ds(pl.multiple_of(blk * 2 * tk, 2 * tk), 2 * tk)
                for ps in pss:
                    jobs.append((q_ref[0, :, ps], kp_ref[0, rows, ps], vp_ref[0, rows, ps], None))
        return jobs

    def cond(carry):
        j, least_used = carry
        return jnp.logical_and(j >= 0, least_used < SKIP_BITS)

    def body(carry):
        j, _ = carry
        absorb(_sb_tiles(past_jobs(j), up_ref, tk, transposed=transposed_cache), [0] * (kpi * npairs))
        return j - 1, jnp.min(used_ref[...])

    ntrip = pl.program_id(1) if ntrip_static is None else ntrip_static
    lax.while_loop(cond, body, (ntrip - 1, jnp.min(used_ref[...])))
    o_ref[0] = acc_ref[...]


def _sb_call(qb, kd, vd, kp, vp, *, tq, tk, self_causal):
    B, T, W = qb.shape
    nd = min(tq, tk)
    tok = lambda b, i: (b, i, 0)
    if self_causal:
        Tp = kp.shape[1] // 2
        kpi = tq // tk
        past_spec = pl.BlockSpec((1, 2 * Tp, W), lambda b, i: (b, 0, 0))
    else:
        Tp = kp.shape[3]
        kpi = 2
        past_spec = pl.BlockSpec((1,) + kp.shape[1:], lambda b, i: (b, 0, 0, 0))
    in_specs = [pl.BlockSpec((1, tq, W), tok), pl.BlockSpec((1, 2 * tq, W), tok), pl.BlockSpec((1, 2 * tq, W), tok),
                past_spec, past_spec, _vmem_spec(), _vmem_spec()]
    return pl.pallas_call(
        functools.partial(_sb_kernel, tq=tq, tk=tk, nd=nd, kpi=kpi, transposed_cache=not self_causal,
                          ntrip_static=None if self_causal else Tp // (tk * kpi)),
        out_shape=jax.ShapeDtypeStruct((B, T, W), F32),
        grid=(B, T // tq), in_specs=in_specs, out_specs=pl.BlockSpec((1, tq, W), tok),
        scratch_shapes=[pltpu.VMEM((tq, W), F32), pltpu.VMEM((tq, W), F32)],
        compiler_params=pltpu.CompilerParams(dimension_semantics=("arbitrary", "arbitrary"),
                                             vmem_limit_bytes=VMEM_LIMIT),
        name="stickbreak",
    )(qb, kd, vd, kp, vp, _suffix_matrix(nd), _suffix_matrix(tk))


def _ffn_kernel(x_ref, ya_ref, ob_ref, mod_ref, cst_ref, onw_ref, n2w_ref, wo_ref, wu_ref, wc_ref,
                bc_ref, wd_ref,
                y_ref, cnew_ref, tail_ref, ubuf_ref, *, bb, tm):
    R = bb * tm

    @pl.when(pl.program_id(1) == 0)
    def _():
        tail_ref[...] = cst_ref[...]

    ycat = []
    for bi in range(bb):
        yb = _rms(ob_ref[bi], onw_ref[...]).astype(BF16)
        ycat.append(jnp.concatenate([ya_ref[bi], yb], axis=1))
    ycat = ycat[0] if bb == 1 else jnp.concatenate(ycat, axis=0)
    mix = jnp.dot(ycat, wo_ref[...], preferred_element_type=F32)

    x1s, h2s = [], []
    for bi in range(bb):
        m = mod_ref[bi]
        x1 = x_ref[bi] + m[2:3, :] * mix[bi * tm:(bi + 1) * tm]
        x1s.append(x1)
        h2s.append((_rms(x1, n2w_ref[...]) * (1.0 + m[4:5, :]) + m[3:4, :]).astype(BF16))
    h2 = h2s[0] if bb == 1 else jnp.concatenate(h2s, axis=0)

    def conv(col0, slot):
        cols = slice(col0, col0 + FF_CHUNK)
        u = jnp.dot(h2, wu_ref[:, cols], preferred_element_type=F32)
        w0, w1, w2 = wc_ref[0:1, cols], wc_ref[1:2, cols], wc_ref[2:3, cols]
        outs = []
        for bi in range(bb):
            ub = u[bi * tm:(bi + 1) * tm]
            base = bi * (tm + 8)
            ubuf_ref[slot, base + 6:base + 8, :] = tail_ref[bi, :, cols]
            ubuf_ref[slot, base + 8:base + 8 + tm, :] = ub
            u1 = ubuf_ref[slot, base + 7:base + 7 + tm, :]
            u2 = ubuf_ref[slot, base + 6:base + 6 + tm, :]
            outs.append(u2 * w0 + u1 * w1 + ub * w2 + bc_ref[:, cols])
            tail_ref[bi, :, cols] = ubuf_ref[slot, base + 6 + tm:base + 8 + tm, :]
        return outs[0] if bb == 1 else jnp.concatenate(outs, axis=0)

    def gated(c):
        ua = conv(c * FF_CHUNK, 2 * (c % 2))
        ug = conv(D_FF + c * FF_CHUNK, 2 * (c % 2) + 1)
        return (ua * jax.nn.sigmoid(ua) * ug).astype(BF16)

    acc = None
    pending, first = [gated(0)], 0
    for c in range(N_FF_CHUNKS):
        nxt = gated(c + 1) if c + 1 < N_FF_CHUNKS else None
        if len(pending) == DOWN_GROUP or nxt is None:
            lhs = pending[0] if len(pending) == 1 else jnp.concatenate(pending, axis=1)
            part = jnp.dot(lhs, wd_ref[first * FF_CHUNK:(c + 1) * FF_CHUNK, :], preferred_element_type=F32)
            acc = part if acc is None else acc + part
            pending, first = [], c + 1
        if nxt is not None:
            pending.append(nxt)

    for bi in range(bb):
        y_ref[bi] = x1s[bi] + mod_ref[bi][5:6, :] * acc[bi * tm:(bi + 1) * tm]

    @pl.when(pl.program_id(1) == pl.num_programs(1) - 1)
    def _():
        cnew_ref[...] = tail_ref[...]


def _ffn_call(x, ya, ob, mod3, cstate, wts, *, bb, tm):
    B, T, D = x.shape
    tok = lambda b, t: (b, t, 0)
    bat = lambda b, t: (b, 0, 0)
    in_specs = [pl.BlockSpec((bb, tm, D), tok), pl.BlockSpec((bb, tm, ML_WIDTH), tok),
                pl.BlockSpec((bb, tm, SB_WIDTH), tok), pl.BlockSpec((bb, 6, D), bat),
                pl.BlockSpec((bb, CONV_W - 1, 2 * D_FF), bat)] + [_vmem_spec()] * 7
    out_shape = [jax.ShapeDtypeStruct((B, T, D), F32), jax.ShapeDtypeStruct((B, CONV_W - 1, 2 * D_FF), F32)]
    out_specs = [pl.BlockSpec((bb, tm, D), tok), pl.BlockSpec((bb, CONV_W - 1, 2 * D_FF), bat)]
    return pl.pallas_call(
        functools.partial(_ffn_kernel, bb=bb, tm=tm),
        out_shape=out_shape, grid=(B // bb, T // tm), in_specs=in_specs, out_specs=out_specs,
        scratch_shapes=[pltpu.VMEM((bb, CONV_W - 1, 2 * D_FF), F32),
                        pltpu.VMEM((4, bb * (tm + 8), FF_CHUNK), F32)],
        compiler_params=pltpu.CompilerParams(dimension_semantics=("arbitrary", "arbitrary"),
                                             vmem_limit_bytes=VMEM_LIMIT),
        name="outproj_convffn",
    )(x, ya, ob, mod3, cstate, wts["onw"], wts["n2w"], wts["wo"], wts["wu"], wts["wc"], wts["bc"], wts["wd"])


def _prep_weights(norm1_w, w_in, b_ig, b_fg, ml_norm_w, sb_q_norm_w, sb_k_norm_w, sb_out_norm_w, w_out,
                  norm2_w, w_up, w_conv, b_conv, w_down):
    nm = 4 * ML_WIDTH
    ng = 2 * ML_HEADS
    w_g = w_in[:, nm:nm + ng]
    b_g = jnp.concatenate([b_ig, b_fg])
    return {
        "n1w": norm1_w.reshape(1, D_MODEL),
        "wm": w_in[:, :nm].astype(BF16),
        "wg": jnp.pad(w_g, ((0, 0), (0, LANES - ng))).astype(BF16),
        "wgt": jnp.pad(w_g.T, ((0, 16 - ng), (0, 0))).astype(BF16),
        "ws": w_in[:, nm + ng:].astype(BF16),
        "bgc": jnp.pad(b_g, (0, LANES - ng)).reshape(1, LANES),
        "bgr": jnp.pad(b_g, (0, 16 - ng)).reshape(16, 1),
        "qnw": jnp.tile(sb_q_norm_w, 2).reshape(1, LANES),
        "knw": jnp.tile(sb_k_norm_w, 2).reshape(1, LANES),
        "mnw": ml_norm_w.reshape(1, ML_WIDTH),
        "onw": sb_out_norm_w.reshape(1, SB_WIDTH),
        "n2w": norm2_w.reshape(1, D_MODEL),
        "wo": w_out.astype(BF16),
        "wu": w_up.astype(BF16),
        "wc": jnp.pad(w_conv, ((0, 8 - CONV_W), (0, 0))),
        "bc": b_conv.reshape(1, 2 * D_FF),
        "wd": w_down.astype(BF16),
    }


def _layer(x, mod3, wts, ml_state, k_past, v_past, conv_state, *, bb, tm, chunk, tq, tk):
    B, T, _ = x.shape
    nch = 2 * N_FF_CHUNKS
    qa, ka, va, og, gcol, grow, qb, kr, vr, k2, v2 = _inproj_call(x, mod3, wts, bb=bb, tm=tm,
                                                                  kblk=min(tq, tk))

    if ml_state is None:
        c0 = jnp.zeros((B, ML_HEADS, ML_DIM, 2 * ML_DIM), F32)
        n0 = jnp.zeros((B, 8, ML_WIDTH), F32)
        m0 = jnp.zeros((B, 1, LANES), F32)
    else:
        c_in, n_in, m_in = ml_state
        c0 = jnp.concatenate([c_in, jnp.broadcast_to(n_in[..., None], c_in.shape)], axis=-1)
        n0 = jnp.pad(jnp.broadcast_to(n_in.reshape(B, 1, ML_WIDTH), (B, ML_HEADS, ML_WIDTH)),
                     ((0, 0), (0, 8 - ML_HEADS), (0, 0)))
        m0 = jnp.pad(m_in, ((0, 0), (0, LANES - ML_HEADS))).reshape(B, 1, LANES)
    ya, c_new, n_new, m_new = _mlstm_call(qa, ka, va, og, gcol, grow, wts["mnw"], c0, n0, m0, L=chunk,
                                           cps=min(4, T // chunk))

    if k_past is None:
        ob = _sb_call(qb, k2, v2, k2, v2, tq=tq, tk=tk, self_causal=True)
    else:
        ob = _sb_call(qb, k2, v2, jnp.transpose(k_past, (0, 2, 3, 1)), jnp.transpose(v_past, (0, 2, 3, 1)),
                      tq=tq, tk=tk, self_causal=False)

    if conv_state is None:
        conv_state = jnp.zeros((B, CONV_W - 1, 2 * D_FF), F32)
    y, conv_new = _ffn_call(x, ya, ob, mod3, conv_state, wts, bb=bb, tm=tm)
    return (y, c_new, n_new, m_new[:, 0, :ML_HEADS],
            kr.reshape(B, T, SB_HEADS, SB_DIM), vr.reshape(B, T, SB_HEADS, SB_DIM), conv_new)


def kernel(x_prompt, x_sample, c_prompt, c_sample, state_mlstm_c, state_mlstm_n, state_mlstm_m, cache_sb_k, cache_sb_v, state_ffn_conv, w_ada, b_ada, norm1_w, w_in, b_ig, b_fg, ml_norm_w, sb_q_norm_w, sb_k_norm_w, sb_out_norm_w, w_out, norm2_w, w_up, w_conv, b_conv, w_down):
    depth = w_ada.shape[0]
    B = x_prompt.shape[0]
    y_p, y_s = x_prompt, x_sample
    outs_p = [[] for _ in range(6)]
    outs_s = [[] for _ in range(6)]
    for l in range(depth):
        mod = _ada_call(jnp.concatenate([c_prompt, c_sample], axis=0), w_ada[l], b_ada[l])
        mod3 = mod.reshape(mod.shape[0], 6, D_MODEL)
        wts = _prep_weights(norm1_w[l], w_in[l], b_ig[l], b_fg[l], ml_norm_w[l], sb_q_norm_w[l],
                            sb_k_norm_w[l], sb_out_norm_w[l], w_out[l], norm2_w[l], w_up[l], w_conv[l],
                            b_conv[l], w_down[l])
        res_p = _layer(y_p, mod3[:B], wts, None, None, None, None,
                       bb=1, tm=512, chunk=128, tq=256, tk=128)
        res_s = _layer(y_s, mod3[B:], wts, (state_mlstm_c[l], state_mlstm_n[l], state_mlstm_m[l]),
                       cache_sb_k[l], cache_sb_v[l], state_ffn_conv[l],
                       bb=x_sample.shape[0], tm=x_sample.shape[1], chunk=x_sample.shape[1],
                       tq=x_sample.shape[1], tk=128)
        y_p, y_s = res_p[0], res_s[0]
        for i in range(6):
            outs_p[i].append(res_p[i + 1])
            outs_s[i].append(res_s[i + 1])
    return (y_p, y_s, *[jnp.stack(o) for o in outs_p], *[jnp.stack(o) for o in outs_s])
```

```python
import functools

import numpy as np
import jax
import jax.numpy as jnp
from jax import lax
from jax.experimental import pallas as pl
from jax.experimental.pallas import tpu as pltpu

F32 = jnp.float32
BF16 = jnp.bfloat16

D_MODEL = 1024
ML_HEADS = 4
ML_DIM = 128
ML_WIDTH = ML_HEADS * ML_DIM
SB_HEADS = 8
SB_DIM = 64
SB_WIDTH = SB_HEADS * SB_DIM
D_FF = 2816
CONV_W = 3
EPS = 1e-6
LOG2E = 1.4426950408889634
SKIP_BITS = 160.0

LANES = 128
FF_CHUNK = 256
N_FF_CHUNKS = D_FF // FF_CHUNK
DOWN_GROUP = 4
DOWN_LAG = 1
VMEM_LIMIT = 52 * 1024 * 1024

NT_DIMS = (((1,), (1,)), ((), ()))
TN_DIMS = (((0,), (0,)), ((), ()))


def _vmem_spec():
    return pl.BlockSpec(memory_space=pltpu.VMEM)


def _rms(x, w):
    return x * lax.rsqrt(jnp.mean(x * x, axis=-1, keepdims=True) + EPS) * w


def _softplus(x):
    return jnp.maximum(x, 0.0) + jnp.log(1.0 + jnp.exp(-jnp.abs(x)))


def _split3(x):
    h1 = x.astype(BF16)
    r1 = x - h1.astype(F32)
    h2 = r1.astype(BF16)
    h3 = (r1 - h2.astype(F32)).astype(BF16)
    return h1, h2, h3


def _ada_kernel(c_ref, w_ref, b_ref, o_ref):
    c = c_ref[...]
    s = (c * jax.nn.sigmoid(c)).astype(BF16)
    o_ref[...] = jnp.dot(s, w_ref[...].astype(BF16), preferred_element_type=F32) + b_ref[...]


def _ada_call(c, w_ada, b_ada):
    n, d = c.shape
    nout = w_ada.shape[1]
    return pl.pallas_call(
        _ada_kernel,
        out_shape=jax.ShapeDtypeStruct((n, nout), F32),
        grid=(nout // d,),
        in_specs=[pl.BlockSpec((n, d), lambda j: (0, 0)),
                  pl.BlockSpec((d, d), lambda j: (0, j)),
                  pl.BlockSpec((1, d), lambda j: (0, j))],
        out_specs=pl.BlockSpec((n, d), lambda j: (0, j)),
        compiler_params=pltpu.CompilerParams(dimension_semantics=("arbitrary",)),
        name="adaln",
    )(c, w_ada, b_ada.reshape(1, nout))


def _headnorm64(x, w128):
    lane = lax.broadcasted_iota(jnp.int32, x.shape, 1)
    lo = lane < SB_DIM
    x2 = x * x
    s0 = jnp.sum(jnp.where(lo, x2, 0.0), axis=-1, keepdims=True)
    s1 = jnp.sum(jnp.where(lo, 0.0, x2), axis=-1, keepdims=True)
    r0 = lax.rsqrt(s0 * (1.0 / SB_DIM) + EPS)
    r1 = lax.rsqrt(s1 * (1.0 / SB_DIM) + EPS)
    return x * jnp.where(lo, r0, r1) * w128


def _inproj_kernel(x_ref, mod_ref, n1w_ref, wm_ref, wg_ref, wgt_ref, ws_ref, bgc_ref, bgr_ref,
                   qnw_ref, knw_ref,
                   qa_ref, ka_ref, va_ref, og_ref, gcol_ref, grow_ref, qb_ref, kr_ref, vr_ref,
                   k2_ref, v2_ref, *, bb, tm, kblk):
    hs = []
    for bi in range(bb):
        m = mod_ref[bi]
        y = _rms(x_ref[bi], n1w_ref[...])
        hs.append((y * (1.0 + m[1:2, :]) + m[0:1, :]).astype(BF16))
    h = hs[0] if bb == 1 else jnp.concatenate(hs, axis=0)

    def put(ref, val):
        for bi in range(bb):
            ref[bi] = val[bi * tm:(bi + 1) * tm].astype(ref.dtype)

    def proj(w_ref, j):
        return jnp.dot(h, w_ref[:, j * ML_WIDTH:(j + 1) * ML_WIDTH], preferred_element_type=F32)

    put(qa_ref, proj(wm_ref, 0))
    put(ka_ref, proj(wm_ref, 1) * (ML_DIM ** -0.5))
    put(va_ref, proj(wm_ref, 2))
    put(og_ref, jax.nn.sigmoid(proj(wm_ref, 3)))

    gc = jnp.dot(h, wg_ref[...], preferred_element_type=F32) + bgc_ref[...]
    lane = lax.broadcasted_iota(jnp.int32, gc.shape, 1)
    put(gcol_ref, jnp.where(lane >= ML_HEADS, -_softplus(-gc), gc))
    gr = lax.dot_general(wgt_ref[...], h, NT_DIMS, preferred_element_type=F32) + bgr_ref[...]
    row = lax.broadcasted_iota(jnp.int32, gr.shape, 0)
    gr = jnp.where(row >= ML_HEADS, -_softplus(-gr), gr)
    for bi in range(bb):
        grow_ref[bi] = gr[:2 * ML_HEADS, bi * tm:(bi + 1) * tm]

    sq = proj(ws_ref, 0)
    sk = proj(ws_ref, 1)
    sv = proj(ws_ref, 2)
    qn, kn = [], []
    for j in range(SB_WIDTH // LANES):
        sl = slice(j * LANES, (j + 1) * LANES)
        qn.append(_headnorm64(sq[:, sl], qnw_ref[...]) * (SB_DIM ** -0.5 * LOG2E))
        kn.append(_headnorm64(sk[:, sl], knw_ref[...]))
    qn = jnp.concatenate(qn, axis=1)
    kn = jnp.concatenate(kn, axis=1)
    put(qb_ref, qn)
    put(kr_ref, kn)
    put(vr_ref, sv)

    even = (lax.broadcasted_iota(jnp.int32, kn.shape, 1) & SB_DIM) == 0

    def put_pairs(ref, val):
        ve = jnp.where(even, val, 0.0).astype(BF16)
        vo = jnp.where(even, 0.0, val).astype(BF16)
        for bi in range(bb):
            parts = []
            for r0 in range(bi * tm, (bi + 1) * tm, kblk):
                parts += [ve[r0:r0 + kblk], vo[r0:r0 + kblk]]
            ref[bi] = jnp.concatenate(parts, axis=0)

    put_pairs(k2_ref, kn)
    put_pairs(v2_ref, sv)


def _inproj_call(x, mod3, wts, *, bb, tm, kblk):
    B, T, D = x.shape
    grid = (B // bb, T // tm)
    tok = lambda b, t: (b, t, 0)

    def tok_spec(w):
        return pl.BlockSpec((bb, tm, w), tok)

    def tok_shape(w, dt):
        return jax.ShapeDtypeStruct((B, T, w), dt)

    out_shape = [tok_shape(ML_WIDTH, BF16), tok_shape(ML_WIDTH, BF16), tok_shape(ML_WIDTH, BF16),
                 tok_shape(ML_WIDTH, F32), tok_shape(LANES, F32),
                 jax.ShapeDtypeStruct((B, 2 * ML_HEADS, T), F32),
                 tok_shape(SB_WIDTH, BF16), tok_shape(SB_WIDTH, F32), tok_shape(SB_WIDTH, F32),
                 jax.ShapeDtypeStruct((B, 2 * T, SB_WIDTH), BF16),
                 jax.ShapeDtypeStruct((B, 2 * T, SB_WIDTH), BF16)]
    pair_spec = pl.BlockSpec((bb, 2 * tm, SB_WIDTH), tok)
    out_specs = [tok_spec(ML_WIDTH), tok_spec(ML_WIDTH), tok_spec(ML_WIDTH), tok_spec(ML_WIDTH),
                 tok_spec(LANES), pl.BlockSpec((bb, 2 * ML_HEADS, tm), lambda b, t: (b, 0, t)),
                 tok_spec(SB_WIDTH), tok_spec(SB_WIDTH), tok_spec(SB_WIDTH), pair_spec, pair_spec]
    in_specs = [tok_spec(D), pl.BlockSpec((bb, 6, D), lambda b, t: (b, 0, 0))] + [_vmem_spec()] * 9
    return pl.pallas_call(
        functools.partial(_inproj_kernel, bb=bb, tm=tm, kblk=kblk),
        out_shape=out_shape, grid=grid, in_specs=in_specs, out_specs=out_specs,
        compiler_params=pltpu.CompilerParams(dimension_semantics=("arbitrary", "arbitrary"),
                                             vmem_limit_bytes=VMEM_LIMIT),
        name="inproj",
    )(x, mod3, wts["n1w"], wts["wm"], wts["wg"], wts["wgt"], wts["ws"], wts["bgc"], wts["bgr"],
      wts["qnw"], wts["knw"])


def _mlstm_kernel(q_ref, k_ref, v_ref, og_ref, gcol_ref, grow_ref, nw_ref, c0_ref, n0_ref, m0_ref,
                  tri_ref, trit_ref,
                  ya_ref, cout_ref, nout_ref, mout_ref,
                  c_scr, n_scr, m_scr, *, L, cps):
    @pl.when(pl.program_id(1) == 0)
    def _():
        c_scr[...] = c0_ref[0]
        n_scr[...] = n0_ref[0]
        m_scr[...] = m0_ref[0]

    heads = range(ML_HEADS)
    hss = [slice(h * ML_DIM, (h + 1) * ML_DIM) for h in heads]
    rowss = [slice(cc * L, (cc + 1) * L) for cc in range(cps)]
    r_i = lax.broadcasted_iota(jnp.int32, (L, L), 0)
    c_i = lax.broadcasted_iota(jnp.int32, (L, L), 1)
    causal = c_i <= r_i
    lane1 = lax.broadcasted_iota(jnp.int32, (1, LANES), 1)
    ones = jnp.ones((L, LANES), BF16)

    qks = []
    for rows in rowss:
        qks.append([lax.dot_general(q_ref[0, rows, hs], k_ref[0, rows, hs], NT_DIMS,
                                    preferred_element_type=F32) for hs in hss])
    gcols = [gcol_ref[0, rows, :] for rows in rowss]
    grows = [grow_ref[0, :, rows] for rows in rowss]
    col_parts = [p for g in gcols for p in _split3(g)]
    row_parts = [p for g in grows for p in _split3(jnp.concatenate([g, jnp.zeros_like(g)], axis=0))]
    bc_all = jnp.dot(tri_ref[...], jnp.concatenate(col_parts, axis=1), preferred_element_type=F32)
    br_all = jnp.dot(jnp.concatenate(row_parts, axis=0), trit_ref[...], preferred_element_type=F32)
    gates = []
    for cc in range(cps):
        bcol = sum(bc_all[:, (3 * cc + i) * LANES:(3 * cc + i + 1) * LANES] for i in range(3))
        brow = sum(br_all[(3 * cc + i) * 16:(3 * cc + i + 1) * 16, :] for i in range(3))
        gates.append((gcols[cc], grows[cc], bcol, brow))

    m_vec = m_scr[...]
    m_run = [m_vec[:, h:h + 1] for h in heads]
    plan = []

    def prepare(cc):
        rows = rowss[cc]
        gcol, grow, bcol, brow = gates[cc]
        per_head = []
        for h in heads:
            b_c = jnp.broadcast_to(bcol[:, ML_HEADS + h:ML_HEADS + h + 1], (L, LANES))
            ig_c = jnp.broadcast_to(gcol[:, h:h + 1], (L, LANES))
            b_r = brow[ML_HEADS + h:ML_HEADS + h + 1, :]
            ig_r = grow[h:h + 1, :]
            b_last = b_r[:, L - 1:L]
            g_r = ig_r - b_r
            m_h = m_run[h]
            dlog = jnp.where(causal, b_c[:, :L] + g_r, -jnp.inf)
            inter = b_c + m_h
            m_t = jnp.maximum(inter, jnp.max(dlog, axis=-1, keepdims=True))
            w_intra = jnp.exp(dlog - m_t[:, :L])
            wk_r = b_last + g_r
            m_new = jnp.maximum(b_last + m_h, jnp.max(wk_r, axis=-1, keepdims=True))
            wk_c = jnp.exp(b_last - b_c + ig_c - m_new)
            v = v_ref[0, rows, hss[h]]
            per_head.append({
                "s": (qks[cc][h] * w_intra).astype(BF16),
                "w_inter": jnp.exp(inter - m_t),
                "floor": jnp.exp(-m_t),
                "decay": jnp.exp(b_last + m_h - m_new),
                "vaug": jnp.concatenate([v, ones], axis=1),
                "wv": jnp.concatenate([(wk_c * v.astype(F32)).astype(BF16), wk_c.astype(BF16)], axis=1),
                "wk_row": jnp.exp(wk_r - m_new),
            })
            m_run[h] = m_new
        for h in heads:
            ph = per_head[h]
            ph["intra"] = jnp.dot(ph["s"], ph["vaug"], preferred_element_type=F32)
            ph["upd"] = lax.dot_general(k_ref[0, rows, hss[h]], ph["wv"], TN_DIMS, preferred_element_type=F32)
        wk16 = jnp.concatenate([per_head[h]["wk_row"] for h in heads]
                               + [jnp.zeros((16 - ML_HEADS, L), F32)], axis=0).astype(BF16)
        nupd = jnp.dot(wk16, k_ref[0, rows, :], preferred_element_type=F32)
        plan.append((per_head, nupd))

    def advance(cc):
        rows = rowss[cc]
        per_head, nupd = plan[cc]
        for h in heads:
            ph = per_head[h]
            hs = hss[h]
            cst = c_scr[h]
            qc = jnp.dot(q_ref[0, rows, hs], cst.astype(BF16), preferred_element_type=F32)
            num = ph["w_inter"] * qc[:, :ML_DIM] + ph["intra"][:, :ML_DIM]
            den = ph["w_inter"] * qc[:, ML_DIM:] + ph["intra"][:, ML_DIM:]
            hcell = num / jnp.maximum(jnp.abs(den), ph["floor"])
            ya_ref[0, rows, hs] = (_rms(hcell, nw_ref[:, hs]) * og_ref[0, rows, hs]).astype(BF16)
            c_scr[h] = ph["decay"] * cst + ph["upd"]
            n_scr[h:h + 1, hs] = ph["decay"] * n_scr[h:h + 1, hs] + nupd[h:h + 1, hs]

    for cc in range(cps):
        prepare(cc)
    for cc in range(cps):
        advance(cc)

    m_out = m_vec
    for h in heads:
        cout_ref[0, h] = c_scr[h][:, :ML_DIM]
        nout_ref[0, h:h + 1, :] = n_scr[h:h + 1, hss[h]]
        m_out = jnp.where(lane1 == h, m_run[h], m_out)
    m_scr[...] = m_out
    mout_ref[0] = m_out


def _mlstm_call(qa, ka, va, og, gcol, grow, nw, c0, n0, m0, *, L, cps):
    B, T, _ = qa.shape
    tri_np = np.tril(np.ones((L, L), np.float32))
    tri = jnp.asarray(tri_np, BF16)
    trit = jnp.asarray(tri_np.T, BF16)
    tok = lambda b, c: (b, c, 0)
    bat3 = lambda b, c: (b, 0, 0)
    bat4 = lambda b, c: (b, 0, 0, 0)
    tl = L * cps
    in_specs = [pl.BlockSpec((1, tl, ML_WIDTH), tok)] * 4 + [
        pl.BlockSpec((1, tl, LANES), tok),
        pl.BlockSpec((1, 2 * ML_HEADS, tl), lambda b, c: (b, 0, c)),
        _vmem_spec(),
        pl.BlockSpec((1, ML_HEADS, ML_DIM, 2 * ML_DIM), bat4),
        pl.BlockSpec((1, 8, ML_WIDTH), bat3),
        pl.BlockSpec((1, 1, LANES), bat3),
        _vmem_spec(), _vmem_spec()]
    out_shape = [jax.ShapeDtypeStruct((B, T, ML_WIDTH), BF16),
                 jax.ShapeDtypeStruct((B, ML_HEADS, ML_DIM, ML_DIM), F32),
                 jax.ShapeDtypeStruct((B, ML_HEADS, ML_DIM), F32),
                 jax.ShapeDtypeStruct((B, 1, LANES), F32)]
    out_specs = [pl.BlockSpec((1, tl, ML_WIDTH), tok),
                 pl.BlockSpec((1, ML_HEADS, ML_DIM, ML_DIM), bat4),
                 pl.BlockSpec((1, ML_HEADS, ML_DIM), bat3),
                 pl.BlockSpec((1, 1, LANES), bat3)]
    return pl.pallas_call(
        functools.partial(_mlstm_kernel, L=L, cps=cps),
        out_shape=out_shape, grid=(B, T // tl), in_specs=in_specs, out_specs=out_specs,
        scratch_shapes=[pltpu.VMEM((ML_HEADS, ML_DIM, 2 * ML_DIM), F32),
                        pltpu.VMEM((8, ML_WIDTH), F32),
                        pltpu.VMEM((1, LANES), F32)],
        compiler_params=pltpu.CompilerParams(dimension_semantics=("arbitrary", "arbitrary"),
                                             vmem_limit_bytes=VMEM_LIMIT),
        name="mlstm",
    )(qa, ka, va, og, gcol, grow, nw, c0, n0, m0, tri, trit)


def _suffix_matrix(n):
    u = np.zeros((2 * n, 2 * LANES), np.float32)
    j = np.arange(n)[:, None]
    s = np.arange(n)[None, :]
    u[:n, :n] = (j >= s)
    u[n:, :n] = (j >= s)
    u[:, LANES:] = 1.0
    return jnp.asarray(u, BF16)


def _sb_tiles(jobs, u_ref, n, transposed=False):
    fused = n % LANES == 0
    zs = []
    for q_p, k2, _, _ in jobs:
        if transposed:
            z2 = jnp.dot(q_p, k2, preferred_element_type=F32)
            zs.append([z2[:, :n], z2[:, n:]])
        elif fused:
            z2 = lax.dot_general(q_p, k2, NT_DIMS, preferred_element_type=F32)
            zs.append([z2[:, :n], z2[:, n:]])
        else:
            zs.append([lax.dot_general(q_p, k2[hh * n:(hh + 1) * n], NT_DIMS, preferred_element_type=F32)
                       for hh in range(2)])
    splits = []
    for (_, _, _, mask), zj in zip(jobs, zs):
        sj = []
        for z in zj:
            s = jnp.maximum(z, 0.0) + jnp.log2(1.0 + jnp.exp2(-jnp.abs(z)))
            if mask is not None:
                s = jnp.where(mask, s, 0.0)
            hi = s.astype(BF16)
            sj.append((hi, (s - hi.astype(F32)).astype(BF16)))
        splits.append(sj)
    weights = []
    for (_, _, _, mask), zj, sj in zip(jobs, zs, splits):
        wj = []
        for z, (hi, lo) in zip(zj, sj):
            if fused:
                res = jnp.dot(jnp.concatenate([hi, lo], axis=1), u_ref[...], preferred_element_type=F32)
            else:
                res = (jnp.dot(hi, u_ref[:n, :], preferred_element_type=F32)
                       + jnp.dot(lo, u_ref[n:, :], preferred_element_type=F32))
            a = jnp.exp2(z - res[:, :n])
            if mask is not None:
                a = jnp.where(mask, a, 0.0)
            wj.append((a.astype(BF16), res[:, LANES:]))
        weights.append(wj)
    outs = []
    for (q_p, _, v2, _), wj in zip(jobs, weights):
        if transposed:
            o = lax.dot_general(jnp.concatenate([wj[0][0], wj[1][0]], axis=1), v2, NT_DIMS,
                                preferred_element_type=F32)
        elif fused:
            o = jnp.dot(jnp.concatenate([wj[0][0], wj[1][0]], axis=1), v2, preferred_element_type=F32)
        else:
            o = (jnp.dot(wj[0][0], v2[:n], preferred_element_type=F32)
                 + jnp.dot(wj[1][0], v2[n:], preferred_element_type=F32))
        lane = lax.broadcasted_iota(jnp.int32, (q_p.shape[0], LANES), 1)
        outs.append((o, jnp.where(lane < SB_DIM, wj[0][1], wj[1][1])))
    return outs


def _sb_kernel(q_ref, kd_ref, vd_ref, kp_ref, vp_ref, ud_ref, up_ref, o_ref, acc_ref, used_ref,
               *, tq, tk, nd, kpi, ntrip_static, transposed_cache):
    npairs = SB_WIDTH // LANES
    pss = [slice(p * LANES, (p + 1) * LANES) for p in range(npairs)]
    acc_ref[...] = jnp.zeros_like(acc_ref)
    used_ref[...] = jnp.zeros_like(used_ref)

    def absorb(results, row0s):
        for idx, (o, r) in enumerate(results):
            ps = pss[idx % npairs]
            rs = slice(row0s[idx], tq)
            used = used_ref[rs, ps]
            acc_ref[rs, ps] = acc_ref[rs, ps] + jnp.exp2(-used) * o
            used_ref[rs, ps] = used + r

    jobs, row0s = [], []
    for d in reversed(range(tq // nd)):
        r_i = lax.broadcasted_iota(jnp.int32, (tq - d * nd, nd), 0)
        c_i = lax.broadcasted_iota(jnp.int32, (tq - d * nd, nd), 1)
        for ps in pss:
            jobs.append((q_ref[0, d * nd:, ps], kd_ref[0, 2 * d * nd:2 * (d + 1) * nd, ps],
                         vd_ref[0, 2 * d * nd:2 * (d + 1) * nd, ps], c_i < r_i))
            row0s.append(d * nd)

    def cache_pair(ref, cols, p):
        even = ref[0, 2 * p, :, cols]
        odd = ref[0, 2 * p + 1, :, cols]
        zero = jnp.zeros_like(even)
        return jnp.concatenate([jnp.concatenate([even, zero], axis=1),
                                jnp.concatenate([zero, odd], axis=1)], axis=0).astype(BF16)

    def past_jobs(j):
        jobs = []
        for kk in reversed(range(kpi)):
            blk = j * kpi + kk
            if transposed_cache:
                cols = pl.ds(pl.multiple_of(blk * tk, tk), tk)
                for p, ps in enumerate(pss):
                    jobs.append((q_ref[0, :, ps], cache_pair(kp_ref, cols, p), cache_pair(vp_ref, cols, p), None))
            else:
                rows = pl.ds(pl.multiple_of(blk * 2 * tk, 2 * tk), 2 * tk)
                for ps in pss:
                    jobs.append((q_ref[0, :, ps], kp_ref[0, rows, ps], vp_ref[0, rows, ps], None))
        return jobs

    def cond(carry):
        j, least_used = carry
        return jnp.logical_and(j >= 0, least_used < SKIP_BITS)

    def body(carry):
        j, _ = carry
        absorb(_sb_tiles(past_jobs(j), up_ref, tk, transposed=transposed_cache), [0] * (kpi * npairs))
        return j - 1, jnp.min(used_ref[...])

    ntrip = pl.program_id(1) if ntrip_static is None else ntrip_static
    if ntrip_static is None and nd == tk:
        @pl.when(ntrip > 0)
        def _():
            absorb(_sb_tiles(jobs + past_jobs(ntrip - 1), up_ref, tk), row0s + [0] * (kpi * npairs))

        @pl.when(ntrip == 0)
        def _():
            absorb(_sb_tiles(jobs, ud_ref, nd), row0s)
        first = ntrip - 2
    else:
        absorb(_sb_tiles(jobs, ud_ref, nd), row0s)
        first = ntrip - 1
    lax.while_loop(cond, body, (first, jnp.min(used_ref[...])))
    o_ref[0] = acc_ref[...]


def _sb_call(qb, kd, vd, kp, vp, *, tq, tk, self_causal):
    B, T, W = qb.shape
    nd = min(tq, tk)
    tok = lambda b, i: (b, i, 0)
    if self_causal:
        Tp = kp.shape[1] // 2
        kpi = tq // tk
        past_spec = pl.BlockSpec((1, 2 * Tp, W), lambda b, i: (b, 0, 0))
    else:
        Tp = kp.shape[3]
        kpi = 2
        past_spec = pl.BlockSpec((1,) + kp.shape[1:], lambda b, i: (b, 0, 0, 0))
    in_specs = [pl.BlockSpec((1, tq, W), tok), pl.BlockSpec((1, 2 * tq, W), tok), pl.BlockSpec((1, 2 * tq, W), tok),
                past_spec, past_spec, _vmem_spec(), _vmem_spec()]
    return pl.pallas_call(
        functools.partial(_sb_kernel, tq=tq, tk=tk, nd=nd, kpi=kpi, transposed_cache=not self_causal,
                          ntrip_static=None if self_causal else Tp // (tk * kpi)),
        out_shape=jax.ShapeDtypeStruct((B, T, W), F32),
        grid=(B, T // tq), in_specs=in_specs, out_specs=pl.BlockSpec((1, tq, W), tok),
        scratch_shapes=[pltpu.VMEM((tq, W), F32), pltpu.VMEM((tq, W), F32)],
        compiler_params=pltpu.CompilerParams(dimension_semantics=("arbitrary", "arbitrary"),
                                             vmem_limit_bytes=VMEM_LIMIT),
        name="stickbreak",
    )(qb, kd, vd, kp, vp, _suffix_matrix(nd), _suffix_matrix(tk))


def _ffn_kernel(x_ref, ya_ref, ob_ref, mod_ref, cst_ref, onw_ref, n2w_ref, wo_ref, wu_ref, wc_ref,
                bc_ref, wd_ref,
                y_ref, cnew_ref, tail_ref, stage_ref, *, bb, tm):
    R = bb * tm
    ng = tm // 8
    interleave = ng % 8 == 0

    def restride(val, row0, picks):
        for j in range(D_MODEL // LANES):
            stage_ref[j, row0:row0 + tm, :] = val[:, j * LANES:(j + 1) * LANES]
        return jnp.concatenate(
            [jnp.concatenate([stage_ref[j, pl.ds(row0 + start, 8, stride=step), :]
                              for j in range(D_MODEL // LANES)], axis=1)
             for start, step in picks], axis=0)

    @pl.when(pl.program_id(1) == 0)
    def _():
        tail_ref[...] = cst_ref[...]

    ycat = []
    for bi in range(bb):
        yb = _rms(ob_ref[bi], onw_ref[...]).astype(BF16)
        ycat.append(jnp.concatenate([ya_ref[bi], yb], axis=1))
    ycat = ycat[0] if bb == 1 else jnp.concatenate(ycat, axis=0)
    mix = jnp.dot(ycat, wo_ref[...], preferred_element_type=F32)

    x1s, h2s = [], []
    for bi in range(bb):
        m = mod_ref[bi]
        x1 = x_ref[bi] + m[2:3, :] * mix[bi * tm:(bi + 1) * tm]
        x1s.append(x1)
        hn = _rms(x1, n2w_ref[...]) * (1.0 + m[4:5, :]) + m[3:4, :]
        if interleave:
            hn = restride(hn, bi * tm, [(g, ng) for g in range(ng)])
        h2s.append(hn.astype(BF16))
    h2 = h2s[0] if bb == 1 else jnp.concatenate(h2s, axis=0)

    sub = lax.broadcasted_iota(jnp.int32, (8, FF_CHUNK), 0)

    def conv(col0):
        cols = slice(col0, col0 + FF_CHUNK)
        u = jnp.dot(h2, wu_ref[:, cols], preferred_element_type=F32)
        w0, w1, w2 = wc_ref[0:1, cols], wc_ref[1:2, cols], wc_ref[2:3, cols]
        outs = []
        for bi in range(bb):
            ub = u[bi * tm:(bi + 1) * tm]
            p0 = tail_ref[bi, 0:1, cols]
            p1 = tail_ref[bi, 1:2, cols]
            if interleave:
                back1 = jnp.where(sub == 0, p1, pltpu.roll(ub[tm - 8:], 1, 0))
                back2 = jnp.where(sub == 0, p0, pltpu.roll(ub[tm - 16:tm - 8], 1, 0))
                u1 = jnp.concatenate([back1, ub[:tm - 8]], axis=0)
                u2 = jnp.concatenate([back2, back1, ub[:tm - 16]], axis=0)
                tail_ref[bi, 0:1, cols] = ub[tm - 9:tm - 8]
                tail_ref[bi, 1:2, cols] = ub[tm - 1:]
            else:
                r1 = pltpu.roll(ub, 1, 0)
                r2 = pltpu.roll(ub, 2, 0)
                u1 = jnp.concatenate([jnp.where(sub == 0, p1, r1[:8]), r1[8:]], axis=0)
                u2 = jnp.concatenate([jnp.where(sub == 0, p0, jnp.where(sub == 1, p1, r2[:8])), r2[8:]], axis=0)
                tail_ref[bi, :, cols] = ub[tm - 2:]
            outs.append(u2 * w0 + u1 * w1 + ub * w2 + bc_ref[:, cols])
        return outs[0] if bb == 1 else jnp.concatenate(outs, axis=0)

    def gated(c):
        ua = conv(c * FF_CHUNK)
        ug = conv(D_FF + c * FF_CHUNK)
        return (ua * jax.nn.sigmoid(ua) * ug).astype(BF16)

    groups = [list(range(s, min(s + DOWN_GROUP, N_FF_CHUNKS))) for s in range(0, N_FF_CHUNKS, DOWN_GROUP)]
    acts, acc, done = {}, None, 0
    for c in range(N_FF_CHUNKS + 1):
        if c < N_FF_CHUNKS:
            acts[c] = gated(c)
        while done < len(groups) and (groups[done][-1] + DOWN_LAG <= c or c == N_FF_CHUNKS):
            g = groups[done]
            lhs = acts[g[0]] if len(g) == 1 else jnp.concatenate([acts[i] for i in g], axis=1)
            part = jnp.dot(lhs, wd_ref[g[0] * FF_CHUNK:(g[-1] + 1) * FF_CHUNK, :], preferred_element_type=F32)
            acc = part if acc is None else acc + part
            done += 1

    for bi in range(bb):
        f_out = acc[bi * tm:(bi + 1) * tm]
        if interleave:
            per = ng // 8
            f_out = restride(f_out, bi * tm, [(64 * (G % per) + G // per, 8) for G in range(ng)])
        y_ref[bi] = x1s[bi] + mod_ref[bi][5:6, :] * f_out

    @pl.when(pl.program_id(1) == pl.num_programs(1) - 1)
    def _():
        cnew_ref[...] = tail_ref[...]


def _ffn_call(x, ya, ob, mod3, cstate, wts, *, bb, tm):
    B, T, D = x.shape
    tok = lambda b, t: (b, t, 0)
    bat = lambda b, t: (b, 0, 0)
    in_specs = [pl.BlockSpec((bb, tm, D), tok), pl.BlockSpec((bb, tm, ML_WIDTH), tok),
                pl.BlockSpec((bb, tm, SB_WIDTH), tok), pl.BlockSpec((bb, 6, D), bat),
                pl.BlockSpec((bb, CONV_W - 1, 2 * D_FF), bat)] + [_vmem_spec()] * 7
    out_shape = [jax.ShapeDtypeStruct((B, T, D), F32), jax.ShapeDtypeStruct((B, CONV_W - 1, 2 * D_FF), F32)]
    out_specs = [pl.BlockSpec((bb, tm, D), tok), pl.BlockSpec((bb, CONV_W - 1, 2 * D_FF), bat)]
    return pl.pallas_call(
        functools.partial(_ffn_kernel, bb=bb, tm=tm),
        out_shape=out_shape, grid=(B // bb, T // tm), in_specs=in_specs, out_specs=out_specs,
        scratch_shapes=[pltpu.VMEM((bb, CONV_W - 1, 2 * D_FF), F32),
                        pltpu.VMEM((D // LANES, bb * tm, LANES), F32)],
        compiler_params=pltpu.CompilerParams(dimension_semantics=("arbitrary", "arbitrary"),
                                             vmem_limit_bytes=VMEM_LIMIT),
        name="outproj_convffn",
    )(x, ya, ob, mod3, cstate, wts["onw"], wts["n2w"], wts["wo"], wts["wu"], wts["wc"], wts["bc"], wts["wd"])


def _prep_weights(norm1_w, w_in, b_ig, b_fg, ml_norm_w, sb_q_norm_w, sb_k_norm_w, sb_out_norm_w, w_out,
                  norm2_w, w_up, w_conv, b_conv, w_down):
    nm = 4 * ML_WIDTH
    ng = 2 * ML_HEADS
    w_g = w_in[:, nm:nm + ng]
    b_g = jnp.concatenate([b_ig, b_fg])
    return {
        "n1w": norm1_w.reshape(1, D_MODEL),
        "wm": w_in[:, :nm].astype(BF16),
        "wg": jnp.pad(w_g, ((0, 0), (0, LANES - ng))).astype(BF16),
        "wgt": jnp.pad(w_g.T, ((0, 16 - ng), (0, 0))).astype(BF16),
        "ws": w_in[:, nm + ng:].astype(BF16),
        "bgc": jnp.pad(b_g, (0, LANES - ng)).reshape(1, LANES),
        "bgr": jnp.pad(b_g, (0, 16 - ng)).reshape(16, 1),
        "qnw": jnp.tile(sb_q_norm_w, 2).reshape(1, LANES),
        "knw": jnp.tile(sb_k_norm_w, 2).reshape(1, LANES),
        "mnw": ml_norm_w.reshape(1, ML_WIDTH),
        "onw": sb_out_norm_w.reshape(1, SB_WIDTH),
        "n2w": norm2_w.reshape(1, D_MODEL),
        "wo": w_out.astype(BF16),
        "wu": w_up.astype(BF16),
        "wc": jnp.pad(w_conv, ((0, 8 - CONV_W), (0, 0))),
        "bc": b_conv.reshape(1, 2 * D_FF),
        "wd": w_down.astype(BF16),
    }


def _layer(x, mod3, wts, ml_state, k_past, v_past, conv_state, *, bb, tm, chunk, tq, tk):
    B, T, _ = x.shape
    nch = 2 * N_FF_CHUNKS
    qa, ka, va, og, gcol, grow, qb, kr, vr, k2, v2 = _inproj_call(x, mod3, wts, bb=bb, tm=tm,
                                                                  kblk=min(tq, tk))

    if ml_state is None:
        c0 = jnp.zeros((B, ML_HEADS, ML_DIM, 2 * ML_DIM), F32)
        n0 = jnp.zeros((B, 8, ML_WIDTH), F32)
        m0 = jnp.zeros((B, 1, LANES), F32)
    else:
        c_in, n_in, m_in = ml_state
        c0 = jnp.concatenate([c_in, jnp.broadcast_to(n_in[..., None], c_in.shape)], axis=-1)
        n0 = jnp.pad(jnp.broadcast_to(n_in.reshape(B, 1, ML_WIDTH), (B, ML_HEADS, ML_WIDTH)),
                     ((0, 0), (0, 8 - ML_HEADS), (0, 0)))
        m0 = jnp.pad(m_in, ((0, 0), (0, LANES - ML_HEADS))).reshape(B, 1, LANES)
    ya, c_new, n_new, m_new = _mlstm_call(qa, ka, va, og, gcol, grow, wts["mnw"], c0, n0, m0, L=chunk,
                                           cps=min(4, T // chunk))

    if k_past is None:
        ob = _sb_call(qb, k2, v2, k2, v2, tq=tq, tk=tk, self_causal=True)
    else:
        ob = _sb_call(qb, k2, v2, jnp.transpose(k_past, (0, 2, 3, 1)), jnp.transpose(v_past, (0, 2, 3, 1)),
                      tq=tq, tk=tk, self_causal=False)

    if conv_state is None:
        conv_state = jnp.zeros((B, CONV_W - 1, 2 * D_FF), F32)
    y, conv_new = _ffn_call(x, ya, ob, mod3, conv_state, wts, bb=bb, tm=tm)
    return (y, c_new, n_new, m_new[:, 0, :ML_HEADS],
            kr.reshape(B, T, SB_HEADS, SB_DIM), vr.reshape(B, T, SB_HEADS, SB_DIM), conv_new)


def kernel(x_prompt, x_sample, c_prompt, c_sample, state_mlstm_c, state_mlstm_n, state_mlstm_m, cache_sb_k, cache_sb_v, state_ffn_conv, w_ada, b_ada, norm1_w, w_in, b_ig, b_fg, ml_norm_w, sb_q_norm_w, sb_k_norm_w, sb_out_norm_w, w_out, norm2_w, w_up, w_conv, b_conv, w_down):
    depth = w_ada.shape[0]
    B = x_prompt.shape[0]
    y_p, y_s = x_prompt, x_sample
    outs_p = [[] for _ in range(6)]
    outs_s = [[] for _ in range(6)]
    for l in range(depth):
        mod = _ada_call(jnp.concatenate([c_prompt, c_sample], axis=0), w_ada[l], b_ada[l])
        mod3 = mod.reshape(mod.shape[0], 6, D_MODEL)
        wts = _prep_weights(norm1_w[l], w_in[l], b_ig[l], b_fg[l], ml_norm_w[l], sb_q_norm_w[l],
                            sb_k_norm_w[l], sb_out_norm_w[l], w_out[l], norm2_w[l], w_up[l], w_conv[l],
                            b_conv[l], w_down[l])
        res_p = _layer(y_p, mod3[:B], wts, None, None, None, None,
                       bb=1, tm=512, chunk=128, tq=256, tk=128)
        res_s = _layer(y_s, mod3[B:], wts, (state_mlstm_c[l], state_mlstm_n[l], state_mlstm_m[l]),
                       cache_sb_k[l], cache_sb_v[l], state_ffn_conv[l],
                       bb=x_sample.shape[0], tm=x_sample.shape[1], chunk=x_sample.shape[1],
                       tq=x_sample.shape[1], tk=128)
        y_p, y_s = res_p[0], res_s[0]
        for i in range(6):
            outs_p[i].append(res_p[i + 1])
            outs_s[i].append(res_s[i + 1])
    return (y_p, y_s, *[jnp.stack(o) for o in outs_p], *[jnp.stack(o) for o in outs_s])
```

```python
import functools

import numpy as np
import jax
import jax.numpy as jnp
from jax import lax
from jax.experimental import pallas as pl
from jax.experimental.pallas import tpu as pltpu

F32 = jnp.float32
BF16 = jnp.bfloat16

D_MODEL = 1024
ML_HEADS = 4
ML_DIM = 128
ML_WIDTH = ML_HEADS * ML_DIM
SB_HEADS = 8
SB_DIM = 64
SB_WIDTH = SB_HEADS * SB_DIM
D_FF = 2816
CONV_W = 3
EPS = 1e-6
LOG2E = 1.4426950408889634
SKIP_BITS = 160.0

LANES = 128
FF_CHUNK = 256
N_FF_CHUNKS = D_FF // FF_CHUNK
DOWN_GROUP = 4
VMEM_LIMIT = 52 * 1024 * 1024

NT_DIMS = (((1,), (1,)), ((), ()))
TN_DIMS = (((0,), (0,)), ((), ()))


def _vmem_spec():
    return pl.BlockSpec(memory_space=pltpu.VMEM)


def _rms(x, w):
    return x * lax.rsqrt(jnp.mean(x * x, axis=-1, keepdims=True) + EPS) * w


def _softplus(x):
    return jnp.maximum(x, 0.0) + jnp.log(1.0 + jnp.exp(-jnp.abs(x)))


def _split3(x):
    h1 = x.astype(BF16)
    r1 = x - h1.astype(F32)
    h2 = r1.astype(BF16)
    h3 = (r1 - h2.astype(F32)).astype(BF16)
    return h1, h2, h3


def _ada_kernel(c_ref, w_ref, b_ref, o_ref):
    c = c_ref[...]
    s = (c * jax.nn.sigmoid(c)).astype(BF16)
    o_ref[...] = jnp.dot(s, w_ref[...].astype(BF16), preferred_element_type=F32) + b_ref[...]


def _ada_call(c, w_ada, b_ada):
    n, d = c.shape
    nout = w_ada.shape[1]
    return pl.pallas_call(
        _ada_kernel,
        out_shape=jax.ShapeDtypeStruct((n, nout), F32),
        grid=(nout // d,),
        in_specs=[pl.BlockSpec((n, d), lambda j: (0, 0)),
                  pl.BlockSpec((d, d), lambda j: (0, j)),
                  pl.BlockSpec((1, d), lambda j: (0, j))],
        out_specs=pl.BlockSpec((n, d), lambda j: (0, j)),
        compiler_params=pltpu.CompilerParams(dimension_semantics=("arbitrary",)),
        name="adaln",
    )(c, w_ada, b_ada.reshape(1, nout))


def _headnorm64(x, w128):
    lane = lax.broadcasted_iota(jnp.int32, x.shape, 1)
    lo = lane < SB_DIM
    x2 = x * x
    s0 = jnp.sum(jnp.where(lo, x2, 0.0), axis=-1, keepdims=True)
    s1 = jnp.sum(jnp.where(lo, 0.0, x2), axis=-1, keepdims=True)
    r0 = lax.rsqrt(s0 * (1.0 / SB_DIM) + EPS)
    r1 = lax.rsqrt(s1 * (1.0 / SB_DIM) + EPS)
    return x * jnp.where(lo, r0, r1) * w128


def _inproj_kernel(x_ref, mod_ref, n1w_ref, wm_ref, wg_ref, wgt_ref, ws_ref, bgc_ref, bgr_ref,
                   qnw_ref, knw_ref,
                   qa_ref, ka_ref, va_ref, og_ref, gcol_ref, grow_ref, qb_ref, kr_ref, vr_ref,
                   k2_ref, v2_ref, *, bb, tm, kblk):
    hs = []
    for bi in range(bb):
        m = mod_ref[bi]
        y = _rms(x_ref[bi], n1w_ref[...])
        hs.append((y * (1.0 + m[1:2, :]) + m[0:1, :]).astype(BF16))
    h = hs[0] if bb == 1 else jnp.concatenate(hs, axis=0)

    def put(ref, val):
        for bi in range(bb):
            ref[bi] = val[bi * tm:(bi + 1) * tm].astype(ref.dtype)

    def proj(w_ref, j):
        return jnp.dot(h, w_ref[:, j * ML_WIDTH:(j + 1) * ML_WIDTH], preferred_element_type=F32)

    put(qa_ref, proj(wm_ref, 0))
    put(ka_ref, proj(wm_ref, 1) * (ML_DIM ** -0.5))
    put(va_ref, proj(wm_ref, 2))
    put(og_ref, jax.nn.sigmoid(proj(wm_ref, 3)))

    gc = jnp.dot(h, wg_ref[...], preferred_element_type=F32) + bgc_ref[...]
    lane = lax.broadcasted_iota(jnp.int32, gc.shape, 1)
    put(gcol_ref, jnp.where(lane >= ML_HEADS, -_softplus(-gc), gc))
    gr = lax.dot_general(wgt_ref[...], h, NT_DIMS, preferred_element_type=F32) + bgr_ref[...]
    row = lax.broadcasted_iota(jnp.int32, gr.shape, 0)
    gr = jnp.where(row >= ML_HEADS, -_softplus(-gr), gr)
    for bi in range(bb):
        grow_ref[bi] = gr[:2 * ML_HEADS, bi * tm:(bi + 1) * tm]

    sq = proj(ws_ref, 0)
    sk = proj(ws_ref, 1)
    sv = proj(ws_ref, 2)
    qn, kn = [], []
    for j in range(SB_WIDTH // LANES):
        sl = slice(j * LANES, (j + 1) * LANES)
        qn.append(_headnorm64(sq[:, sl], qnw_ref[...]) * (SB_DIM ** -0.5 * LOG2E))
        kn.append(_headnorm64(sk[:, sl], knw_ref[...]))
    qn = jnp.concatenate(qn, axis=1)
    kn = jnp.concatenate(kn, axis=1)
    put(qb_ref, qn)
    put(kr_ref, kn)
    put(vr_ref, sv)

    even = (lax.broadcasted_iota(jnp.int32, kn.shape, 1) & SB_DIM) == 0

    def put_pairs(ref, val):
        ve = jnp.where(even, val, 0.0).astype(BF16)
        vo = jnp.where(even, 0.0, val).astype(BF16)
        for bi in range(bb):
            parts = []
            for r0 in range(bi * tm, (bi + 1) * tm, kblk):
                parts += [ve[r0:r0 + kblk], vo[r0:r0 + kblk]]
            ref[bi] = jnp.concatenate(parts, axis=0)

    put_pairs(k2_ref, kn)
    put_pairs(v2_ref, sv)


def _inproj_call(x, mod3, wts, *, bb, tm, kblk):
    B, T, D = x.shape
    grid = (B // bb, T // tm)
    tok = lambda b, t: (b, t, 0)

    def tok_spec(w):
        return pl.BlockSpec((bb, tm, w), tok)

    def tok_shape(w, dt):
        return jax.ShapeDtypeStruct((B, T, w), dt)

    out_shape = [tok_shape(ML_WIDTH, BF16), tok_shape(ML_WIDTH, BF16), tok_shape(ML_WIDTH, BF16),
                 tok_shape(ML_WIDTH, F32), tok_shape(LANES, F32),
                 jax.ShapeDtypeStruct((B, 2 * ML_HEADS, T), F32),
                 tok_shape(SB_WIDTH, BF16), tok_shape(SB_WIDTH, F32), tok_shape(SB_WIDTH, F32),
                 jax.ShapeDtypeStruct((B, 2 * T, SB_WIDTH), BF16),
                 jax.ShapeDtypeStruct((B, 2 * T, SB_WIDTH), BF16)]
    pair_spec = pl.BlockSpec((bb, 2 * tm, SB_WIDTH), tok)
    out_specs = [tok_spec(ML_WIDTH), tok_spec(ML_WIDTH), tok_spec(ML_WIDTH), tok_spec(ML_WIDTH),
                 tok_spec(LANES), pl.BlockSpec((bb, 2 * ML_HEADS, tm), lambda b, t: (b, 0, t)),
                 tok_spec(SB_WIDTH), tok_spec(SB_WIDTH), tok_spec(SB_WIDTH), pair_spec, pair_spec]
    in_specs = [tok_spec(D), pl.BlockSpec((bb, 6, D), lambda b, t: (b, 0, 0))] + [_vmem_spec()] * 9
    return pl.pallas_call(
        functools.partial(_inproj_kernel, bb=bb, tm=tm, kblk=kblk),
        out_shape=out_shape, grid=grid, in_specs=in_specs, out_specs=out_specs,
        compiler_params=pltpu.CompilerParams(dimension_semantics=("arbitrary", "arbitrary"),
                                             vmem_limit_bytes=VMEM_LIMIT),
        name="inproj",
    )(x, mod3, wts["n1w"], wts["wm"], wts["wg"], wts["wgt"], wts["ws"], wts["bgc"], wts["bgr"],
      wts["qnw"], wts["knw"])


def _mlstm_kernel(q_ref, k_ref, v_ref, og_ref, gcol_ref, grow_ref, nw_ref, c0_ref, n0_ref, m0_ref,
                  tri_ref, trit_ref,
                  ya_ref, cout_ref, nout_ref, mout_ref,
                  c_scr, n_scr, m_scr, *, L, cps):
    @pl.when(pl.program_id(1) == 0)
    def _():
        c_scr[...] = c0_ref[0]
        n_scr[...] = n0_ref[0]
        m_scr[...] = m0_ref[0]

    heads = range(ML_HEADS)
    hss = [slice(h * ML_DIM, (h + 1) * ML_DIM) for h in heads]
    rowss = [slice(cc * L, (cc + 1) * L) for cc in range(cps)]
    r_i = lax.broadcasted_iota(jnp.int32, (L, L), 0)
    c_i = lax.broadcasted_iota(jnp.int32, (L, L), 1)
    causal = c_i <= r_i
    lane1 = lax.broadcasted_iota(jnp.int32, (1, LANES), 1)
    ones = jnp.ones((L, LANES), BF16)

    qks = []
    for rows in rowss:
        qks.append([lax.dot_general(q_ref[0, rows, hs], k_ref[0, rows, hs], NT_DIMS,
                                    preferred_element_type=F32) for hs in hss])
    gcols = [gcol_ref[0, rows, :] for rows in rowss]
    grows = [grow_ref[0, :, rows] for rows in rowss]
    col_parts = [p for g in gcols for p in _split3(g)]
    row_parts = [p for g in grows for p in _split3(jnp.concatenate([g, jnp.zeros_like(g)], axis=0))]
    bc_all = jnp.dot(tri_ref[...], jnp.concatenate(col_parts, axis=1), preferred_element_type=F32)
    br_all = jnp.dot(jnp.concatenate(row_parts, axis=0), trit_ref[...], preferred_element_type=F32)
    gates = []
    for cc in range(cps):
        bcol = sum(bc_all[:, (3 * cc + i) * LANES:(3 * cc + i + 1) * LANES] for i in range(3))
        brow = sum(br_all[(3 * cc + i) * 16:(3 * cc + i + 1) * 16, :] for i in range(3))
        gates.append((gcols[cc], grows[cc], bcol, brow))

    m_vec = m_scr[...]
    m_run = [m_vec[:, h:h + 1] for h in heads]
    plan = []

    def prepare(cc):
        rows = rowss[cc]
        gcol, grow, bcol, brow = gates[cc]
        per_head = []
        for h in heads:
            b_c = jnp.broadcast_to(bcol[:, ML_HEADS + h:ML_HEADS + h + 1], (L, LANES))
            ig_c = jnp.broadcast_to(gcol[:, h:h + 1], (L, LANES))
            b_r = brow[ML_HEADS + h:ML_HEADS + h + 1, :]
            ig_r = grow[h:h + 1, :]
            b_last = b_r[:, L - 1:L]
            g_r = ig_r - b_r
            m_h = m_run[h]
            dlog = jnp.where(causal, b_c[:, :L] + g_r, -jnp.inf)
            inter = b_c + m_h
            m_t = jnp.maximum(inter, jnp.max(dlog, axis=-1, keepdims=True))
            w_intra = jnp.exp(dlog - m_t[:, :L])
            wk_r = b_last + g_r
            m_new = jnp.maximum(b_last + m_h, jnp.max(wk_r, axis=-1, keepdims=True))
            wk_c = jnp.exp(b_last - b_c + ig_c - m_new)
            v = v_ref[0, rows, hss[h]]
            per_head.append({
                "s": (qks[cc][h] * w_intra).astype(BF16),
                "w_inter": jnp.exp(inter - m_t),
                "floor": jnp.exp(-m_t),
                "decay": jnp.exp(b_last + m_h - m_new),
                "vaug": jnp.concatenate([v, ones], axis=1),
                "wv": jnp.concatenate([(wk_c * v.astype(F32)).astype(BF16), wk_c.astype(BF16)], axis=1),
                "wk_row": jnp.exp(wk_r - m_new),
            })
            m_run[h] = m_new
        for h in heads:
            ph = per_head[h]
            ph["intra"] = jnp.dot(ph["s"], ph["vaug"], preferred_element_type=F32)
            ph["upd"] = lax.dot_general(k_ref[0, rows, hss[h]], ph["wv"], TN_DIMS, preferred_element_type=F32)
        wk16 = jnp.concatenate([per_head[h]["wk_row"] for h in heads]
                               + [jnp.zeros((16 - ML_HEADS, L), F32)], axis=0).astype(BF16)
        nupd = jnp.dot(wk16, k_ref[0, rows, :], preferred_element_type=F32)
        plan.append((per_head, nupd))

    def advance(cc):
        rows = rowss[cc]
        per_head, nupd = plan[cc]
        for h in heads:
            ph = per_head[h]
            hs = hss[h]
            cst = c_scr[h]
            qc = jnp.dot(q_ref[0, rows, hs], cst.astype(BF16), preferred_element_type=F32)
            num = ph["w_inter"] * qc[:, :ML_DIM] + ph["intra"][:, :ML_DIM]
            den = ph["w_inter"] * qc[:, ML_DIM:] + ph["intra"][:, ML_DIM:]
            hcell = num / jnp.maximum(jnp.abs(den), ph["floor"])
            ya_ref[0, rows, hs] = (_rms(hcell, nw_ref[:, hs]) * og_ref[0, rows, hs]).astype(BF16)
            c_scr[h] = ph["decay"] * cst + ph["upd"]
            n_scr[h:h + 1, hs] = ph["decay"] * n_scr[h:h + 1, hs] + nupd[h:h + 1, hs]

    for cc in range(cps):
        prepare(cc)
    for cc in range(cps):
        advance(cc)

    m_out = m_vec
    for h in heads:
        cout_ref[0, h] = c_scr[h][:, :ML_DIM]
        nout_ref[0, h:h + 1, :] = n_scr[h:h + 1, hss[h]]
        m_out = jnp.where(lane1 == h, m_run[h], m_out)
    m_scr[...] = m_out
    mout_ref[0] = m_out


def _mlstm_call(qa, ka, va, og, gcol, grow, nw, c0, n0, m0, *, L, cps):
    B, T, _ = qa.shape
    tri_np = np.tril(np.ones((L, L), np.float32))
    tri = jnp.asarray(tri_np, BF16)
    trit = jnp.asarray(tri_np.T, BF16)
    tok = lambda b, c: (b, c, 0)
    bat3 = lambda b, c: (b, 0, 0)
    bat4 = lambda b, c: (b, 0, 0, 0)
    tl = L * cps
    in_specs = [pl.BlockSpec((1, tl, ML_WIDTH), tok)] * 4 + [
        pl.BlockSpec((1, tl, LANES), tok),
        pl.BlockSpec((1, 2 * ML_HEADS, tl), lambda b, c: (b, 0, c)),
        _vmem_spec(),
        pl.BlockSpec((1, ML_HEADS, ML_DIM, 2 * ML_DIM), bat4),
        pl.BlockSpec((1, 8, ML_WIDTH), bat3),
        pl.BlockSpec((1, 1, LANES), bat3),
        _vmem_spec(), _vmem_spec()]
    out_shape = [jax.ShapeDtypeStruct((B, T, ML_WIDTH), BF16),
                 jax.ShapeDtypeStruct((B, ML_HEADS, ML_DIM, ML_DIM), F32),
                 jax.ShapeDtypeStruct((B, ML_HEADS, ML_DIM), F32),
                 jax.ShapeDtypeStruct((B, 1, LANES), F32)]
    out_specs = [pl.BlockSpec((1, tl, ML_WIDTH), tok),
                 pl.BlockSpec((1, ML_HEADS, ML_DIM, ML_DIM), bat4),
                 pl.BlockSpec((1, ML_HEADS, ML_DIM), bat3),
                 pl.BlockSpec((1, 1, LANES), bat3)]
    return pl.pallas_call(
        functools.partial(_mlstm_kernel, L=L, cps=cps),
        out_shape=out_shape, grid=(B, T // tl), in_specs=in_specs, out_specs=out_specs,
        scratch_shapes=[pltpu.VMEM((ML_HEADS, ML_DIM, 2 * ML_DIM), F32),
                        pltpu.VMEM((8, ML_WIDTH), F32),
                        pltpu.VMEM((1, LANES), F32)],
        compiler_params=pltpu.CompilerParams(dimension_semantics=("arbitrary", "arbitrary"),
                                             vmem_limit_bytes=VMEM_LIMIT),
        name="mlstm",
    )(qa, ka, va, og, gcol, grow, nw, c0, n0, m0, tri, trit)


def _suffix_matrix(n):
    u = np.zeros((2 * n, 2 * LANES), np.float32)
    j = np.arange(n)[:, None]
    s = np.arange(n)[None, :]
    u[:n, :n] = (j >= s)
    u[n:, :n] = (j >= s)
    u[:, LANES:] = 1.0
    return jnp.asarray(u, BF16)


def _sb_tiles(jobs, u_ref, n, transposed=False):
    fused = n % LANES == 0
    zs = []
    for q_p, k2, _, _ in jobs:
        if transposed:
            z2 = jnp.dot(q_p, k2, preferred_element_type=F32)
            zs.append([z2[:, :n], z2[:, n:]])
        elif fused:
            z2 = lax.dot_general(q_p, k2, NT_DIMS, preferred_element_type=F32)
            zs.append([z2[:, :n], z2[:, n:]])
        else:
            zs.append([lax.dot_general(q_p, k2[hh * n:(hh + 1) * n], NT_DIMS, preferred_element_type=F32)
                       for hh in range(2)])
    splits = []
    for (_, _, _, mask), zj in zip(jobs, zs):
        sj = []
        for z in zj:
            s = jnp.maximum(z, 0.0) + jnp.log2(1.0 + jnp.exp2(-jnp.abs(z)))
            if mask is not None:
                s = jnp.where(mask, s, 0.0)
            hi = s.astype(BF16)
            sj.append((hi, (s - hi.astype(F32)).astype(BF16)))
        splits.append(sj)
    weights = []
    for (_, _, _, mask), zj, sj in zip(jobs, zs, splits):
        wj = []
        for z, (hi, lo) in zip(zj, sj):
            if fused:
                res = jnp.dot(jnp.concatenate([hi, lo], axis=1), u_ref[...], preferred_element_type=F32)
            else:
                res = (jnp.dot(hi, u_ref[:n, :], preferred_element_type=F32)
                       + jnp.dot(lo, u_ref[n:, :], preferred_element_type=F32))
            a = jnp.exp2(z - res[:, :n])
            if mask is not None:
                a = jnp.where(mask, a, 0.0)
            wj.append((a.astype(BF16), res[:, LANES:]))
        weights.append(wj)
    outs = []
    for (q_p, _, v2, _), wj in zip(jobs, weights):
        if transposed:
            o = lax.dot_general(jnp.concatenate([wj[0][0], wj[1][0]], axis=1), v2, NT_DIMS,
                                preferred_element_type=F32)
        elif fused:
            o = jnp.dot(jnp.concatenate([wj[0][0], wj[1][0]], axis=1), v2, preferred_element_type=F32)
        else:
            o = (jnp.dot(wj[0][0], v2[:n], preferred_element_type=F32)
                 + jnp.dot(wj[1][0], v2[n:], preferred_element_type=F32))
        lane = lax.broadcasted_iota(jnp.int32, (q_p.shape[0], LANES), 1)
        outs.append((o, jnp.where(lane < SB_DIM, wj[0][1], wj[1][1])))
    return outs


def _sb_kernel(q_ref, kd_ref, vd_ref, kp_ref, vp_ref, ud_ref, up_ref, o_ref, acc_ref, used_ref,
               *, tq, tk, nd, kpi, ntrip_static, transposed_cache):
    npairs = SB_WIDTH // LANES
    pss = [slice(p * LANES, (p + 1) * LANES) for p in range(npairs)]
    acc_ref[...] = jnp.zeros_like(acc_ref)
    used_ref[...] = jnp.zeros_like(used_ref)

    def absorb(results, row0s):
        for idx, (o, r) in enumerate(results):
            ps = pss[idx % npairs]
            rs = slice(row0s[idx], tq)
            used = used_ref[rs, ps]
            acc_ref[rs, ps] = acc_ref[rs, ps] + jnp.exp2(-used) * o
            used_ref[rs, ps] = used + r

    jobs, row0s = [], []
    for d in reversed(range(tq // nd)):
        r_i = lax.broadcasted_iota(jnp.int32, (tq - d * nd, nd), 0)
        c_i = lax.broadcasted_iota(jnp.int32, (tq - d * nd, nd), 1)
        for ps in pss:
            jobs.append((q_ref[0, d * nd:, ps], kd_ref[0, 2 * d * nd:2 * (d + 1) * nd, ps],
                         vd_ref[0, 2 * d * nd:2 * (d + 1) * nd, ps], c_i < r_i))
            row0s.append(d * nd)
    absorb(_sb_tiles(jobs, ud_ref, nd), row0s)

    def cache_pair(ref, cols, p):
        even = ref[0, 2 * p, :, cols]
        odd = ref[0, 2 * p + 1, :, cols]
        zero = jnp.zeros_like(even)
        return jnp.concatenate([jnp.concatenate([even, zero], axis=1),
                                jnp.concatenate([zero, odd], axis=1)], axis=0).astype(BF16)

    def past_jobs(j):
        jobs = []
        for kk in reversed(range(kpi)):
            blk = j * kpi + kk
            if transposed_cache:
                cols = pl.ds(pl.multiple_of(blk * tk, tk), tk)
                for p, ps in enumerate(pss):
                    jobs.append((q_ref[0, :, ps], cache_pair(kp_ref, cols, p), cache_pair(vp_ref, cols, p), None))
            else:
                rows = pl.ds(pl.multiple_of(blk * 2 * tk, 2 * tk), 2 * tk)
                for ps in pss:
                    jobs.append((q_ref[0, :, ps], kp_ref[0, rows, ps], vp_ref[0, rows, ps], None))
        return jobs

    def cond(carry):
        j, least_used = carry
        return jnp.logical_and(j >= 0, least_used < SKIP_BITS)

    def body(carry):
        j, _ = carry
        absorb(_sb_tiles(past_jobs(j), up_ref, tk, transposed=transposed_cache), [0] * (kpi * npairs))
        return j - 1, jnp.min(used_ref[...])

    ntrip = pl.program_id(1) if ntrip_static is None else ntrip_static
    lax.while_loop(cond, body, (ntrip - 1, jnp.min(used_ref[...])))
    o_ref[0] = acc_ref[...]


def _sb_call(qb, kd, vd, kp, vp, *, tq, tk, self_causal):
    B, T, W = qb.shape
    nd = min(tq, tk)
    tok = lambda b, i: (b, i, 0)
    if self_causal:
        Tp = kp.shape[1] // 2
        kpi = tq // tk
        past_spec = pl.BlockSpec((1, 2 * Tp, W), lambda b, i: (b, 0, 0))
    else:
        Tp = kp.shape[3]
        kpi = 2
        past_spec = pl.BlockSpec((1,) + kp.shape[1:], lambda b, i: (b, 0, 0, 0))
    in_specs = [pl.BlockSpec((1, tq, W), tok), pl.BlockSpec((1, 2 * tq, W), tok), pl.BlockSpec((1, 2 * tq, W), tok),
                past_spec, past_spec, _vmem_spec(), _vmem_spec()]
    return pl.pallas_call(
        functools.partial(_sb_kernel, tq=tq, tk=tk, nd=nd, kpi=kpi, transposed_cache=not self_causal,
                          ntrip_static=None if self_causal else Tp // (tk * kpi)),
        out_shape=jax.ShapeDtypeStruct((B, T, W), F32),
        grid=(B, T // tq), in_specs=in_specs, out_specs=pl.BlockSpec((1, tq, W), tok),
        scratch_shapes=[pltpu.VMEM((tq, W), F32), pltpu.VMEM((tq, W), F32)],
        compiler_params=pltpu.CompilerParams(dimension_semantics=("arbitrary", "arbitrary"),
                                             vmem_limit_bytes=VMEM_LIMIT),
        name="stickbreak",
    )(qb, kd, vd, kp, vp, _suffix_matrix(nd), _suffix_matrix(tk))


def _ffn_kernel(x_ref, ya_ref, ob_ref, mod_ref, cst_ref, onw_ref, n2w_ref, wo_ref, wu_ref, wc_ref,
                bc_ref, wd_ref,
                y_ref, cnew_ref, tail_ref, ubuf_ref, *, bb, tm):
    @pl.when(pl.program_id(1) == 0)
    def _():
        tail_ref[...] = cst_ref[...]

    ycat = []
    for bi in range(bb):
        yb = _rms(ob_ref[bi], onw_ref[...]).astype(BF16)
        ycat.append(jnp.concatenate([ya_ref[bi], yb], axis=1))
    ycat = ycat[0] if bb == 1 else jnp.concatenate(ycat, axis=0)
    mix = jnp.dot(ycat, wo_ref[...], preferred_element_type=F32)

    x1s, h2s = [], []
    for bi in range(bb):
        m = mod_ref[bi]
        x1 = x_ref[bi] + m[2:3, :] * mix[bi * tm:(bi + 1) * tm]
        x1s.append(x1)
        h2s.append((_rms(x1, n2w_ref[...]) * (1.0 + m[4:5, :]) + m[3:4, :]).astype(BF16))
    h2 = h2s[0] if bb == 1 else jnp.concatenate(h2s, axis=0)

    def conv(col0, slot):
        cols = slice(col0, col0 + FF_CHUNK)
        u = jnp.dot(h2, wu_ref[:, cols], preferred_element_type=F32)
        w0, w1, w2 = wc_ref[0:1, cols], wc_ref[1:2, cols], wc_ref[2:3, cols]
        outs = []
        for bi in range(bb):
            ub = u[bi * tm:(bi + 1) * tm]
            base = bi * (tm + 8)
            ubuf_ref[slot, base + 6:base + 8, :] = tail_ref[bi, :, cols]
            ubuf_ref[slot, base + 8:base + 8 + tm, :] = ub
            u1 = ubuf_ref[slot, base + 7:base + 7 + tm, :]
            u2 = ubuf_ref[slot, base + 6:base + 6 + tm, :]
            outs.append(u2 * w0 + u1 * w1 + ub * w2 + bc_ref[:, cols])
            tail_ref[bi, :, cols] = ubuf_ref[slot, base + 6 + tm:base + 8 + tm, :]
        return outs[0] if bb == 1 else jnp.concatenate(outs, axis=0)

    def gated(c):
        ua = conv(c * FF_CHUNK, 2 * (c % 2))
        ug = conv(D_FF + c * FF_CHUNK, 2 * (c % 2) + 1)
        return (ua * jax.nn.sigmoid(ua) * ug).astype(BF16)

    acc = None
    pending, first = [gated(0)], 0
    for c in range(N_FF_CHUNKS):
        nxt = gated(c + 1) if c + 1 < N_FF_CHUNKS else None
        if len(pending) == DOWN_GROUP or nxt is None:
            lhs = pending[0] if len(pending) == 1 else jnp.concatenate(pending, axis=1)
            part = jnp.dot(lhs, wd_ref[first * FF_CHUNK:(c + 1) * FF_CHUNK, :], preferred_element_type=F32)
            acc = part if acc is None else acc + part
            pending, first = [], c + 1
        if nxt is not None:
            pending.append(nxt)

    for bi in range(bb):
        y_ref[bi] = x1s[bi] + mod_ref[bi][5:6, :] * acc[bi * tm:(bi + 1) * tm]

    @pl.when(pl.program_id(1) == pl.num_programs(1) - 1)
    def _():
        cnew_ref[...] = tail_ref[...]


def _ffn_call(x, ya, ob, mod3, cstate, wts, *, bb, tm):
    B, T, D = x.shape
    tok = lambda b, t: (b, t, 0)
    bat = lambda b, t: (b, 0, 0)
    in_specs = [pl.BlockSpec((bb, tm, D), tok), pl.BlockSpec((bb, tm, ML_WIDTH), tok),
                pl.BlockSpec((bb, tm, SB_WIDTH), tok), pl.BlockSpec((bb, 6, D), bat),
                pl.BlockSpec((bb, CONV_W - 1, 2 * D_FF), bat)] + [_vmem_spec()] * 7
    out_shape = [jax.ShapeDtypeStruct((B, T, D), F32), jax.ShapeDtypeStruct((B, CONV_W - 1, 2 * D_FF), F32)]
    out_specs = [pl.BlockSpec((bb, tm, D), tok), pl.BlockSpec((bb, CONV_W - 1, 2 * D_FF), bat)]
    return pl.pallas_call(
        functools.partial(_ffn_kernel, bb=bb, tm=tm),
        out_shape=out_shape, grid=(B // bb, T // tm), in_specs=in_specs, out_specs=out_specs,
        scratch_shapes=[pltpu.VMEM((bb, CONV_W - 1, 2 * D_FF), F32),
                        pltpu.VMEM((4, bb * (tm + 8), FF_CHUNK), F32)],
        compiler_params=pltpu.CompilerParams(dimension_semantics=("arbitrary", "arbitrary"),
                                             vmem_limit_bytes=VMEM_LIMIT),
        name="outproj_convffn",
    )(x, ya, ob, mod3, cstate, wts["onw"], wts["n2w"], wts["wo"], wts["wu"], wts["wc"], wts["bc"], wts["wd"])


def _prep_weights(norm1_w, w_in, b_ig, b_fg, ml_norm_w, sb_q_norm_w, sb_k_norm_w, sb_out_norm_w, w_out,
                  norm2_w, w_up, w_conv, b_conv, w_down):
    nm = 4 * ML_WIDTH
    ng = 2 * ML_HEADS
    w_g = w_in[:, nm:nm + ng]
    b_g = jnp.concatenate([b_ig, b_fg])
    return {
        "n1w": norm1_w.reshape(1, D_MODEL),
        "wm": w_in[:, :nm].astype(BF16),
        "wg": jnp.pad(w_g, ((0, 0), (0, LANES - ng))).astype(BF16),
        "wgt": jnp.pad(w_g.T, ((0, 16 - ng), (0, 0))).astype(BF16),
        "ws": w_in[:, nm + ng:].astype(BF16),
        "bgc": jnp.pad(b_g, (0, LANES - ng)).reshape(1, LANES),
        "bgr": jnp.pad(b_g, (0, 16 - ng)).reshape(16, 1),
        "qnw": jnp.tile(sb_q_norm_w, 2).reshape(1, LANES),
        "knw": jnp.tile(sb_k_norm_w, 2).reshape(1, LANES),
        "mnw": ml_norm_w.reshape(1, ML_WIDTH),
        "onw": sb_out_norm_w.reshape(1, SB_WIDTH),
        "n2w": norm2_w.reshape(1, D_MODEL),
        "wo": w_out.astype(BF16),
        "wu": w_up.astype(BF16),
        "wc": jnp.pad(w_conv, ((0, 8 - CONV_W), (0, 0))),
        "bc": b_conv.reshape(1, 2 * D_FF),
        "wd": w_down.astype(BF16),
    }


def _layer(x, mod3, wts, ml_state, k_past, v_past, conv_state, *, bb, tm, chunk, tq, tk):
    B, T, _ = x.shape
    nch = 2 * N_FF_CHUNKS
    qa, ka, va, og, gcol, grow, qb, kr, vr, k2, v2 = _inproj_call(x, mod3, wts, bb=bb, tm=min(T, 2 * tm),
                                                                  kblk=min(tq, tk))

    if ml_state is None:
        c0 = jnp.zeros((B, ML_HEADS, ML_DIM, 2 * ML_DIM), F32)
        n0 = jnp.zeros((B, 8, ML_WIDTH), F32)
        m0 = jnp.zeros((B, 1, LANES), F32)
    else:
        c_in, n_in, m_in = ml_state
        c0 = jnp.concatenate([c_in, jnp.broadcast_to(n_in[..., None], c_in.shape)], axis=-1)
        n0 = jnp.pad(jnp.broadcast_to(n_in.reshape(B, 1, ML_WIDTH), (B, ML_HEADS, ML_WIDTH)),
                     ((0, 0), (0, 8 - ML_HEADS), (0, 0)))
        m0 = jnp.pad(m_in, ((0, 0), (0, LANES - ML_HEADS))).reshape(B, 1, LANES)
    ya, c_new, n_new, m_new = _mlstm_call(qa, ka, va, og, gcol, grow, wts["mnw"], c0, n0, m0, L=chunk,
                                           cps=min(8, T // chunk))

    if k_past is None:
        ob = _sb_call(qb, k2, v2, k2, v2, tq=tq, tk=tk, self_causal=True)
    else:
        ob = _sb_call(qb, k2, v2, jnp.transpose(k_past, (0, 2, 3, 1)), jnp.transpose(v_past, (0, 2, 3, 1)),
                      tq=tq, tk=tk, self_causal=False)

    if conv_state is None:
        conv_state = jnp.zeros((B, CONV_W - 1, 2 * D_FF), F32)
    y, conv_new = _ffn_call(x, ya, ob, mod3, conv_state, wts, bb=bb, tm=tm)
    return (y, c_new, n_new, m_new[:, 0, :ML_HEADS],
            kr.reshape(B, T, SB_HEADS, SB_DIM), vr.reshape(B, T, SB_HEADS, SB_DIM), conv_new)


def kernel(x_prompt, x_sample, c_prompt, c_sample, state_mlstm_c, state_mlstm_n, state_mlstm_m, cache_sb_k, cache_sb_v, state_ffn_conv, w_ada, b_ada, norm1_w, w_in, b_ig, b_fg, ml_norm_w, sb_q_norm_w, sb_k_norm_w, sb_out_norm_w, w_out, norm2_w, w_up, w_conv, b_conv, w_down):
    depth = w_ada.shape[0]
    B = x_prompt.shape[0]
    y_p, y_s = x_prompt, x_sample
    outs_p = [[] for _ in range(6)]
    outs_s = [[] for _ in range(6)]
    for l in range(depth):
        mod = _ada_call(jnp.concatenate([c_prompt, c_sample], axis=0), w_ada[l], b_ada[l])
        mod3 = mod.reshape(mod.shape[0], 6, D_MODEL)
        wts = _prep_weights(norm1_w[l], w_in[l], b_ig[l], b_fg[l], ml_norm_w[l], sb_q_norm_w[l],
                            sb_k_norm_w[l], sb_out_norm_w[l], w_out[l], norm2_w[l], w_up[l], w_conv[l],
                            b_conv[l], w_down[l])
        res_p = _layer(y_p, mod3[:B], wts, None, None, None, None,
                       bb=1, tm=512, chunk=128, tq=256, tk=128)
        res_s = _layer(y_s, mod3[B:], wts, (state_mlstm_c[l], state_mlstm_n[l], state_mlstm_m[l]),
                       cache_sb_k[l], cache_sb_v[l], state_ffn_conv[l],
                       bb=x_sample.shape[0], tm=x_sample.shape[1], chunk=x_sample.shape[1],
                       tq=x_sample.shape[1], tk=128)
        y_p, y_s = res_p[0], res_s[0]
        for i in range(6):
            outs_p[i].append(res_p[i + 1])
            outs_s[i].append(res_s[i + 1])
    return (y_p, y_s, *[jnp.stack(o) for o in outs_p], *[jnp.stack(o) for o in outs_s])
```

```python
import functools

import numpy as np
import jax
import jax.numpy as jnp
from jax import lax
from jax.experimental import pallas as pl
from jax.experimental.pallas import tpu as pltpu

F32 = jnp.float32
BF16 = jnp.bfloat16

D_MODEL = 1024
ML_HEADS = 4
ML_DIM = 128
ML_WIDTH = ML_HEADS * ML_DIM
SB_HEADS = 8
SB_DIM = 64
SB_WIDTH = SB_HEADS * SB_DIM
D_FF = 2816
CONV_W = 3
EPS = 1e-6
LOG2E = 1.4426950408889634
SKIP_BITS = 160.0

LANES = 128
FF_CHUNK = 256
N_FF_CHUNKS = D_FF // FF_CHUNK
DOWN_GROUP = 4
VMEM_LIMIT = 52 * 1024 * 1024
FUSED_ORDER = "cppccccppcccc"

NT_DIMS = (((1,), (1,)), ((), ()))
TN_DIMS = (((0,), (0,)), ((), ()))


def _vmem_spec():
    return pl.BlockSpec(memory_space=pltpu.VMEM)


def _rms(x, w):
    return x * lax.rsqrt(jnp.mean(x * x, axis=-1, keepdims=True) + EPS) * w


def _softplus(x):
    return jnp.maximum(x, 0.0) + jnp.log(1.0 + jnp.exp(-jnp.abs(x)))


def _split3(x):
    h1 = x.astype(BF16)
    r1 = x - h1.astype(F32)
    h2 = r1.astype(BF16)
    h3 = (r1 - h2.astype(F32)).astype(BF16)
    return h1, h2, h3


def _ada_kernel(c_ref, w_ref, b_ref, o_ref):
    c = c_ref[...]
    s = (c * jax.nn.sigmoid(c)).astype(BF16)
    o_ref[...] = jnp.dot(s, w_ref[...].astype(BF16), preferred_element_type=F32) + b_ref[...]


def _ada_call(c, w_ada, b_ada):
    n, d = c.shape
    nout = w_ada.shape[1]
    return pl.pallas_call(
        _ada_kernel,
        out_shape=jax.ShapeDtypeStruct((n, nout), F32),
        grid=(nout // d,),
        in_specs=[pl.BlockSpec((n, d), lambda j: (0, 0)),
                  pl.BlockSpec((d, d), lambda j: (0, j)),
                  pl.BlockSpec((1, d), lambda j: (0, j))],
        out_specs=pl.BlockSpec((n, d), lambda j: (0, j)),
        compiler_params=pltpu.CompilerParams(dimension_semantics=("arbitrary",)),
        name="adaln",
    )(c, w_ada, b_ada.reshape(1, nout))


def _headnorm64(x, w128):
    lane = lax.broadcasted_iota(jnp.int32, x.shape, 1)
    lo = lane < SB_DIM
    x2 = x * x
    s0 = jnp.sum(jnp.where(lo, x2, 0.0), axis=-1, keepdims=True)
    s1 = jnp.sum(jnp.where(lo, 0.0, x2), axis=-1, keepdims=True)
    r0 = lax.rsqrt(s0 * (1.0 / SB_DIM) + EPS)
    r1 = lax.rsqrt(s1 * (1.0 / SB_DIM) + EPS)
    return x * jnp.where(lo, r0, r1) * w128


def _inproj_phases(x_ref, mod_ref, n1w_ref, wm_ref, wg_ref, wgt_ref, ws_ref, bgc_ref, bgr_ref,
                   qnw_ref, knw_ref,
                   qa_ref, ka_ref, va_ref, og_ref, gcol_ref, grow_ref, qb_ref, kr_ref, vr_ref,
                   k2_ref, v2_ref, *, bb, tm, kblk):
    hs = []
    for bi in range(bb):
        m = mod_ref[bi]
        y = _rms(x_ref[bi], n1w_ref[...])
        hs.append((y * (1.0 + m[1:2, :]) + m[0:1, :]).astype(BF16))
    h = hs[0] if bb == 1 else jnp.concatenate(hs, axis=0)

    def put(ref, val):
        for bi in range(bb):
            ref[bi] = val[bi * tm:(bi + 1) * tm].astype(ref.dtype)

    def proj(w_ref, j):
        return jnp.dot(h, w_ref[:, j * ML_WIDTH:(j + 1) * ML_WIDTH], preferred_element_type=F32)

    put(qa_ref, proj(wm_ref, 0))
    put(ka_ref, proj(wm_ref, 1) * (ML_DIM ** -0.5))
    yield
    put(va_ref, proj(wm_ref, 2))
    put(og_ref, jax.nn.sigmoid(proj(wm_ref, 3)))
    yield

    gc = jnp.dot(h, wg_ref[...], preferred_element_type=F32) + bgc_ref[...]
    lane = lax.broadcasted_iota(jnp.int32, gc.shape, 1)
    put(gcol_ref, jnp.where(lane >= ML_HEADS, -_softplus(-gc), gc))
    gr = lax.dot_general(wgt_ref[...], h, NT_DIMS, preferred_element_type=F32) + bgr_ref[...]
    row = lax.broadcasted_iota(jnp.int32, gr.shape, 0)
    gr = jnp.where(row >= ML_HEADS, -_softplus(-gr), gr)
    for bi in range(bb):
        grow_ref[bi] = gr[:2 * ML_HEADS, bi * tm:(bi + 1) * tm]

    sq = proj(ws_ref, 0)
    sk = proj(ws_ref, 1)
    sv = proj(ws_ref, 2)
    yield
    qn, kn = [], []
    for j in range(SB_WIDTH // LANES):
        sl = slice(j * LANES, (j + 1) * LANES)
        qn.append(_headnorm64(sq[:, sl], qnw_ref[...]) * (SB_DIM ** -0.5 * LOG2E))
        kn.append(_headnorm64(sk[:, sl], knw_ref[...]))
    qn = jnp.concatenate(qn, axis=1)
    kn = jnp.concatenate(kn, axis=1)
    put(qb_ref, qn)
    put(kr_ref, kn)
    put(vr_ref, sv)

    even = (lax.broadcasted_iota(jnp.int32, kn.shape, 1) & SB_DIM) == 0

    def put_pairs(ref, val):
        ve = jnp.where(even, val, 0.0).astype(BF16)
        vo = jnp.where(even, 0.0, val).astype(BF16)
        for bi in range(bb):
            parts = []
            for r0 in range(bi * tm, (bi + 1) * tm, kblk):
                parts += [ve[r0:r0 + kblk], vo[r0:r0 + kblk]]
            ref[bi] = jnp.concatenate(parts, axis=0)

    put_pairs(k2_ref, kn)
    put_pairs(v2_ref, sv)
    yield


def _inproj_kernel(*refs, bb, tm, kblk):
    for _ in _inproj_phases(*refs, bb=bb, tm=tm, kblk=kblk):
        pass


def _inproj_call(x, mod3, wts, *, bb, tm, kblk):
    B, T, D = x.shape
    grid = (B // bb, T // tm)
    tok = lambda b, t: (b, t, 0)

    def tok_spec(w):
        return pl.BlockSpec((bb, tm, w), tok)

    def tok_shape(w, dt):
        return jax.ShapeDtypeStruct((B, T, w), dt)

    out_shape = [tok_shape(ML_WIDTH, BF16), tok_shape(ML_WIDTH, BF16), tok_shape(ML_WIDTH, BF16),
                 tok_shape(ML_WIDTH, F32), tok_shape(LANES, F32),
                 jax.ShapeDtypeStruct((B, 2 * ML_HEADS, T), F32),
                 tok_shape(SB_WIDTH, BF16), tok_shape(SB_WIDTH, F32), tok_shape(SB_WIDTH, F32),
                 jax.ShapeDtypeStruct((B, 2 * T, SB_WIDTH), BF16),
                 jax.ShapeDtypeStruct((B, 2 * T, SB_WIDTH), BF16)]
    pair_spec = pl.BlockSpec((bb, 2 * tm, SB_WIDTH), tok)
    out_specs = [tok_spec(ML_WIDTH), tok_spec(ML_WIDTH), tok_spec(ML_WIDTH), tok_spec(ML_WIDTH),
                 tok_spec(LANES), pl.BlockSpec((bb, 2 * ML_HEADS, tm), lambda b, t: (b, 0, t)),
                 tok_spec(SB_WIDTH), tok_spec(SB_WIDTH), tok_spec(SB_WIDTH), pair_spec, pair_spec]
    in_specs = [tok_spec(D), pl.BlockSpec((bb, 6, D), lambda b, t: (b, 0, 0))] + [_vmem_spec()] * 9
    return pl.pallas_call(
        functools.partial(_inproj_kernel, bb=bb, tm=tm, kblk=kblk),
        out_shape=out_shape, grid=grid, in_specs=in_specs, out_specs=out_specs,
        compiler_params=pltpu.CompilerParams(dimension_semantics=("arbitrary", "arbitrary"),
                                             vmem_limit_bytes=VMEM_LIMIT),
        name="inproj",
    )(x, mod3, wts["n1w"], wts["wm"], wts["wg"], wts["wgt"], wts["ws"], wts["bgc"], wts["bgr"],
      wts["qnw"], wts["knw"])


def _mlstm_kernel(q_ref, k_ref, v_ref, og_ref, gcol_ref, grow_ref, nw_ref, c0_ref, m0_ref,
                  tri_ref, trit_ref,
                  ya_ref, cout_ref, nout_ref, mout_ref,
                  c_scr, m_scr, *, L, cps):
    @pl.when(pl.program_id(1) == 0)
    def _():
        c_scr[...] = c0_ref[0]
        m_scr[...] = m0_ref[0]

    for _ in _mlstm_phases(q_ref, k_ref, v_ref, og_ref, gcol_ref, grow_ref, nw_ref, tri_ref, trit_ref,
                           ya_ref, cout_ref, nout_ref, mout_ref, c_scr, m_scr, L=L, cps=cps):
        pass


def _mlstm_phases(q_ref, k_ref, v_ref, og_ref, gcol_ref, grow_ref, nw_ref, tri_ref, trit_ref,
                  ya_ref, cout_ref, nout_ref, mout_ref, c_scr, m_scr, *, L, cps):
    heads = range(ML_HEADS)
    hss = [slice(h * ML_DIM, (h + 1) * ML_DIM) for h in heads]
    rowss = [slice(cc * L, (cc + 1) * L) for cc in range(cps)]
    r_i = lax.broadcasted_iota(jnp.int32, (L, L), 0)
    c_i = lax.broadcasted_iota(jnp.int32, (L, L), 1)
    causal = c_i <= r_i
    lane1 = lax.broadcasted_iota(jnp.int32, (1, LANES), 1)
    ones = jnp.ones((L, LANES), BF16)

    qks = []
    for rows in rowss:
        qks.append([lax.dot_general(q_ref[0, rows, hs], k_ref[0, rows, hs], NT_DIMS,
                                    preferred_element_type=F32) for hs in hss])
    gcols = [gcol_ref[0, rows, :] for rows in rowss]
    grows = [grow_ref[0, :, rows] for rows in rowss]
    col_parts = [p for g in gcols for p in _split3(g)]
    row_parts = [p for g in grows for p in _split3(jnp.concatenate([g, jnp.zeros_like(g)], axis=0))]
    bc_all = jnp.dot(tri_ref[...], jnp.concatenate(col_parts, axis=1), preferred_element_type=F32)
    br_all = jnp.dot(jnp.concatenate(row_parts, axis=0), trit_ref[...], preferred_element_type=F32)
    gates = []
    for cc in range(cps):
        bcol = sum(bc_all[:, (3 * cc + i) * LANES:(3 * cc + i + 1) * LANES] for i in range(3))
        brow = sum(br_all[(3 * cc + i) * 16:(3 * cc + i + 1) * 16, :] for i in range(3))
        gates.append((gcols[cc], grows[cc], bcol, brow))
    yield

    m_vec = m_scr[...]
    m_run = [m_vec[:, h:h + 1] for h in heads]
    plan = []

    def prepare(cc):
        rows = rowss[cc]
        gcol, grow, bcol, brow = gates[cc]
        per_head = []
        for h in heads:
            b_c = jnp.broadcast_to(bcol[:, ML_HEADS + h:ML_HEADS + h + 1], (L, LANES))
            ig_c = jnp.broadcast_to(gcol[:, h:h + 1], (L, LANES))
            b_r = brow[ML_HEADS + h:ML_HEADS + h + 1, :]
            ig_r = grow[h:h + 1, :]
            b_last = b_r[:, L - 1:L]
            g_r = ig_r - b_r
            m_h = m_run[h]
            dlog = jnp.where(causal, b_c[:, :L] + g_r, -jnp.inf)
            inter = b_c + m_h
            m_t = jnp.maximum(inter, jnp.max(dlog, axis=-1, keepdims=True))
            w_intra = jnp.exp(dlog - m_t[:, :L])
            wk_r = b_last + g_r
            m_new = jnp.maximum(b_last + m_h, jnp.max(wk_r, axis=-1, keepdims=True))
            wk_c = jnp.exp(b_last - b_c + ig_c - m_new)
            v = v_ref[0, rows, hss[h]]
            per_head.append({
                "s": (qks[cc][h] * w_intra).astype(BF16),
                "w_inter": jnp.exp(inter - m_t),
                "floor": jnp.exp(-m_t),
                "decay": jnp.exp(b_last + m_h - m_new),
                "vaug": jnp.concatenate([v, ones], axis=1),
                "wv": jnp.concatenate([(wk_c * v.astype(F32)).astype(BF16), wk_c.astype(BF16)], axis=1),
            })
            m_run[h] = m_new
        for h in heads:
            ph = per_head[h]
            ph["intra"] = jnp.dot(ph["s"], ph["vaug"], preferred_element_type=F32)
            ph["upd"] = lax.dot_general(k_ref[0, rows, hss[h]], ph["wv"], TN_DIMS, preferred_element_type=F32)
        plan.append(per_head)

    def advance(cc):
        rows = rowss[cc]
        per_head = plan[cc]
        for h in heads:
            ph = per_head[h]
            hs = hss[h]
            cst = c_scr[h]
            qc = jnp.dot(q_ref[0, rows, hs], cst.astype(BF16), preferred_element_type=F32)
            num = ph["w_inter"] * qc[:, :ML_DIM] + ph["intra"][:, :ML_DIM]
            den = ph["w_inter"] * qc[:, ML_DIM:] + ph["intra"][:, ML_DIM:]
            hcell = num / jnp.maximum(jnp.abs(den), ph["floor"])
            ya_ref[0, rows, hs] = (_rms(hcell, nw_ref[:, hs]) * og_ref[0, rows, hs]).astype(BF16)
            c_scr[h] = ph["decay"] * cst + ph["upd"]

    for cc in range(cps):
        prepare(cc)
        yield
    for cc in range(cps):
        advance(cc)
        yield

    m_out = m_vec
    mean_row = jnp.full((8, ML_DIM), 1.0 / ML_DIM, BF16)
    for h in heads:
        cst = c_scr[h]
        cout_ref[0, h] = cst[:, :ML_DIM]
        n_rows = sum(lax.dot_general(mean_row, p, NT_DIMS, preferred_element_type=F32)
                     for p in _split3(cst[:, ML_DIM:]))
        nout_ref[0, h:h + 1, :] = n_rows[0:1, :]
        m_out = jnp.where(lane1 == h, m_run[h], m_out)
    m_scr[...] = m_out
    mout_ref[0] = m_out


def _mlstm_call(qa, ka, va, og, gcol, grow, nw, c0, m0, *, L, cps):
    B, T, _ = qa.shape
    tri_np = np.tril(np.ones((L, L), np.float32))
    tri = jnp.asarray(tri_np, BF16)
    trit = jnp.asarray(tri_np.T, BF16)
    tok = lambda b, c: (b, c, 0)
    bat3 = lambda b, c: (b, 0, 0)
    bat4 = lambda b, c: (b, 0, 0, 0)
    tl = L * cps
    in_specs = [pl.BlockSpec((1, tl, ML_WIDTH), tok)] * 4 + [
        pl.BlockSpec((1, tl, LANES), tok),
        pl.BlockSpec((1, 2 * ML_HEADS, tl), lambda b, c: (b, 0, c)),
        _vmem_spec(),
        pl.BlockSpec((1, ML_HEADS, ML_DIM, 2 * ML_DIM), bat4),
        pl.BlockSpec((1, 1, LANES), bat3),
        _vmem_spec(), _vmem_spec()]
    out_shape = [jax.ShapeDtypeStruct((B, T, ML_WIDTH), BF16),
                 jax.ShapeDtypeStruct((B, ML_HEADS, ML_DIM, ML_DIM), F32),
                 jax.ShapeDtypeStruct((B, ML_HEADS, ML_DIM), F32),
                 jax.ShapeDtypeStruct((B, 1, LANES), F32)]
    out_specs = [pl.BlockSpec((1, tl, ML_WIDTH), tok),
                 pl.BlockSpec((1, ML_HEADS, ML_DIM, ML_DIM), bat4),
                 pl.BlockSpec((1, ML_HEADS, ML_DIM), bat3),
                 pl.BlockSpec((1, 1, LANES), bat3)]
    return pl.pallas_call(
        functools.partial(_mlstm_kernel, L=L, cps=cps),
        out_shape=out_shape, grid=(B, T // tl), in_specs=in_specs, out_specs=out_specs,
        scratch_shapes=[pltpu.VMEM((ML_HEADS, ML_DIM, 2 * ML_DIM), F32),
                        pltpu.VMEM((1, LANES), F32)],
        compiler_params=pltpu.CompilerParams(dimension_semantics=("arbitrary", "arbitrary"),
                                             vmem_limit_bytes=VMEM_LIMIT),
        name="mlstm",
    )(qa, ka, va, og, gcol, grow, nw, c0, m0, tri, trit)


def _inproj_mlstm_kernel(x_ref, mod_ref, n1w_ref, wm_ref, wg_ref, wgt_ref, ws_ref, bgc_ref, bgr_ref,
                         qnw_ref, knw_ref, nw_ref, c0_ref, m0_ref, tri_ref, trit_ref,
                         qb_ref, kr_ref, vr_ref, k2_ref, v2_ref, ya_ref, cout_ref, nout_ref, mout_ref,
                         cur_q, cur_k, cur_v, cur_og, cur_gc, cur_gr,
                         prv_q, prv_k, prv_v, prv_og, prv_gc, prv_gr, c_scr, m_scr,
                         *, tm, kblk, L, cps, nt):
    s = pl.program_id(0)
    cur = (cur_q, cur_k, cur_v, cur_og, cur_gc, cur_gr)
    prv = (prv_q, prv_k, prv_v, prv_og, prv_gc, prv_gr)

    @pl.when(s == 0)
    def _():
        for ref in prv + (c_scr, m_scr):
            ref[...] = jnp.zeros_like(ref)

    @pl.when(jnp.logical_and(s > 0, lax.rem(jnp.maximum(s - 1, 0), nt) == 0))
    def _():
        c_scr[...] = c0_ref[0]
        m_scr[...] = m0_ref[0]

    proj = _inproj_phases(x_ref, mod_ref, n1w_ref, wm_ref, wg_ref, wgt_ref, ws_ref, bgc_ref, bgr_ref,
                          qnw_ref, knw_ref, *cur, qb_ref, kr_ref, vr_ref, k2_ref, v2_ref,
                          bb=1, tm=tm, kblk=kblk)
    cell = _mlstm_phases(*prv, nw_ref, tri_ref, trit_ref, ya_ref, cout_ref, nout_ref, mout_ref,
                         c_scr, m_scr, L=L, cps=cps)
    for turn in FUSED_ORDER:
        next(cell if turn == "c" else proj, None)
    for gen in (cell, proj):
        for _ in gen:
            pass
    for c_ref, p_ref in zip(cur, prv):
        p_ref[...] = c_ref[...]


def _inproj_mlstm_call(x, mod3, wts, c0, m0, *, tm, kblk, L, cps):
    B, T, D = x.shape
    nt = T // tm
    n_tiles = B * nt
    tl = L * cps
    assert tl == tm
    tri_np = np.tril(np.ones((L, L), np.float32))

    def cur_tile(s):
        t = jnp.minimum(s, n_tiles - 1)
        return t // nt, t % nt

    def prev_tile(s):
        t = jnp.maximum(s - 1, 0)
        return t // nt, t % nt

    tok = lambda s: (*cur_tile(s), 0)
    tok_prev = lambda s: (*prev_tile(s), 0)
    bat3 = lambda s: (prev_tile(s)[0], 0, 0)
    bat4 = lambda s: (prev_tile(s)[0], 0, 0, 0)
    in_specs = ([pl.BlockSpec((1, tm, D), tok), pl.BlockSpec((1, 6, D), lambda s: (cur_tile(s)[0], 0, 0))]
                + [_vmem_spec()] * 10
                + [pl.BlockSpec((1, ML_HEADS, ML_DIM, 2 * ML_DIM), bat4),
                   pl.BlockSpec((1, 1, LANES), bat3), _vmem_spec(), _vmem_spec()])
    sb_shape = jax.ShapeDtypeStruct((B, T, SB_WIDTH), F32)
    pair_shape = jax.ShapeDtypeStruct((B, 2 * T, SB_WIDTH), BF16)
    out_shape = [jax.ShapeDtypeStruct((B, T, SB_WIDTH), BF16), sb_shape, sb_shape, pair_shape, pair_shape,
                 jax.ShapeDtypeStruct((B, T, ML_WIDTH), BF16),
                 jax.ShapeDtypeStruct((B, ML_HEADS, ML_DIM, ML_DIM), F32),
                 jax.ShapeDtypeStruct((B, ML_HEADS, ML_DIM), F32),
                 jax.ShapeDtypeStruct((B, 1, LANES), F32)]
    sb_spec = pl.BlockSpec((1, tm, SB_WIDTH), tok)
    pair_spec = pl.BlockSpec((1, 2 * tm, SB_WIDTH), tok)
    out_specs = [sb_spec, sb_spec, sb_spec, pair_spec, pair_spec,
                 pl.BlockSpec((1, tm, ML_WIDTH), tok_prev),
                 pl.BlockSpec((1, ML_HEADS, ML_DIM, ML_DIM), bat4),
                 pl.BlockSpec((1, ML_HEADS, ML_DIM), bat3),
                 pl.BlockSpec((1, 1, LANES), bat3)]
    operands = [pltpu.VMEM((1, tm, ML_WIDTH), BF16)] * 3 + [pltpu.VMEM((1, tm, ML_WIDTH), F32),
                                                           pltpu.VMEM((1, tm, LANES), F32),
                                                           pltpu.VMEM((1, 2 * ML_HEADS, tm), F32)]
    return pl.pallas_call(
        functools.partial(_inproj_mlstm_kernel, tm=tm, kblk=kblk, L=L, cps=cps, nt=nt),
        out_shape=out_shape, grid=(n_tiles + 1,), in_specs=in_specs, out_specs=out_specs,
        scratch_shapes=operands + operands + [pltpu.VMEM((ML_HEADS, ML_DIM, 2 * ML_DIM), F32),
                                              pltpu.VMEM((1, LANES), F32)],
        compiler_params=pltpu.CompilerParams(dimension_semantics=("arbitrary",),
                                             vmem_limit_bytes=VMEM_LIMIT),
        name="inproj_mlstm",
    )(x, mod3, wts["n1w"], wts["wm"], wts["wg"], wts["wgt"], wts["ws"], wts["bgc"], wts["bgr"],
      wts["qnw"], wts["knw"], wts["mnw"], c0, m0, jnp.asarray(tri_np, BF16), jnp.asarray(tri_np.T, BF16))


def _suffix_matrix(n):
    u = np.zeros((2 * n, 2 * LANES), np.float32)
    j = np.arange(n)[:, None]
    s = np.arange(n)[None, :]
    u[:n, :n] = (j >= s)
    u[n:, :n] = (j >= s)
    u[:, LANES:] = 1.0
    return jnp.asarray(u, BF16)


def _sb_tiles(jobs, u_ref, n, transposed=False):
    fused = n % LANES == 0
    zs = []
    for q_p, k2, _, _ in jobs:
        if transposed:
            z2 = jnp.dot(q_p, k2, preferred_element_type=F32)
            zs.append([z2[:, :n], z2[:, n:]])
        elif fused:
            z2 = lax.dot_general(q_p, k2, NT_DIMS, preferred_element_type=F32)
            zs.append([z2[:, :n], z2[:, n:]])
        else:
            zs.append([lax.dot_general(q_p, k2[hh * n:(hh + 1) * n], NT_DIMS, preferred_element_type=F32)
                       for hh in range(2)])
    splits = []
    for (_, _, _, mask), zj in zip(jobs, zs):
        sj = []
        for z in zj:
            s = jnp.maximum(z, 0.0) + jnp.log2(1.0 + jnp.exp2(-jnp.abs(z)))
            if mask is not None:
                s = jnp.where(mask, s, 0.0)
            hi = s.astype(BF16)
            sj.append((hi, (s - hi.astype(F32)).astype(BF16)))
        splits.append(sj)
    weights = []
    for (_, _, _, mask), zj, sj in zip(jobs, zs, splits):
        wj = []
        for z, (hi, lo) in zip(zj, sj):
            if fused:
                res = jnp.dot(jnp.concatenate([hi, lo], axis=1), u_ref[...], preferred_element_type=F32)
            else:
                res = (jnp.dot(hi, u_ref[:n, :], preferred_element_type=F32)
                       + jnp.dot(lo, u_ref[n:, :], preferred_element_type=F32))
            a = jnp.exp2(z - res[:, :n])
            if mask is not None:
                a = jnp.where(mask, a, 0.0)
            wj.append((a.astype(BF16), res[:, LANES:]))
        weights.append(wj)
    outs = []
    for (q_p, _, v2, _), wj in zip(jobs, weights):
        if transposed:
            o = lax.dot_general(jnp.concatenate([wj[0][0], wj[1][0]], axis=1), v2, NT_DIMS,
                                preferred_element_type=F32)
        elif fused:
            o = jnp.dot(jnp.concatenate([wj[0][0], wj[1][0]], axis=1), v2, preferred_element_type=F32)
        else:
            o = (jnp.dot(wj[0][0], v2[:n], preferred_element_type=F32)
                 + jnp.dot(wj[1][0], v2[n:], preferred_element_type=F32))
        lane = lax.broadcasted_iota(jnp.int32, (q_p.shape[0], LANES), 1)
        outs.append((o, jnp.where(lane < SB_DIM, wj[0][1], wj[1][1])))
    return outs


def _sb_kernel(q_ref, kd_ref, vd_ref, kp_ref, vp_ref, ud_ref, up_ref, o_ref, acc_ref, used_ref,
               *, tq, tk, nd, kpi, ntrip_static, transposed_cache):
    npairs = SB_WIDTH // LANES
    pss = [slice(p * LANES, (p + 1) * LANES) for p in range(npairs)]
    acc_ref[...] = jnp.zeros_like(acc_ref)
    used_ref[...] = jnp.zeros_like(used_ref)

    def absorb(results, row0s):
        for idx, (o, r) in enumerate(results):
            ps = pss[idx % npairs]
            rs = slice(row0s[idx], tq)
            used = used_ref[rs, ps]
            acc_ref[rs, ps] = acc_ref[rs, ps] + jnp.exp2(-used) * o
            used_ref[rs, ps] = used + r

    jobs, row0s = [], []
    for d in reversed(range(tq // nd)):
        r_i = lax.broadcasted_iota(jnp.int32, (tq - d * nd, nd), 0)
        c_i = lax.broadcasted_iota(jnp.int32, (tq - d * nd, nd), 1)
        for ps in pss:
            jobs.append((q_ref[0, d * nd:, ps], kd_ref[0, 2 * d * nd:2 * (d + 1) * nd, ps],
                         vd_ref[0, 2 * d * nd:2 * (d + 1) * nd, ps], c_i < r_i))
            row0s.append(d * nd)
    absorb(_sb_tiles(jobs, ud_ref, nd), row0s)

    def cache_pair(ref, cols, p):
        even = ref[0, 2 * p, :, cols]
        odd = ref[0, 2 * p + 1, :, cols]
        zero = jnp.zeros_like(even)
        return jnp.concatenate([jnp.concatenate([even, zero], axis=1),
                                jnp.concatenate([zero, odd], axis=1)], axis=0).astype(BF16)

    def past_jobs(j):
        jobs = []
        for kk in reversed(range(kpi)):
            blk = j * kpi + kk
            if transposed_cache:
                cols = pl.ds(pl.multiple_of(blk * tk, tk), tk)
                for p, ps in enumerate(pss):
                    jobs.append((q_ref[0, :, ps], cache_pair(kp_ref, cols, p), cache_pair(vp_ref, cols, p), None))
            else:
                rows = pl.ds(pl.multiple_of(blk * 2 * tk, 2 * tk), 2 * tk)
                for ps in pss:
                    jobs.append((q_ref[0, :, ps], kp_ref[0, rows, ps], vp_ref[0, rows, ps], None))
        return jobs

    def cond(carry):
        j, least_used = carry
        return jnp.logical_and(j >= 0, least_used < SKIP_BITS)

    def body(carry):
        j, _ = carry
        absorb(_sb_tiles(past_jobs(j), up_ref, tk, transposed=transposed_cache), [0] * (kpi * npairs))
        return j - 1, jnp.min(used_ref[...])

    ntrip = pl.program_id(1) if ntrip_static is None else ntrip_static
    lax.while_loop(cond, body, (ntrip - 1, jnp.min(used_ref[...])))
    o_ref[0] = acc_ref[...]


def _sb_call(qb, kd, vd, kp, vp, *, tq, tk, self_causal):
    B, T, W = qb.shape
    nd = min(tq, tk)
    tok = lambda b, i: (b, i, 0)
    if self_causal:
        Tp = kp.shape[1] // 2
        kpi = tq // tk
        past_spec = pl.BlockSpec((1, 2 * Tp, W), lambda b, i: (b, 0, 0))
    else:
        Tp = kp.shape[3]
        kpi = 2
        past_spec = pl.BlockSpec((1,) + kp.shape[1:], lambda b, i: (b, 0, 0, 0))
    in_specs = [pl.BlockSpec((1, tq, W), tok), pl.BlockSpec((1, 2 * tq, W), tok), pl.BlockSpec((1, 2 * tq, W), tok),
                past_spec, past_spec, _vmem_spec(), _vmem_spec()]
    return pl.pallas_call(
        functools.partial(_sb_kernel, tq=tq, tk=tk, nd=nd, kpi=kpi, transposed_cache=not self_causal,
                          ntrip_static=None if self_causal else Tp // (tk * kpi)),
        out_shape=jax.ShapeDtypeStruct((B, T, W), F32),
        grid=(B, T // tq), in_specs=in_specs, out_specs=pl.BlockSpec((1, tq, W), tok),
        scratch_shapes=[pltpu.VMEM((tq, W), F32), pltpu.VMEM((tq, W), F32)],
        compiler_params=pltpu.CompilerParams(dimension_semantics=("arbitrary", "arbitrary"),
                                             vmem_limit_bytes=VMEM_LIMIT),
        name="stickbreak",
    )(qb, kd, vd, kp, vp, _suffix_matrix(nd), _suffix_matrix(tk))


def _ffn_kernel(x_ref, ya_ref, ob_ref, mod_ref, cst_ref, onw_ref, n2w_ref, wo_ref, wu_ref, wc_ref,
                bc_ref, wd_ref,
                y_ref, cnew_ref, tail_ref, ubuf_ref, *, bb, tm):
    @pl.when(pl.program_id(1) == 0)
    def _():
        tail_ref[...] = cst_ref[...]

    ycat = []
    for bi in range(bb):
        yb = _rms(ob_ref[bi], onw_ref[...]).astype(BF16)
        ycat.append(jnp.concatenate([ya_ref[bi], yb], axis=1))
    ycat = ycat[0] if bb == 1 else jnp.concatenate(ycat, axis=0)
    mix = jnp.dot(ycat, wo_ref[...], preferred_element_type=F32)

    x1s, h2s = [], []
    for bi in range(bb):
        m = mod_ref[bi]
        x1 = x_ref[bi] + m[2:3, :] * mix[bi * tm:(bi + 1) * tm]
        x1s.append(x1)
        h2s.append((_rms(x1, n2w_ref[...]) * (1.0 + m[4:5, :]) + m[3:4, :]).astype(BF16))
    h2 = h2s[0] if bb == 1 else jnp.concatenate(h2s, axis=0)

    def conv(col0, slot):
        cols = slice(col0, col0 + FF_CHUNK)
        u = jnp.dot(h2, wu_ref[:, cols], preferred_element_type=F32)
        w0, w1, w2 = wc_ref[0:1, cols], wc_ref[1:2, cols], wc_ref[2:3, cols]
        outs = []
        for bi in range(bb):
            ub = u[bi * tm:(bi + 1) * tm]
            base = bi * (tm + 8)
            ubuf_ref[slot, base + 6:base + 8, :] = tail_ref[bi, :, cols]
            ubuf_ref[slot, base + 8:base + 8 + tm, :] = ub
            u1 = ubuf_ref[slot, base + 7:base + 7 + tm, :]
            u2 = ubuf_ref[slot, base + 6:base + 6 + tm, :]
            outs.append(u2 * w0 + u1 * w1 + ub * w2 + bc_ref[:, cols])
            tail_ref[bi, :, cols] = ubuf_ref[slot, base + 6 + tm:base + 8 + tm, :]
        return outs[0] if bb == 1 else jnp.concatenate(outs, axis=0)

    def gated(c):
        ua = conv(c * FF_CHUNK, 2 * (c % 2))
        ug = conv(D_FF + c * FF_CHUNK, 2 * (c % 2) + 1)
        return (ua * jax.nn.sigmoid(ua) * ug).astype(BF16)

    acc = None
    pending, first = [gated(0)], 0
    for c in range(N_FF_CHUNKS):
        nxt = gated(c + 1) if c + 1 < N_FF_CHUNKS else None
        if len(pending) == DOWN_GROUP or nxt is None:
            lhs = pending[0] if len(pending) == 1 else jnp.concatenate(pending, axis=1)
            part = jnp.dot(lhs, wd_ref[first * FF_CHUNK:(c + 1) * FF_CHUNK, :], preferred_element_type=F32)
            acc = part if acc is None else acc + part
            pending, first = [], c + 1
        if nxt is not None:
            pending.append(nxt)

    for bi in range(bb):
        y_ref[bi] = x1s[bi] + mod_ref[bi][5:6, :] * acc[bi * tm:(bi + 1) * tm]

    @pl.when(pl.program_id(1) == pl.num_programs(1) - 1)
    def _():
        cnew_ref[...] = tail_ref[...]


def _ffn_call(x, ya, ob, mod3, cstate, wts, *, bb, tm):
    B, T, D = x.shape
    tok = lambda b, t: (b, t, 0)
    bat = lambda b, t: (b, 0, 0)
    in_specs = [pl.BlockSpec((bb, tm, D), tok), pl.BlockSpec((bb, tm, ML_WIDTH), tok),
                pl.BlockSpec((bb, tm, SB_WIDTH), tok), pl.BlockSpec((bb, 6, D), bat),
                pl.BlockSpec((bb, CONV_W - 1, 2 * D_FF), bat)] + [_vmem_spec()] * 7
    out_shape = [jax.ShapeDtypeStruct((B, T, D), F32), jax.ShapeDtypeStruct((B, CONV_W - 1, 2 * D_FF), F32)]
    out_specs = [pl.BlockSpec((bb, tm, D), tok), pl.BlockSpec((bb, CONV_W - 1, 2 * D_FF), bat)]
    return pl.pallas_call(
        functools.partial(_ffn_kernel, bb=bb, tm=tm),
        out_shape=out_shape, grid=(B // bb, T // tm), in_specs=in_specs, out_specs=out_specs,
        scratch_shapes=[pltpu.VMEM((bb, CONV_W - 1, 2 * D_FF), F32),
                        pltpu.VMEM((4, bb * (tm + 8), FF_CHUNK), F32)],
        compiler_params=pltpu.CompilerParams(dimension_semantics=("arbitrary", "arbitrary"),
                                             vmem_limit_bytes=VMEM_LIMIT),
        name="outproj_convffn",
    )(x, ya, ob, mod3, cstate, wts["onw"], wts["n2w"], wts["wo"], wts["wu"], wts["wc"], wts["bc"], wts["wd"])


def _prep_weights(norm1_w, w_in, b_ig, b_fg, ml_norm_w, sb_q_norm_w, sb_k_norm_w, sb_out_norm_w, w_out,
                  norm2_w, w_up, w_conv, b_conv, w_down):
    nm = 4 * ML_WIDTH
    ng = 2 * ML_HEADS
    w_g = w_in[:, nm:nm + ng]
    b_g = jnp.concatenate([b_ig, b_fg])
    return {
        "n1w": norm1_w.reshape(1, D_MODEL),
        "wm": w_in[:, :nm].astype(BF16),
        "wg": jnp.pad(w_g, ((0, 0), (0, LANES - ng))).astype(BF16),
        "wgt": jnp.pad(w_g.T, ((0, 16 - ng), (0, 0))).astype(BF16),
        "ws": w_in[:, nm + ng:].astype(BF16),
        "bgc": jnp.pad(b_g, (0, LANES - ng)).reshape(1, LANES),
        "bgr": jnp.pad(b_g, (0, 16 - ng)).reshape(16, 1),
        "qnw": jnp.tile(sb_q_norm_w, 2).reshape(1, LANES),
        "knw": jnp.tile(sb_k_norm_w, 2).reshape(1, LANES),
        "mnw": ml_norm_w.reshape(1, ML_WIDTH),
        "onw": sb_out_norm_w.reshape(1, SB_WIDTH),
        "n2w": norm2_w.reshape(1, D_MODEL),
        "wo": w_out.astype(BF16),
        "wu": w_up.astype(BF16),
        "wc": jnp.pad(w_conv, ((0, 8 - CONV_W), (0, 0))),
        "bc": b_conv.reshape(1, 2 * D_FF),
        "wd": w_down.astype(BF16),
    }


def _layer(x, mod3, wts, ml_state, k_past, v_past, conv_state, *, bb, tm, chunk, tq, tk):
    B, T, _ = x.shape
    if ml_state is None:
        c0 = jnp.zeros((B, ML_HEADS, ML_DIM, 2 * ML_DIM), F32)
        m0 = jnp.zeros((B, 1, LANES), F32)
    else:
        c_in, n_in, m_in = ml_state
        c0 = jnp.concatenate([c_in, jnp.broadcast_to(n_in[..., None], c_in.shape)], axis=-1)
        m0 = jnp.pad(m_in, ((0, 0), (0, LANES - ML_HEADS))).reshape(B, 1, LANES)
    cps = min(4, T // chunk)
    if bb == 1 and tm == chunk * cps and T > tm:
        qb, kr, vr, k2, v2, ya, c_new, n_new, m_new = _inproj_mlstm_call(
            x, mod3, wts, c0, m0, tm=tm, kblk=min(tq, tk), L=chunk, cps=cps)
    else:
        qa, ka, va, og, gcol, grow, qb, kr, vr, k2, v2 = _inproj_call(x, mod3, wts, bb=bb, tm=tm,
                                                                      kblk=min(tq, tk))
        ya, c_new, n_new, m_new = _mlstm_call(qa, ka, va, og, gcol, grow, wts["mnw"], c0, m0, L=chunk,
                                               cps=cps)

    if k_past is None:
        ob = _sb_call(qb, k2, v2, k2, v2, tq=tq, tk=tk, self_causal=True)
    else:
        ob = _sb_call(qb, k2, v2, jnp.transpose(k_past, (0, 2, 3, 1)), jnp.transpose(v_past, (0, 2, 3, 1)),
                      tq=tq, tk=tk, self_causal=False)

    if conv_state is None:
        conv_state = jnp.zeros((B, CONV_W - 1, 2 * D_FF), F32)
    y, conv_new = _ffn_call(x, ya, ob, mod3, conv_state, wts, bb=bb, tm=tm)
    return (y, c_new, n_new, m_new[:, 0, :ML_HEADS],
            kr.reshape(B, T, SB_HEADS, SB_DIM), vr.reshape(B, T, SB_HEADS, SB_DIM), conv_new)


def kernel(x_prompt, x_sample, c_prompt, c_sample, state_mlstm_c, state_mlstm_n, state_mlstm_m, cache_sb_k, cache_sb_v, state_ffn_conv, w_ada, b_ada, norm1_w, w_in, b_ig, b_fg, ml_norm_w, sb_q_norm_w, sb_k_norm_w, sb_out_norm_w, w_out, norm2_w, w_up, w_conv, b_conv, w_down):
    depth = w_ada.shape[0]
    B = x_prompt.shape[0]
    y_p, y_s = x_prompt, x_sample
    outs_p = [[] for _ in range(6)]
    outs_s = [[] for _ in range(6)]
    for l in range(depth):
        mod = _ada_call(jnp.concatenate([c_prompt, c_sample], axis=0), w_ada[l], b_ada[l])
        mod3 = mod.reshape(mod.shape[0], 6, D_MODEL)
        wts = _prep_weights(norm1_w[l], w_in[l], b_ig[l], b_fg[l], ml_norm_w[l], sb_q_norm_w[l],
                            sb_k_norm_w[l], sb_out_norm_w[l], w_out[l], norm2_w[l], w_up[l], w_conv[l],
                            b_conv[l], w_down[l])
        res_p = _layer(y_p, mod3[:B], wts, None, None, None, None,
                       bb=1, tm=512, chunk=128, tq=256, tk=128)
        res_s = _layer(y_s, mod3[B:], wts, (state_mlstm_c[l], state_mlstm_n[l], state_mlstm_m[l]),
                       cache_sb_k[l], cache_sb_v[l], state_ffn_conv[l],
                       bb=x_sample.shape[0], tm=x_sample.shape[1], chunk=x_sample.shape[1],
                       tq=x_sample.shape[1], tk=128)
        y_p, y_s = res_p[0], res_s[0]
        for i in range(6):
            outs_p[i].append(res_p[i + 1])
            outs_s[i].append(res_s[i + 1])
    return (y_p, y_s, *[jnp.stack(o) for o in outs_p], *[jnp.stack(o) for o in outs_s])
```

```python
import functools

import numpy as np
import jax
import jax.numpy as jnp
from jax import lax
from jax.experimental import pallas as pl
from jax.experimental.pallas import tpu as pltpu

F32 = jnp.float32
BF16 = jnp.bfloat16

D_MODEL = 1024
ML_HEADS = 4
ML_DIM = 128
ML_WIDTH = ML_HEADS * ML_DIM
SB_HEADS = 8
SB_DIM = 64
SB_WIDTH = SB_HEADS * SB_DIM
D_FF = 2816
CONV_W = 3
EPS = 1e-6
LOG2E = 1.4426950408889634
SKIP_BITS = 160.0

LANES = 128
FF_CHUNK = 256
N_FF_CHUNKS = D_FF // FF_CHUNK
DOWN_GROUP = 4
VMEM_LIMIT = 52 * 1024 * 1024
FUSED_ORDER = "cppccccppcccc"

NT_DIMS = (((1,), (1,)), ((), ()))
TN_DIMS = (((0,), (0,)), ((), ()))


def _vmem_spec():
    return pl.BlockSpec(memory_space=pltpu.VMEM)


def _rms(x, w):
    return x * lax.rsqrt(jnp.mean(x * x, axis=-1, keepdims=True) + EPS) * w


def _softplus(x):
    return jnp.maximum(x, 0.0) + jnp.log(1.0 + jnp.exp(-jnp.abs(x)))


def _split3(x):
    h1 = x.astype(BF16)
    r1 = x - h1.astype(F32)
    h2 = r1.astype(BF16)
    h3 = (r1 - h2.astype(F32)).astype(BF16)
    return h1, h2, h3


def _ada_kernel(c_ref, w_ref, b_ref, o_ref):
    c = c_ref[...]
    s = (c * jax.nn.sigmoid(c)).astype(BF16)
    o_ref[...] = jnp.dot(s, w_ref[...].astype(BF16), preferred_element_type=F32) + b_ref[...]


def _ada_call(c, w_ada, b_ada):
    n, d = c.shape
    nout = w_ada.shape[1]
    return pl.pallas_call(
        _ada_kernel,
        out_shape=jax.ShapeDtypeStruct((n, nout), F32),
        grid=(nout // d,),
        in_specs=[pl.BlockSpec((n, d), lambda j: (0, 0)),
                  pl.BlockSpec((d, d), lambda j: (0, j)),
                  pl.BlockSpec((1, d), lambda j: (0, j))],
        out_specs=pl.BlockSpec((n, d), lambda j: (0, j)),
        compiler_params=pltpu.CompilerParams(dimension_semantics=("arbitrary",)),
        name="adaln",
    )(c, w_ada, b_ada.reshape(1, nout))


def _headnorm64(x, w128):
    lane = lax.broadcasted_iota(jnp.int32, x.shape, 1)
    lo = lane < SB_DIM
    x2 = x * x
    s0 = jnp.sum(jnp.where(lo, x2, 0.0), axis=-1, keepdims=True)
    s1 = jnp.sum(jnp.where(lo, 0.0, x2), axis=-1, keepdims=True)
    r0 = lax.rsqrt(s0 * (1.0 / SB_DIM) + EPS)
    r1 = lax.rsqrt(s1 * (1.0 / SB_DIM) + EPS)
    return x * jnp.where(lo, r0, r1) * w128


def _inproj_phases(x_ref, mod_ref, n1w_ref, wm_ref, wg_ref, wgt_ref, ws_ref, bgc_ref, bgr_ref,
                   qnw_ref, knw_ref,
                   qa_ref, ka_ref, va_ref, og_ref, gcol_ref, grow_ref, qb_ref, kr_ref, vr_ref,
                   k2_ref, v2_ref, *, bb, tm, kblk):
    hs = []
    for bi in range(bb):
        m = mod_ref[bi]
        y = _rms(x_ref[bi], n1w_ref[...])
        hs.append((y * (1.0 + m[1:2, :]) + m[0:1, :]).astype(BF16))
    h = hs[0] if bb == 1 else jnp.concatenate(hs, axis=0)

    def put(ref, val):
        for bi in range(bb):
            ref[bi] = val[bi * tm:(bi + 1) * tm].astype(ref.dtype)

    def proj(w_ref, j):
        return jnp.dot(h, w_ref[:, j * ML_WIDTH:(j + 1) * ML_WIDTH], preferred_element_type=F32)

    put(qa_ref, proj(wm_ref, 0))
    put(ka_ref, proj(wm_ref, 1) * (ML_DIM ** -0.5))
    yield
    put(va_ref, proj(wm_ref, 2))
    put(og_ref, jax.nn.sigmoid(proj(wm_ref, 3)))
    yield

    gc = jnp.dot(h, wg_ref[...], preferred_element_type=F32) + bgc_ref[...]
    lane = lax.broadcasted_iota(jnp.int32, gc.shape, 1)
    put(gcol_ref, jnp.where(lane >= ML_HEADS, -_softplus(-gc), gc))
    gr = lax.dot_general(wgt_ref[...], h, NT_DIMS, preferred_element_type=F32) + bgr_ref[...]
    row = lax.broadcasted_iota(jnp.int32, gr.shape, 0)
    gr = jnp.where(row >= ML_HEADS, -_softplus(-gr), gr)
    for bi in range(bb):
        grow_ref[bi] = gr[:2 * ML_HEADS, bi * tm:(bi + 1) * tm]

    sq = proj(ws_ref, 0)
    sk = proj(ws_ref, 1)
    sv = proj(ws_ref, 2)
    yield
    qn, kn = [], []
    for j in range(SB_WIDTH // LANES):
        sl = slice(j * LANES, (j + 1) * LANES)
        qn.append(_headnorm64(sq[:, sl], qnw_ref[...]) * (SB_DIM ** -0.5 * LOG2E))
        kn.append(_headnorm64(sk[:, sl], knw_ref[...]))
    qn = jnp.concatenate(qn, axis=1)
    kn = jnp.concatenate(kn, axis=1)
    put(qb_ref, qn)
    put(kr_ref, kn)
    put(vr_ref, sv)

    even = (lax.broadcasted_iota(jnp.int32, kn.shape, 1) & SB_DIM) == 0

    def put_pairs(ref, val):
        ve = jnp.where(even, val, 0.0).astype(BF16)
        vo = jnp.where(even, 0.0, val).astype(BF16)
        for bi in range(bb):
            parts = []
            for r0 in range(bi * tm, (bi + 1) * tm, kblk):
                parts += [ve[r0:r0 + kblk], vo[r0:r0 + kblk]]
            ref[bi] = jnp.concatenate(parts, axis=0)

    put_pairs(k2_ref, kn)
    put_pairs(v2_ref, sv)
    yield


def _inproj_kernel(*refs, bb, tm, kblk):
    for _ in _inproj_phases(*refs, bb=bb, tm=tm, kblk=kblk):
        pass


def _inproj_call(x, mod3, wts, *, bb, tm, kblk):
    B, T, D = x.shape
    grid = (B // bb, T // tm)
    tok = lambda b, t: (b, t, 0)

    def tok_spec(w):
        return pl.BlockSpec((bb, tm, w), tok)

    def tok_shape(w, dt):
        return jax.ShapeDtypeStruct((B, T, w), dt)

    out_shape = [tok_shape(ML_WIDTH, BF16), tok_shape(ML_WIDTH, BF16), tok_shape(ML_WIDTH, BF16),
                 tok_shape(ML_WIDTH, F32), tok_shape(LANES, F32),
                 jax.ShapeDtypeStruct((B, 2 * ML_HEADS, T), F32),
                 tok_shape(SB_WIDTH, BF16), tok_shape(SB_WIDTH, F32), tok_shape(SB_WIDTH, F32),
                 jax.ShapeDtypeStruct((B, 2 * T, SB_WIDTH), BF16),
                 jax.ShapeDtypeStruct((B, 2 * T, SB_WIDTH), BF16)]
    pair_spec = pl.BlockSpec((bb, 2 * tm, SB_WIDTH), tok)
    out_specs = [tok_spec(ML_WIDTH), tok_spec(ML_WIDTH), tok_spec(ML_WIDTH), tok_spec(ML_WIDTH),
                 tok_spec(LANES), pl.BlockSpec((bb, 2 * ML_HEADS, tm), lambda b, t: (b, 0, t)),
                 tok_spec(SB_WIDTH), tok_spec(SB_WIDTH), tok_spec(SB_WIDTH), pair_spec, pair_spec]
    in_specs = [tok_spec(D), pl.BlockSpec((bb, 6, D), lambda b, t: (b, 0, 0))] + [_vmem_spec()] * 9
    return pl.pallas_call(
        functools.partial(_inproj_kernel, bb=bb, tm=tm, kblk=kblk),
        out_shape=out_shape, grid=grid, in_specs=in_specs, out_specs=out_specs,
        compiler_params=pltpu.CompilerParams(dimension_semantics=("arbitrary", "arbitrary"),
                                             vmem_limit_bytes=VMEM_LIMIT),
        name="inproj",
    )(x, mod3, wts["n1w"], wts["wm"], wts["wg"], wts["wgt"], wts["ws"], wts["bgc"], wts["bgr"],
      wts["qnw"], wts["knw"])


def _mlstm_kernel(q_ref, k_ref, v_ref, og_ref, gcol_ref, grow_ref, nw_ref, c0_ref, m0_ref,
                  tri_ref, trit_ref,
                  ya_ref, cout_ref, nout_ref, mout_ref,
                  c_scr, m_scr, *, L, cps):
    @pl.when(pl.program_id(1) == 0)
    def _():
        c_scr[...] = c0_ref[0]
        m_scr[...] = m0_ref[0]

    for _ in _mlstm_phases(q_ref, k_ref, v_ref, og_ref, gcol_ref, grow_ref, nw_ref, tri_ref, trit_ref,
                           ya_ref, cout_ref, nout_ref, mout_ref, c_scr, m_scr, L=L, cps=cps):
        pass


def _mlstm_phases(q_ref, k_ref, v_ref, og_ref, gcol_ref, grow_ref, nw_ref, tri_ref, trit_ref,
                  ya_ref, cout_ref, nout_ref, mout_ref, c_scr, m_scr, *, L, cps):
    heads = range(ML_HEADS)
    hss = [slice(h * ML_DIM, (h + 1) * ML_DIM) for h in heads]
    rowss = [slice(cc * L, (cc + 1) * L) for cc in range(cps)]
    r_i = lax.broadcasted_iota(jnp.int32, (L, L), 0)
    c_i = lax.broadcasted_iota(jnp.int32, (L, L), 1)
    causal = c_i <= r_i
    lane1 = lax.broadcasted_iota(jnp.int32, (1, LANES), 1)
    ones = jnp.ones((L, LANES), BF16)

    qks = []
    for rows in rowss:
        qks.append([lax.dot_general(q_ref[0, rows, hs], k_ref[0, rows, hs], NT_DIMS,
                                    preferred_element_type=F32) for hs in hss])
    gcols = [gcol_ref[0, rows, :] for rows in rowss]
    grows = [grow_ref[0, :, rows] for rows in rowss]
    col_parts = [p for g in gcols for p in _split3(g)]
    row_parts = [p for g in grows for p in _split3(jnp.concatenate([g, jnp.zeros_like(g)], axis=0))]
    bc_all = jnp.dot(tri_ref[...], jnp.concatenate(col_parts, axis=1), preferred_element_type=F32)
    br_all = jnp.dot(jnp.concatenate(row_parts, axis=0), trit_ref[...], preferred_element_type=F32)
    gates = []
    for cc in range(cps):
        bcol = sum(bc_all[:, (3 * cc + i) * LANES:(3 * cc + i + 1) * LANES] for i in range(3))
        brow = sum(br_all[(3 * cc + i) * 16:(3 * cc + i + 1) * 16, :] for i in range(3))
        gates.append((gcols[cc], grows[cc], bcol, brow))
    yield

    m_vec = m_scr[...]
    m_run = [m_vec[:, h:h + 1] for h in heads]
    plan = []

    def prepare(cc):
        rows = rowss[cc]
        gcol, grow, bcol, brow = gates[cc]
        per_head = []
        for h in heads:
            b_c = jnp.broadcast_to(bcol[:, ML_HEADS + h:ML_HEADS + h + 1], (L, LANES))
            ig_c = jnp.broadcast_to(gcol[:, h:h + 1], (L, LANES))
            b_r = brow[ML_HEADS + h:ML_HEADS + h + 1, :]
            ig_r = grow[h:h + 1, :]
            b_last = b_r[:, L - 1:L]
            g_r = ig_r - b_r
            m_h = m_run[h]
            dlog = jnp.where(causal, b_c[:, :L] + g_r, -jnp.inf)
            inter = b_c + m_h
            m_t = jnp.maximum(inter, jnp.max(dlog, axis=-1, keepdims=True))
            w_intra = jnp.exp(dlog - m_t[:, :L])
            wk_r = b_last + g_r
            m_new = jnp.maximum(b_last + m_h, jnp.max(wk_r, axis=-1, keepdims=True))
            wk_c = jnp.exp(b_last - b_c + ig_c - m_new)
            v = v_ref[0, rows, hss[h]]
            per_head.append({
                "s": (qks[cc][h] * w_intra).astype(BF16),
                "w_inter": jnp.exp(inter - m_t),
                "floor": jnp.exp(-m_t),
                "decay": jnp.exp(b_last + m_h - m_new),
                "vaug": jnp.concatenate([v, ones], axis=1),
                "wv": jnp.concatenate([(wk_c * v.astype(F32)).astype(BF16), wk_c.astype(BF16)], axis=1),
            })
            m_run[h] = m_new
        for h in heads:
            ph = per_head[h]
            ph["intra"] = jnp.dot(ph["s"], ph["vaug"], preferred_element_type=F32)
            ph["upd"] = lax.dot_general(k_ref[0, rows, hss[h]], ph["wv"], TN_DIMS, preferred_element_type=F32)
        plan.append(per_head)

    def advance(cc):
        rows = rowss[cc]
        per_head = plan[cc]
        for h in heads:
            ph = per_head[h]
            hs = hss[h]
            cst = c_scr[h]
            qc = jnp.dot(q_ref[0, rows, hs], cst.astype(BF16), preferred_element_type=F32)
            num = ph["w_inter"] * qc[:, :ML_DIM] + ph["intra"][:, :ML_DIM]
            den = ph["w_inter"] * qc[:, ML_DIM:] + ph["intra"][:, ML_DIM:]
            hcell = num / jnp.maximum(jnp.abs(den), ph["floor"])
            ya_ref[0, rows, hs] = (_rms(hcell, nw_ref[:, hs]) * og_ref[0, rows, hs]).astype(BF16)
            c_scr[h] = ph["decay"] * cst + ph["upd"]

    for cc in range(cps):
        prepare(cc)
        yield
    for cc in range(cps):
        advance(cc)
        yield

    m_out = m_vec
    mean_row = jnp.full((8, ML_DIM), 1.0 / ML_DIM, BF16)
    for h in heads:
        cst = c_scr[h]
        cout_ref[0, h] = cst[:, :ML_DIM]
        n_rows = sum(lax.dot_general(mean_row, p, NT_DIMS, preferred_element_type=F32)
                     for p in _split3(cst[:, ML_DIM:]))
        nout_ref[0, h:h + 1, :] = n_rows[0:1, :]
        m_out = jnp.where(lane1 == h, m_run[h], m_out)
    m_scr[...] = m_out
    mout_ref[0] = m_out


def _mlstm_call(qa, ka, va, og, gcol, grow, nw, c0, m0, *, L, cps):
    B, T, _ = qa.shape
    tri_np = np.tril(np.ones((L, L), np.float32))
    tri = jnp.asarray(tri_np, BF16)
    trit = jnp.asarray(tri_np.T, BF16)
    tok = lambda b, c: (b, c, 0)
    bat3 = lambda b, c: (b, 0, 0)
    bat4 = lambda b, c: (b, 0, 0, 0)
    tl = L * cps
    in_specs = [pl.BlockSpec((1, tl, ML_WIDTH), tok)] * 4 + [
        pl.BlockSpec((1, tl, LANES), tok),
        pl.BlockSpec((1, 2 * ML_HEADS, tl), lambda b, c: (b, 0, c)),
        _vmem_spec(),
        pl.BlockSpec((1, ML_HEADS, ML_DIM, 2 * ML_DIM), bat4),
        pl.BlockSpec((1, 1, LANES), bat3),
        _vmem_spec(), _vmem_spec()]
    out_shape = [jax.ShapeDtypeStruct((B, T, ML_WIDTH), BF16),
                 jax.ShapeDtypeStruct((B, ML_HEADS, ML_DIM, ML_DIM), F32),
                 jax.ShapeDtypeStruct((B, ML_HEADS, ML_DIM), F32),
                 jax.ShapeDtypeStruct((B, 1, LANES), F32)]
    out_specs = [pl.BlockSpec((1, tl, ML_WIDTH), tok),
                 pl.BlockSpec((1, ML_HEADS, ML_DIM, ML_DIM), bat4),
                 pl.BlockSpec((1, ML_HEADS, ML_DIM), bat3),
                 pl.BlockSpec((1, 1, LANES), bat3)]
    return pl.pallas_call(
        functools.partial(_mlstm_kernel, L=L, cps=cps),
        out_shape=out_shape, grid=(B, T // tl), in_specs=in_specs, out_specs=out_specs,
        scratch_shapes=[pltpu.VMEM((ML_HEADS, ML_DIM, 2 * ML_DIM), F32),
                        pltpu.VMEM((1, LANES), F32)],
        compiler_params=pltpu.CompilerParams(dimension_semantics=("arbitrary", "arbitrary"),
                                             vmem_limit_bytes=VMEM_LIMIT),
        name="mlstm",
    )(qa, ka, va, og, gcol, grow, nw, c0, m0, tri, trit)


def _inproj_mlstm_kernel(x_ref, mod_ref, n1w_ref, wm_ref, wg_ref, wgt_ref, ws_ref, bgc_ref, bgr_ref,
                         qnw_ref, knw_ref, nw_ref, c0_ref, m0_ref, tri_ref, trit_ref,
                         qb_ref, kr_ref, vr_ref, k2_ref, v2_ref, ya_ref, cout_ref, nout_ref, mout_ref,
                         cur_q, cur_k, cur_v, cur_og, cur_gc, cur_gr,
                         prv_q, prv_k, prv_v, prv_og, prv_gc, prv_gr, c_scr, m_scr,
                         *, tm, kblk, L, cps, nt):
    s = pl.program_id(0)
    cur = (cur_q, cur_k, cur_v, cur_og, cur_gc, cur_gr)
    prv = (prv_q, prv_k, prv_v, prv_og, prv_gc, prv_gr)

    @pl.when(s == 0)
    def _():
        for ref in prv + (c_scr, m_scr):
            ref[...] = jnp.zeros_like(ref)

    @pl.when(jnp.logical_and(s > 0, lax.rem(jnp.maximum(s - 1, 0), nt) == 0))
    def _():
        c_scr[...] = c0_ref[0]
        m_scr[...] = m0_ref[0]

    proj = _inproj_phases(x_ref, mod_ref, n1w_ref, wm_ref, wg_ref, wgt_ref, ws_ref, bgc_ref, bgr_ref,
                          qnw_ref, knw_ref, *cur, qb_ref, kr_ref, vr_ref, k2_ref, v2_ref,
                          bb=1, tm=tm, kblk=kblk)
    cell = _mlstm_phases(*prv, nw_ref, tri_ref, trit_ref, ya_ref, cout_ref, nout_ref, mout_ref,
                         c_scr, m_scr, L=L, cps=cps)
    for turn in FUSED_ORDER:
        next(cell if turn == "c" else proj, None)
    for gen in (cell, proj):
        for _ in gen:
            pass
    for c_ref, p_ref in zip(cur, prv):
        p_ref[...] = c_ref[...]


def _inproj_mlstm_call(x, mod3, wts, c0, m0, *, tm, kblk, L, cps):
    B, T, D = x.shape
    nt = T // tm
    n_tiles = B * nt
    tl = L * cps
    assert tl == tm
    tri_np = np.tril(np.ones((L, L), np.float32))

    def cur_tile(s):
        t = jnp.minimum(s, n_tiles - 1)
        return t // nt, t % nt

    def prev_tile(s):
        t = jnp.maximum(s - 1, 0)
        return t // nt, t % nt

    tok = lambda s: (*cur_tile(s), 0)
    tok_prev = lambda s: (*prev_tile(s), 0)
    bat3 = lambda s: (prev_tile(s)[0], 0, 0)
    bat4 = lambda s: (prev_tile(s)[0], 0, 0, 0)
    in_specs = ([pl.BlockSpec((1, tm, D), tok), pl.BlockSpec((1, 6, D), lambda s: (cur_tile(s)[0], 0, 0))]
                + [_vmem_spec()] * 10
                + [pl.BlockSpec((1, ML_HEADS, ML_DIM, 2 * ML_DIM), bat4),
                   pl.BlockSpec((1, 1, LANES), bat3), _vmem_spec(), _vmem_spec()])
    sb_shape = jax.ShapeDtypeStruct((B, T, SB_WIDTH), F32)
    pair_shape = jax.ShapeDtypeStruct((B, 2 * T, SB_WIDTH), BF16)
    out_shape = [jax.ShapeDtypeStruct((B, T, SB_WIDTH), BF16), sb_shape, sb_shape, pair_shape, pair_shape,
                 jax.ShapeDtypeStruct((B, T, ML_WIDTH), BF16),
                 jax.ShapeDtypeStruct((B, ML_HEADS, ML_DIM, ML_DIM), F32),
                 jax.ShapeDtypeStruct((B, ML_HEADS, ML_DIM), F32),
                 jax.ShapeDtypeStruct((B, 1, LANES), F32)]
    sb_spec = pl.BlockSpec((1, tm, SB_WIDTH), tok)
    pair_spec = pl.BlockSpec((1, 2 * tm, SB_WIDTH), tok)
    out_specs = [sb_spec, sb_spec, sb_spec, pair_spec, pair_spec,
                 pl.BlockSpec((1, tm, ML_WIDTH), tok_prev),
                 pl.BlockSpec((1, ML_HEADS, ML_DIM, ML_DIM), bat4),
                 pl.BlockSpec((1, ML_HEADS, ML_DIM), bat3),
                 pl.BlockSpec((1, 1, LANES), bat3)]
    operands = [pltpu.VMEM((1, tm, ML_WIDTH), BF16)] * 3 + [pltpu.VMEM((1, tm, ML_WIDTH), F32),
                                                           pltpu.VMEM((1, tm, LANES), F32),
                                                           pltpu.VMEM((1, 2 * ML_HEADS, tm), F32)]
    return pl.pallas_call(
        functools.partial(_inproj_mlstm_kernel, tm=tm, kblk=kblk, L=L, cps=cps, nt=nt),
        out_shape=out_shape, grid=(n_tiles + 1,), in_specs=in_specs, out_specs=out_specs,
        scratch_shapes=operands + operands + [pltpu.VMEM((ML_HEADS, ML_DIM, 2 * ML_DIM), F32),
                                              pltpu.VMEM((1, LANES), F32)],
        compiler_params=pltpu.CompilerParams(dimension_semantics=("arbitrary",),
                                             vmem_limit_bytes=VMEM_LIMIT),
        name="inproj_mlstm",
    )(x, mod3, wts["n1w"], wts["wm"], wts["wg"], wts["wgt"], wts["ws"], wts["bgc"], wts["bgr"],
      wts["qnw"], wts["knw"], wts["mnw"], c0, m0, jnp.asarray(tri_np, BF16), jnp.asarray(tri_np.T, BF16))


def _suffix_matrix(n):
    u = np.zeros((2 * n, 2 * LANES), np.float32)
    j = np.arange(n)[:, None]
    s = np.arange(n)[None, :]
    u[:n, :n] = (j >= s)
    u[n:, :n] = (j >= s)
    u[:, LANES:] = 1.0
    return jnp.asarray(u, BF16)


def _sb_tiles(jobs, u_ref, n, transposed=False):
    fused = n % LANES == 0
    zs = []
    for q_p, k2, _, _ in jobs:
        if transposed:
            z2 = jnp.dot(q_p, k2, preferred_element_type=F32)
            zs.append([z2[:, :n], z2[:, n:]])
        elif fused:
            z2 = lax.dot_general(q_p, k2, NT_DIMS, preferred_element_type=F32)
            zs.append([z2[:, :n], z2[:, n:]])
        else:
            zs.append([lax.dot_general(q_p, k2[hh * n:(hh + 1) * n], NT_DIMS, preferred_element_type=F32)
                       for hh in range(2)])
    splits = []
    for (_, _, _, mask), zj in zip(jobs, zs):
        sj = []
        for z in zj:
            s = jnp.maximum(z, 0.0) + jnp.log2(1.0 + jnp.exp2(-jnp.abs(z)))
            if mask is not None:
                s = jnp.where(mask, s, 0.0)
            hi = s.astype(BF16)
            sj.append((hi, (s - hi.astype(F32)).astype(BF16)))
        splits.append(sj)
    weights = []
    for (_, _, _, mask), zj, sj in zip(jobs, zs, splits):
        wj = []
        for z, (hi, lo) in zip(zj, sj):
            if fused:
                res = jnp.dot(jnp.concatenate([hi, lo], axis=1), u_ref[...], preferred_element_type=F32)
            else:
                res = (jnp.dot(hi, u_ref[:n, :], preferred_element_type=F32)
                       + jnp.dot(lo, u_ref[n:, :], preferred_element_type=F32))
            a = jnp.exp2(z - res[:, :n])
            if mask is not None:
                a = jnp.where(mask, a, 0.0)
            wj.append((a.astype(BF16), res[:, LANES:]))
        weights.append(wj)
    outs = []
    for (q_p, _, v2, _), wj in zip(jobs, weights):
        if transposed:
            o = lax.dot_general(jnp.concatenate([wj[0][0], wj[1][0]], axis=1), v2, NT_DIMS,
                                preferred_element_type=F32)
        elif fused:
            o = jnp.dot(jnp.concatenate([wj[0][0], wj[1][0]], axis=1), v2, preferred_element_type=F32)
        else:
            o = (jnp.dot(wj[0][0], v2[:n], preferred_element_type=F32)
                 + jnp.dot(wj[1][0], v2[n:], preferred_element_type=F32))
        lane = lax.broadcasted_iota(jnp.int32, (q_p.shape[0], LANES), 1)
        outs.append((o, jnp.where(lane < SB_DIM, wj[0][1], wj[1][1])))
    return outs


def _sb_kernel(q_ref, kd_ref, vd_ref, kp_ref, vp_ref, ud_ref, up_ref, o_ref, acc_ref, used_ref,
               *, tq, tk, nd, kpi, ntrip_static, transposed_cache):
    npairs = SB_WIDTH // LANES
    pss = [slice(p * LANES, (p + 1) * LANES) for p in range(npairs)]
    acc_ref[...] = jnp.zeros_like(acc_ref)
    used_ref[...] = jnp.zeros_like(used_ref)

    def absorb(results, row0s):
        for idx, (o, r) in enumerate(results):
            ps = pss[idx % npairs]
            rs = slice(row0s[idx], tq)
            used = used_ref[rs, ps]
            acc_ref[rs, ps] = acc_ref[rs, ps] + jnp.exp2(-used) * o
            used_ref[rs, ps] = used + r

    jobs, row0s = [], []
    for d in reversed(range(tq // nd)):
        r_i = lax.broadcasted_iota(jnp.int32, (tq - d * nd, nd), 0)
        c_i = lax.broadcasted_iota(jnp.int32, (tq - d * nd, nd), 1)
        for ps in pss:
            jobs.append((q_ref[0, d * nd:, ps], kd_ref[0, 2 * d * nd:2 * (d + 1) * nd, ps],
                         vd_ref[0, 2 * d * nd:2 * (d + 1) * nd, ps], c_i < r_i))
            row0s.append(d * nd)
    absorb(_sb_tiles(jobs, ud_ref, nd), row0s)

    def cache_pair(ref, cols, p):
        even = ref[0, 2 * p, :, cols]
        odd = ref[0, 2 * p + 1, :, cols]
        zero = jnp.zeros_like(even)
        return jnp.concatenate([jnp.concatenate([even, zero], axis=1),
                                jnp.concatenate([zero, odd], axis=1)], axis=0).astype(BF16)

    def past_jobs(j):
        jobs = []
        for kk in reversed(range(kpi)):
            blk = j * kpi + kk
            if transposed_cache:
                cols = pl.ds(pl.multiple_of(blk * tk, tk), tk)
                for p, ps in enumerate(pss):
                    jobs.append((q_ref[0, :, ps], cache_pair(kp_ref, cols, p), cache_pair(vp_ref, cols, p), None))
            else:
                rows = pl.ds(pl.multiple_of(blk * 2 * tk, 2 * tk), 2 * tk)
                for ps in pss:
                    jobs.append((q_ref[0, :, ps], kp_ref[0, rows, ps], vp_ref[0, rows, ps], None))
        return jobs

    def cond(carry):
        j, least_used = carry
        return jnp.logical_and(j >= 0, least_used < SKIP_BITS)

    def body(carry):
        j, _ = carry
        absorb(_sb_tiles(past_jobs(j), up_ref, tk, transposed=transposed_cache), [0] * (kpi * npairs))
        return j - 1, jnp.min(used_ref[...])

    ntrip = pl.program_id(1) if ntrip_static is None else ntrip_static
    lax.while_loop(cond, body, (ntrip - 1, jnp.min(used_ref[...])))
    o_ref[0] = acc_ref[...]


def _sb_band_kernel(q_ref, kd_ref, vd_ref, kp_ref, vp_ref, u_ref, o_ref, acc_ref, used_ref, *, tk):
    npairs = SB_WIDTH // LANES
    pss = [slice(p * LANES, (p + 1) * LANES) for p in range(npairs)]
    tq = 2 * tk
    i = pl.program_id(1)
    acc_ref[...] = jnp.zeros_like(acc_ref)
    used_ref[...] = jnp.zeros_like(used_ref)

    def absorb(results, spans, valids):
        for idx, (o, r) in enumerate(results):
            ps = pss[idx % npairs]
            rs = slice(*spans[idx])
            if valids[idx] is not None:
                o, r = o * valids[idx], r * valids[idx]
            used = used_ref[rs, ps]
            acc_ref[rs, ps] = acc_ref[rs, ps] + jnp.exp2(-used) * o
            used_ref[rs, ps] = used + r

    def block(b):
        rows = pl.ds(pl.multiple_of(jnp.maximum(b, 0) * 2 * tk, 2 * tk), 2 * tk)
        return rows, jnp.where(b >= 0, 1.0, 0.0).astype(F32).reshape(1, 1)

    def tiles(row_span, rows, mask):
        lo, hi = row_span
        return [(q_ref[0, lo:hi, ps], kp_ref[0, rows, ps], vp_ref[0, rows, ps], mask) for ps in pss]

    upper, lower, both = (tk, tq), (0, tk), (0, tq)
    strict = lambda n: (lax.broadcasted_iota(jnp.int32, (n, tk), 1) < lax.broadcasted_iota(jnp.int32, (n, tk), 0))
    rows1, ok1 = block(2 * i - 1)
    rows2, ok2 = block(2 * i - 2)
    jobs = ([(q_ref[0, tk:, ps], kd_ref[0, 2 * tk:, ps], vd_ref[0, 2 * tk:, ps], strict(tk)) for ps in pss]
            + [(q_ref[0, :, ps], kd_ref[0, :2 * tk, ps], vd_ref[0, :2 * tk, ps], strict(tq)) for ps in pss]
            + tiles(both, rows1, None) + tiles(lower, rows2, None))
    absorb(_sb_tiles(jobs, u_ref, tk),
           [upper] * npairs + [both] * npairs + [both] * npairs + [lower] * npairs,
           [None] * (2 * npairs) + [ok1] * npairs + [ok2] * npairs)

    def cond(carry):
        b, least_used = carry
        return jnp.logical_and(b >= 0, least_used < SKIP_BITS)

    def body(carry):
        b, _ = carry
        rows_u, ok_u = block(b)
        rows_l, ok_l = block(b - 1)
        absorb(_sb_tiles(tiles(upper, rows_u, None) + tiles(lower, rows_l, None), u_ref, tk),
               [upper] * npairs + [lower] * npairs, [ok_u] * npairs + [ok_l] * npairs)
        return b - 1, jnp.min(used_ref[...])

    lax.while_loop(cond, body, (2 * i - 2, jnp.min(used_ref[...])))
    o_ref[0] = acc_ref[...]


def _sb_band_call(qb, k2, v2, *, tk):
    B, T, W = qb.shape
    tq = 2 * tk
    tok = lambda b, i: (b, i, 0)
    seq = pl.BlockSpec((1, 2 * T, W), lambda b, i: (b, 0, 0))
    own = pl.BlockSpec((1, 2 * tq, W), tok)
    return pl.pallas_call(
        functools.partial(_sb_band_kernel, tk=tk),
        out_shape=jax.ShapeDtypeStruct((B, T, W), F32),
        grid=(B, T // tq), in_specs=[pl.BlockSpec((1, tq, W), tok), own, own, seq, seq, _vmem_spec()],
        out_specs=pl.BlockSpec((1, tq, W), tok),
        scratch_shapes=[pltpu.VMEM((tq, W), F32), pltpu.VMEM((tq, W), F32)],
        compiler_params=pltpu.CompilerParams(dimension_semantics=("arbitrary", "arbitrary"),
                                             vmem_limit_bytes=VMEM_LIMIT),
        name="stickbreak_band",
    )(qb, k2, v2, k2, v2, _suffix_matrix(tk))


def _sb_call(qb, kd, vd, kp, vp, *, tq, tk, self_causal):
    B, T, W = qb.shape
    nd = min(tq, tk)
    tok = lambda b, i: (b, i, 0)
    if self_causal:
        Tp = kp.shape[1] // 2
        kpi = tq // tk
        past_spec = pl.BlockSpec((1, 2 * Tp, W), lambda b, i: (b, 0, 0))
    else:
        Tp = kp.shape[3]
        kpi = 2
        past_spec = pl.BlockSpec((1,) + kp.shape[1:], lambda b, i: (b, 0, 0, 0))
    in_specs = [pl.BlockSpec((1, tq, W), tok), pl.BlockSpec((1, 2 * tq, W), tok), pl.BlockSpec((1, 2 * tq, W), tok),
                past_spec, past_spec, _vmem_spec(), _vmem_spec()]
    return pl.pallas_call(
        functools.partial(_sb_kernel, tq=tq, tk=tk, nd=nd, kpi=kpi, transposed_cache=not self_causal,
                          ntrip_static=None if self_causal else Tp // (tk * kpi)),
        out_shape=jax.ShapeDtypeStruct((B, T, W), F32),
        grid=(B, T // tq), in_specs=in_specs, out_specs=pl.BlockSpec((1, tq, W), tok),
        scratch_shapes=[pltpu.VMEM((tq, W), F32), pltpu.VMEM((tq, W), F32)],
        compiler_params=pltpu.CompilerParams(dimension_semantics=("arbitrary", "arbitrary"),
                                             vmem_limit_bytes=VMEM_LIMIT),
        name="stickbreak",
    )(qb, kd, vd, kp, vp, _suffix_matrix(nd), _suffix_matrix(tk))


def _ffn_kernel(x_ref, ya_ref, ob_ref, mod_ref, cst_ref, onw_ref, n2w_ref, wo_ref, wu_ref, wc_ref,
                bc_ref, wd_ref,
                y_ref, cnew_ref, tail_ref, ubuf_ref, *, bb, tm):
    @pl.when(pl.program_id(1) == 0)
    def _():
        tail_ref[...] = cst_ref[...]

    ycat = []
    for bi in range(bb):
        yb = _rms(ob_ref[bi], onw_ref[...]).astype(BF16)
        ycat.append(jnp.concatenate([ya_ref[bi], yb], axis=1))
    ycat = ycat[0] if bb == 1 else jnp.concatenate(ycat, axis=0)
    mix = jnp.dot(ycat, wo_ref[...], preferred_element_type=F32)

    x1s, h2s = [], []
    for bi in range(bb):
        m = mod_ref[bi]
        x1 = x_ref[bi] + m[2:3, :] * mix[bi * tm:(bi + 1) * tm]
        x1s.append(x1)
        h2s.append((_rms(x1, n2w_ref[...]) * (1.0 + m[4:5, :]) + m[3:4, :]).astype(BF16))
    h2 = h2s[0] if bb == 1 else jnp.concatenate(h2s, axis=0)

    def conv(col0, slot):
        cols = slice(col0, col0 + FF_CHUNK)
        u = jnp.dot(h2, wu_ref[:, cols], preferred_element_type=F32)
        w0, w1, w2 = wc_ref[0:1, cols], wc_ref[1:2, cols], wc_ref[2:3, cols]
        outs = []
        for bi in range(bb):
            ub = u[bi * tm:(bi + 1) * tm]
            base = bi * (tm + 8)
            ubuf_ref[slot, base + 6:base + 8, :] = tail_ref[bi, :, cols]
            ubuf_ref[slot, base + 8:base + 8 + tm, :] = ub
            u1 = ubuf_ref[slot, base + 7:base + 7 + tm, :]
            u2 = ubuf_ref[slot, base + 6:base + 6 + tm, :]
            outs.append(u2 * w0 + u1 * w1 + ub * w2 + bc_ref[:, cols])
            tail_ref[bi, :, cols] = ubuf_ref[slot, base + 6 + tm:base + 8 + tm, :]
        return outs[0] if bb == 1 else jnp.concatenate(outs, axis=0)

    def gated(c):
        ua = conv(c * FF_CHUNK, 2 * (c % 2))
        ug = conv(D_FF + c * FF_CHUNK, 2 * (c % 2) + 1)
        return (ua * jax.nn.sigmoid(ua) * ug).astype(BF16)

    acc = None
    pending, first = [gated(0)], 0
    for c in range(N_FF_CHUNKS):
        nxt = gated(c + 1) if c + 1 < N_FF_CHUNKS else None
        if len(pending) == DOWN_GROUP or nxt is None:
            lhs = pending[0] if len(pending) == 1 else jnp.concatenate(pending, axis=1)
            part = jnp.dot(lhs, wd_ref[first * FF_CHUNK:(c + 1) * FF_CHUNK, :], preferred_element_type=F32)
            acc = part if acc is None else acc + part
            pending, first = [], c + 1
        if nxt is not None:
            pending.append(nxt)

    for bi in range(bb):
        y_ref[bi] = x1s[bi] + mod_ref[bi][5:6, :] * acc[bi * tm:(bi + 1) * tm]

    @pl.when(pl.program_id(1) == pl.num_programs(1) - 1)
    def _():
        cnew_ref[...] = tail_ref[...]


def _ffn_call(x, ya, ob, mod3, cstate, wts, *, bb, tm):
    B, T, D = x.shape
    tok = lambda b, t: (b, t, 0)
    bat = lambda b, t: (b, 0, 0)
    in_specs = [pl.BlockSpec((bb, tm, D), tok), pl.BlockSpec((bb, tm, ML_WIDTH), tok),
                pl.BlockSpec((bb, tm, SB_WIDTH), tok), pl.BlockSpec((bb, 6, D), bat),
                pl.BlockSpec((bb, CONV_W - 1, 2 * D_FF), bat)] + [_vmem_spec()] * 7
    out_shape = [jax.ShapeDtypeStruct((B, T, D), F32), jax.ShapeDtypeStruct((B, CONV_W - 1, 2 * D_FF), F32)]
    out_specs = [pl.BlockSpec((bb, tm, D), tok), pl.BlockSpec((bb, CONV_W - 1, 2 * D_FF), bat)]
    return pl.pallas_call(
        functools.partial(_ffn_kernel, bb=bb, tm=tm),
        out_shape=out_shape, grid=(B // bb, T // tm), in_specs=in_specs, out_specs=out_specs,
        scratch_shapes=[pltpu.VMEM((bb, CONV_W - 1, 2 * D_FF), F32),
                        pltpu.VMEM((4, bb * (tm + 8), FF_CHUNK), F32)],
        compiler_params=pltpu.CompilerParams(dimension_semantics=("arbitrary", "arbitrary"),
                                             vmem_limit_bytes=VMEM_LIMIT),
        name="outproj_convffn",
    )(x, ya, ob, mod3, cstate, wts["onw"], wts["n2w"], wts["wo"], wts["wu"], wts["wc"], wts["bc"], wts["wd"])


def _prep_weights(norm1_w, w_in, b_ig, b_fg, ml_norm_w, sb_q_norm_w, sb_k_norm_w, sb_out_norm_w, w_out,
                  norm2_w, w_up, w_conv, b_conv, w_down):
    nm = 4 * ML_WIDTH
    ng = 2 * ML_HEADS
    w_g = w_in[:, nm:nm + ng]
    b_g = jnp.concatenate([b_ig, b_fg])
    return {
        "n1w": norm1_w.reshape(1, D_MODEL),
        "wm": w_in[:, :nm].astype(BF16),
        "wg": jnp.pad(w_g, ((0, 0), (0, LANES - ng))).astype(BF16),
        "wgt": jnp.pad(w_g.T, ((0, 16 - ng), (0, 0))).astype(BF16),
        "ws": w_in[:, nm + ng:].astype(BF16),
        "bgc": jnp.pad(b_g, (0, LANES - ng)).reshape(1, LANES),
        "bgr": jnp.pad(b_g, (0, 16 - ng)).reshape(16, 1),
        "qnw": jnp.tile(sb_q_norm_w, 2).reshape(1, LANES),
        "knw": jnp.tile(sb_k_norm_w, 2).reshape(1, LANES),
        "mnw": ml_norm_w.reshape(1, ML_WIDTH),
        "onw": sb_out_norm_w.reshape(1, SB_WIDTH),
        "n2w": norm2_w.reshape(1, D_MODEL),
        "wo": w_out.astype(BF16),
        "wu": w_up.astype(BF16),
        "wc": jnp.pad(w_conv, ((0, 8 - CONV_W), (0, 0))),
        "bc": b_conv.reshape(1, 2 * D_FF),
        "wd": w_down.astype(BF16),
    }


def _layer(x, mod3, wts, ml_state, k_past, v_past, conv_state, *, bb, tm, chunk, tq, tk):
    B, T, _ = x.shape
    if ml_state is None:
        c0 = jnp.zeros((B, ML_HEADS, ML_DIM, 2 * ML_DIM), F32)
        m0 = jnp.zeros((B, 1, LANES), F32)
    else:
        c_in, n_in, m_in = ml_state
        c0 = jnp.concatenate([c_in, jnp.broadcast_to(n_in[..., None], c_in.shape)], axis=-1)
        m0 = jnp.pad(m_in, ((0, 0), (0, LANES - ML_HEADS))).reshape(B, 1, LANES)
    cps = min(4, T // chunk)
    if bb == 1 and tm == chunk * cps and T > tm:
        qb, kr, vr, k2, v2, ya, c_new, n_new, m_new = _inproj_mlstm_call(
            x, mod3, wts, c0, m0, tm=tm, kblk=min(tq, tk), L=chunk, cps=cps)
    else:
        qa, ka, va, og, gcol, grow, qb, kr, vr, k2, v2 = _inproj_call(x, mod3, wts, bb=bb, tm=tm,
                                                                      kblk=min(tq, tk))
        ya, c_new, n_new, m_new = _mlstm_call(qa, ka, va, og, gcol, grow, wts["mnw"], c0, m0, L=chunk,
                                               cps=cps)

    if k_past is None and tq == 2 * tk:
        ob = _sb_band_call(qb, k2, v2, tk=tk)
    elif k_past is None:
        ob = _sb_call(qb, k2, v2, k2, v2, tq=tq, tk=tk, self_causal=True)
    else:
        ob = _sb_call(qb, k2, v2, jnp.transpose(k_past, (0, 2, 3, 1)), jnp.transpose(v_past, (0, 2, 3, 1)),
                      tq=tq, tk=tk, self_causal=False)

    if conv_state is None:
        conv_state = jnp.zeros((B, CONV_W - 1, 2 * D_FF), F32)
    y, conv_new = _ffn_call(x, ya, ob, mod3, conv_state, wts, bb=bb, tm=tm)
    return (y, c_new, n_new, m_new[:, 0, :ML_HEADS],
            kr.reshape(B, T, SB_HEADS, SB_DIM), vr.reshape(B, T, SB_HEADS, SB_DIM), conv_new)


def kernel(x_prompt, x_sample, c_prompt, c_sample, state_mlstm_c, state_mlstm_n, state_mlstm_m, cache_sb_k, cache_sb_v, state_ffn_conv, w_ada, b_ada, norm1_w, w_in, b_ig, b_fg, ml_norm_w, sb_q_norm_w, sb_k_norm_w, sb_out_norm_w, w_out, norm2_w, w_up, w_conv, b_conv, w_down):
    depth = w_ada.shape[0]
    B = x_prompt.shape[0]
    y_p, y_s = x_prompt, x_sample
    outs_p = [[] for _ in range(6)]
    outs_s = [[] for _ in range(6)]
    for l in range(depth):
        mod = _ada_call(jnp.concatenate([c_prompt, c_sample], axis=0), w_ada[l], b_ada[l])
        mod3 = mod.reshape(mod.shape[0], 6, D_MODEL)
        wts = _prep_weights(norm1_w[l], w_in[l], b_ig[l], b_fg[l], ml_norm_w[l], sb_q_norm_w[l],
                            sb_k_norm_w[l], sb_out_norm_w[l], w_out[l], norm2_w[l], w_up[l], w_conv[l],
                            b_conv[l], w_down[l])
        res_p = _layer(y_p, mod3[:B], wts, None, None, None, None,
                       bb=1, tm=512, chunk=128, tq=256, tk=128)
        res_s = _layer(y_s, mod3[B:], wts, (state_mlstm_c[l], state_mlstm_n[l], state_mlstm_m[l]),
                       cache_sb_k[l], cache_sb_v[l], state_ffn_conv[l],
                       bb=x_sample.shape[0], tm=x_sample.shape[1], chunk=x_sample.shape[1],
                       tq=x_sample.shape[1], tk=128)
        y_p, y_s = res_p[0], res_s[0]
        for i in range(6):
            outs_p[i].append(res_p[i + 1])
            outs_s[i].append(res_s[i + 1])
    return (y_p, y_s, *[jnp.stack(o) for o in outs_p], *[jnp.stack(o) for o in outs_s])
```

```python
import functools

import numpy as np
import jax
import jax.numpy as jnp
from jax import lax
from jax.experimental import pallas as pl
from jax.experimental.pallas import tpu as pltpu

F32 = jnp.float32
BF16 = jnp.bfloat16

D_MODEL = 1024
ML_HEADS = 4
ML_DIM = 128
ML_WIDTH = ML_HEADS * ML_DIM
SB_HEADS = 8
SB_DIM = 64
SB_WIDTH = SB_HEADS * SB_DIM
D_FF = 2816
CONV_W = 3
EPS = 1e-6
LOG2E = 1.4426950408889634
SKIP_BITS = 160.0

LANES = 128
FF_CHUNK = 256
N_FF_CHUNKS = D_FF // FF_CHUNK
DOWN_GROUP = 4
VMEM_LIMIT = 52 * 1024 * 1024
VMEM_LIMIT_BAND = 60 * 1024 * 1024
FUSED_ORDER = "cppccccppcccc"

NT_DIMS = (((1,), (1,)), ((), ()))
TN_DIMS = (((0,), (0,)), ((), ()))


def _vmem_spec():
    return pl.BlockSpec(memory_space=pltpu.VMEM)


def _rms(x, w):
    return x * lax.rsqrt(jnp.mean(x * x, axis=-1, keepdims=True) + EPS) * w


def _softplus(x):
    return jnp.maximum(x, 0.0) + jnp.log(1.0 + jnp.exp(-jnp.abs(x)))


def _split3(x):
    h1 = x.astype(BF16)
    r1 = x - h1.astype(F32)
    h2 = r1.astype(BF16)
    h3 = (r1 - h2.astype(F32)).astype(BF16)
    return h1, h2, h3


def _ada_kernel(c_ref, w_ref, b_ref, o_ref):
    c = c_ref[...]
    s = (c * jax.nn.sigmoid(c)).astype(BF16)
    o_ref[...] = jnp.dot(s, w_ref[...].astype(BF16), preferred_element_type=F32) + b_ref[...]


def _ada_call(c, w_ada, b_ada):
    n, d = c.shape
    nout = w_ada.shape[1]
    return pl.pallas_call(
        _ada_kernel,
        out_shape=jax.ShapeDtypeStruct((n, nout), F32),
        grid=(nout // d,),
        in_specs=[pl.BlockSpec((n, d), lambda j: (0, 0)),
                  pl.BlockSpec((d, d), lambda j: (0, j)),
                  pl.BlockSpec((1, d), lambda j: (0, j))],
        out_specs=pl.BlockSpec((n, d), lambda j: (0, j)),
        compiler_params=pltpu.CompilerParams(dimension_semantics=("arbitrary",)),
        name="adaln",
    )(c, w_ada, b_ada.reshape(1, nout))


def _headnorm64(x, w128):
    lane = lax.broadcasted_iota(jnp.int32, x.shape, 1)
    lo = lane < SB_DIM
    x2 = x * x
    s0 = jnp.sum(jnp.where(lo, x2, 0.0), axis=-1, keepdims=True)
    s1 = jnp.sum(jnp.where(lo, 0.0, x2), axis=-1, keepdims=True)
    r0 = lax.rsqrt(s0 * (1.0 / SB_DIM) + EPS)
    r1 = lax.rsqrt(s1 * (1.0 / SB_DIM) + EPS)
    return x * jnp.where(lo, r0, r1) * w128


def _inproj_phases(x_ref, mod_ref, n1w_ref, wm_ref, wg_ref, wgt_ref, ws_ref, bgc_ref, bgr_ref,
                   qnw_ref, knw_ref,
                   qa_ref, ka_ref, va_ref, og_ref, gcol_ref, grow_ref, qb_ref, kr_ref, vr_ref,
                   k2_ref, v2_ref, *, bb, tm, kblk):
    hs = []
    for bi in range(bb):
        m = mod_ref[bi]
        y = _rms(x_ref[bi], n1w_ref[...])
        hs.append((y * (1.0 + m[1:2, :]) + m[0:1, :]).astype(BF16))
    h = hs[0] if bb == 1 else jnp.concatenate(hs, axis=0)

    def put(ref, val):
        for bi in range(bb):
            ref[bi] = val[bi * tm:(bi + 1) * tm].astype(ref.dtype)

    def proj(w_ref, j):
        return jnp.dot(h, w_ref[:, j * ML_WIDTH:(j + 1) * ML_WIDTH], preferred_element_type=F32)

    put(qa_ref, proj(wm_ref, 0))
    put(ka_ref, proj(wm_ref, 1) * (ML_DIM ** -0.5))
    yield
    put(va_ref, proj(wm_ref, 2))
    put(og_ref, jax.nn.sigmoid(proj(wm_ref, 3)))
    yield

    gc = jnp.dot(h, wg_ref[...], preferred_element_type=F32) + bgc_ref[...]
    lane = lax.broadcasted_iota(jnp.int32, gc.shape, 1)
    put(gcol_ref, jnp.where(lane >= ML_HEADS, -_softplus(-gc), gc))
    gr = lax.dot_general(wgt_ref[...], h, NT_DIMS, preferred_element_type=F32) + bgr_ref[...]
    row = lax.broadcasted_iota(jnp.int32, gr.shape, 0)
    gr = jnp.where(row >= ML_HEADS, -_softplus(-gr), gr)
    for bi in range(bb):
        grow_ref[bi] = gr[:2 * ML_HEADS, bi * tm:(bi + 1) * tm]

    sq = proj(ws_ref, 0)
    sk = proj(ws_ref, 1)
    sv = proj(ws_ref, 2)
    yield
    qn, kn = [], []
    for j in range(SB_WIDTH // LANES):
        sl = slice(j * LANES, (j + 1) * LANES)
        qn.append(_headnorm64(sq[:, sl], qnw_ref[...]) * (SB_DIM ** -0.5 * LOG2E))
        kn.append(_headnorm64(sk[:, sl], knw_ref[...]))
    qn = jnp.concatenate(qn, axis=1)
    kn = jnp.concatenate(kn, axis=1)
    put(qb_ref, qn)
    put(kr_ref, kn)
    put(vr_ref, sv)

    even = (lax.broadcasted_iota(jnp.int32, kn.shape, 1) & SB_DIM) == 0

    def put_pairs(ref, val):
        ve = jnp.where(even, val, 0.0).astype(BF16)
        vo = jnp.where(even, 0.0, val).astype(BF16)
        for bi in range(bb):
            parts = []
            for r0 in range(bi * tm, (bi + 1) * tm, kblk):
                parts += [ve[r0:r0 + kblk], vo[r0:r0 + kblk]]
            ref[bi] = jnp.concatenate(parts, axis=0)

    put_pairs(k2_ref, kn)
    put_pairs(v2_ref, sv)
    yield


def _inproj_kernel(*refs, bb, tm, kblk):
    for _ in _inproj_phases(*refs, bb=bb, tm=tm, kblk=kblk):
        pass


def _inproj_call(x, mod3, wts, *, bb, tm, kblk):
    B, T, D = x.shape
    grid = (B // bb, T // tm)
    tok = lambda b, t: (b, t, 0)

    def tok_spec(w):
        return pl.BlockSpec((bb, tm, w), tok)

    def tok_shape(w, dt):
        return jax.ShapeDtypeStruct((B, T, w), dt)

    out_shape = [tok_shape(ML_WIDTH, BF16), tok_shape(ML_WIDTH, BF16), tok_shape(ML_WIDTH, BF16),
                 tok_shape(ML_WIDTH, F32), tok_shape(LANES, F32),
                 jax.ShapeDtypeStruct((B, 2 * ML_HEADS, T), F32),
                 tok_shape(SB_WIDTH, BF16), tok_shape(SB_WIDTH, F32), tok_shape(SB_WIDTH, F32),
                 jax.ShapeDtypeStruct((B, 2 * T, SB_WIDTH), BF16),
                 jax.ShapeDtypeStruct((B, 2 * T, SB_WIDTH), BF16)]
    pair_spec = pl.BlockSpec((bb, 2 * tm, SB_WIDTH), tok)
    out_specs = [tok_spec(ML_WIDTH), tok_spec(ML_WIDTH), tok_spec(ML_WIDTH), tok_spec(ML_WIDTH),
                 tok_spec(LANES), pl.BlockSpec((bb, 2 * ML_HEADS, tm), lambda b, t: (b, 0, t)),
                 tok_spec(SB_WIDTH), tok_spec(SB_WIDTH), tok_spec(SB_WIDTH), pair_spec, pair_spec]
    in_specs = [tok_spec(D), pl.BlockSpec((bb, 6, D), lambda b, t: (b, 0, 0))] + [_vmem_spec()] * 9
    return pl.pallas_call(
        functools.partial(_inproj_kernel, bb=bb, tm=tm, kblk=kblk),
        out_shape=out_shape, grid=grid, in_specs=in_specs, out_specs=out_specs,
        compiler_params=pltpu.CompilerParams(dimension_semantics=("arbitrary", "arbitrary"),
                                             vmem_limit_bytes=VMEM_LIMIT),
        name="inproj",
    )(x, mod3, wts["n1w"], wts["wm"], wts["wg"], wts["wgt"], wts["ws"], wts["bgc"], wts["bgr"],
      wts["qnw"], wts["knw"])


def _mlstm_kernel(q_ref, k_ref, v_ref, og_ref, gcol_ref, grow_ref, nw_ref, c0_ref, m0_ref,
                  tri_ref, trit_ref,
                  ya_ref, cout_ref, nout_ref, mout_ref,
                  c_scr, m_scr, *, L, cps):
    @pl.when(pl.program_id(1) == 0)
    def _():
        c_scr[...] = c0_ref[0]
        m_scr[...] = m0_ref[0]

    for _ in _mlstm_phases(q_ref, k_ref, v_ref, og_ref, gcol_ref, grow_ref, nw_ref, tri_ref, trit_ref,
                           ya_ref, cout_ref, nout_ref, mout_ref, c_scr, m_scr, L=L, cps=cps):
        pass


def _mlstm_phases(q_ref, k_ref, v_ref, og_ref, gcol_ref, grow_ref, nw_ref, tri_ref, trit_ref,
                  ya_ref, cout_ref, nout_ref, mout_ref, c_scr, m_scr, *, L, cps):
    heads = range(ML_HEADS)
    hss = [slice(h * ML_DIM, (h + 1) * ML_DIM) for h in heads]
    rowss = [slice(cc * L, (cc + 1) * L) for cc in range(cps)]
    r_i = lax.broadcasted_iota(jnp.int32, (L, L), 0)
    c_i = lax.broadcasted_iota(jnp.int32, (L, L), 1)
    causal = c_i <= r_i
    lane1 = lax.broadcasted_iota(jnp.int32, (1, LANES), 1)
    ones = jnp.ones((L, LANES), BF16)

    qks = []
    for rows in rowss:
        qks.append([lax.dot_general(q_ref[0, rows, hs], k_ref[0, rows, hs], NT_DIMS,
                                    preferred_element_type=F32) for hs in hss])
    gcols = [gcol_ref[0, rows, :] for rows in rowss]
    grows = [grow_ref[0, :, rows] for rows in rowss]
    col_parts = [p for g in gcols for p in _split3(g)]
    row_parts = [p for g in grows for p in _split3(jnp.concatenate([g, jnp.zeros_like(g)], axis=0))]
    bc_all = jnp.dot(tri_ref[...], jnp.concatenate(col_parts, axis=1), preferred_element_type=F32)
    br_all = jnp.dot(jnp.concatenate(row_parts, axis=0), trit_ref[...], preferred_element_type=F32)
    gates = []
    for cc in range(cps):
        bcol = sum(bc_all[:, (3 * cc + i) * LANES:(3 * cc + i + 1) * LANES] for i in range(3))
        brow = sum(br_all[(3 * cc + i) * 16:(3 * cc + i + 1) * 16, :] for i in range(3))
        gates.append((gcols[cc], grows[cc], bcol, brow))
    yield

    m_vec = m_scr[...]
    m_run = [m_vec[:, h:h + 1] for h in heads]
    plan = []

    def prepare(cc):
        rows = rowss[cc]
        gcol, grow, bcol, brow = gates[cc]
        per_head = []
        for h in heads:
            b_c = jnp.broadcast_to(bcol[:, ML_HEADS + h:ML_HEADS + h + 1], (L, LANES))
            ig_c = jnp.broadcast_to(gcol[:, h:h + 1], (L, LANES))
            b_r = brow[ML_HEADS + h:ML_HEADS + h + 1, :]
            ig_r = grow[h:h + 1, :]
            b_last = b_r[:, L - 1:L]
            g_r = ig_r - b_r
            m_h = m_run[h]
            dlog = jnp.where(causal, b_c[:, :L] + g_r, -jnp.inf)
            inter = b_c + m_h
            m_t = jnp.maximum(inter, jnp.max(dlog, axis=-1, keepdims=True))
            w_intra = jnp.exp(dlog - m_t[:, :L])
            wk_r = b_last + g_r
            m_new = jnp.maximum(b_last + m_h, jnp.max(wk_r, axis=-1, keepdims=True))
            wk_c = jnp.exp(b_last - b_c + ig_c - m_new)
            v = v_ref[0, rows, hss[h]]
            per_head.append({
                "s": (qks[cc][h] * w_intra).astype(BF16),
                "w_inter": jnp.exp(inter - m_t),
                "floor": jnp.exp(-m_t),
                "decay": jnp.exp(b_last + m_h - m_new),
                "vaug": jnp.concatenate([v, ones], axis=1),
                "wv": jnp.concatenate([(wk_c * v.astype(F32)).astype(BF16), wk_c.astype(BF16)], axis=1),
            })
            m_run[h] = m_new
        for h in heads:
            ph = per_head[h]
            ph["intra"] = jnp.dot(ph["s"], ph["vaug"], preferred_element_type=F32)
            ph["upd"] = lax.dot_general(k_ref[0, rows, hss[h]], ph["wv"], TN_DIMS, preferred_element_type=F32)
        plan.append(per_head)

    def advance(cc):
        rows = rowss[cc]
        per_head = plan[cc]
        for h in heads:
            ph = per_head[h]
            hs = hss[h]
            cst = c_scr[h]
            qc = jnp.dot(q_ref[0, rows, hs], cst.astype(BF16), preferred_element_type=F32)
            num = ph["w_inter"] * qc[:, :ML_DIM] + ph["intra"][:, :ML_DIM]
            den = ph["w_inter"] * qc[:, ML_DIM:] + ph["intra"][:, ML_DIM:]
            hcell = num / jnp.maximum(jnp.abs(den), ph["floor"])
            ya_ref[0, rows, hs] = (_rms(hcell, nw_ref[:, hs]) * og_ref[0, rows, hs]).astype(BF16)
            c_scr[h] = ph["decay"] * cst + ph["upd"]

    for cc in range(cps):
        prepare(cc)
        yield
    for cc in range(cps):
        advance(cc)
        yield

    m_out = m_vec
    mean_row = jnp.full((8, ML_DIM), 1.0 / ML_DIM, BF16)
    for h in heads:
        cst = c_scr[h]
        cout_ref[0, h] = cst[:, :ML_DIM]
        n_rows = sum(lax.dot_general(mean_row, p, NT_DIMS, preferred_element_type=F32)
                     for p in _split3(cst[:, ML_DIM:]))
        nout_ref[0, h:h + 1, :] = n_rows[0:1, :]
        m_out = jnp.where(lane1 == h, m_run[h], m_out)
    m_scr[...] = m_out
    mout_ref[0] = m_out


def _mlstm_call(qa, ka, va, og, gcol, grow, nw, c0, m0, *, L, cps):
    B, T, _ = qa.shape
    tri_np = np.tril(np.ones((L, L), np.float32))
    tri = jnp.asarray(tri_np, BF16)
    trit = jnp.asarray(tri_np.T, BF16)
    tok = lambda b, c: (b, c, 0)
    bat3 = lambda b, c: (b, 0, 0)
    bat4 = lambda b, c: (b, 0, 0, 0)
    tl = L * cps
    in_specs = [pl.BlockSpec((1, tl, ML_WIDTH), tok)] * 4 + [
        pl.BlockSpec((1, tl, LANES), tok),
        pl.BlockSpec((1, 2 * ML_HEADS, tl), lambda b, c: (b, 0, c)),
        _vmem_spec(),
        pl.BlockSpec((1, ML_HEADS, ML_DIM, 2 * ML_DIM), bat4),
        pl.BlockSpec((1, 1, LANES), bat3),
        _vmem_spec(), _vmem_spec()]
    out_shape = [jax.ShapeDtypeStruct((B, T, ML_WIDTH), BF16),
                 jax.ShapeDtypeStruct((B, ML_HEADS, ML_DIM, ML_DIM), F32),
                 jax.ShapeDtypeStruct((B, ML_HEADS, ML_DIM), F32),
                 jax.ShapeDtypeStruct((B, 1, LANES), F32)]
    out_specs = [pl.BlockSpec((1, tl, ML_WIDTH), tok),
                 pl.BlockSpec((1, ML_HEADS, ML_DIM, ML_DIM), bat4),
                 pl.BlockSpec((1, ML_HEADS, ML_DIM), bat3),
                 pl.BlockSpec((1, 1, LANES), bat3)]
    return pl.pallas_call(
        functools.partial(_mlstm_kernel, L=L, cps=cps),
        out_shape=out_shape, grid=(B, T // tl), in_specs=in_specs, out_specs=out_specs,
        scratch_shapes=[pltpu.VMEM((ML_HEADS, ML_DIM, 2 * ML_DIM), F32),
                        pltpu.VMEM((1, LANES), F32)],
        compiler_params=pltpu.CompilerParams(dimension_semantics=("arbitrary", "arbitrary"),
                                             vmem_limit_bytes=VMEM_LIMIT),
        name="mlstm",
    )(qa, ka, va, og, gcol, grow, nw, c0, m0, tri, trit)


def _inproj_mlstm_kernel(x_ref, mod_ref, n1w_ref, wm_ref, wg_ref, wgt_ref, ws_ref, bgc_ref, bgr_ref,
                         qnw_ref, knw_ref, nw_ref, c0_ref, m0_ref, tri_ref, trit_ref,
                         qb_ref, kr_ref, vr_ref, k2_ref, v2_ref, ya_ref, cout_ref, nout_ref, mout_ref,
                         cur_q, cur_k, cur_v, cur_og, cur_gc, cur_gr,
                         prv_q, prv_k, prv_v, prv_og, prv_gc, prv_gr, c_scr, m_scr,
                         *, tm, kblk, L, cps, nt):
    s = pl.program_id(0)
    cur = (cur_q, cur_k, cur_v, cur_og, cur_gc, cur_gr)
    prv = (prv_q, prv_k, prv_v, prv_og, prv_gc, prv_gr)

    @pl.when(s == 0)
    def _():
        for ref in prv + (c_scr, m_scr):
            ref[...] = jnp.zeros_like(ref)

    @pl.when(jnp.logical_and(s > 0, lax.rem(jnp.maximum(s - 1, 0), nt) == 0))
    def _():
        c_scr[...] = c0_ref[0]
        m_scr[...] = m0_ref[0]

    proj = _inproj_phases(x_ref, mod_ref, n1w_ref, wm_ref, wg_ref, wgt_ref, ws_ref, bgc_ref, bgr_ref,
                          qnw_ref, knw_ref, *cur, qb_ref, kr_ref, vr_ref, k2_ref, v2_ref,
                          bb=1, tm=tm, kblk=kblk)
    cell = _mlstm_phases(*prv, nw_ref, tri_ref, trit_ref, ya_ref, cout_ref, nout_ref, mout_ref,
                         c_scr, m_scr, L=L, cps=cps)
    for turn in FUSED_ORDER:
        next(cell if turn == "c" else proj, None)
    for gen in (cell, proj):
        for _ in gen:
            pass
    for c_ref, p_ref in zip(cur, prv):
        p_ref[...] = c_ref[...]


def _inproj_mlstm_call(x, mod3, wts, c0, m0, *, tm, kblk, L, cps):
    B, T, D = x.shape
    nt = T // tm
    n_tiles = B * nt
    tl = L * cps
    assert tl == tm
    tri_np = np.tril(np.ones((L, L), np.float32))

    def cur_tile(s):
        t = jnp.minimum(s, n_tiles - 1)
        return t // nt, t % nt

    def prev_tile(s):
        t = jnp.maximum(s - 1, 0)
        return t // nt, t % nt

    tok = lambda s: (*cur_tile(s), 0)
    tok_prev = lambda s: (*prev_tile(s), 0)
    bat3 = lambda s: (prev_tile(s)[0], 0, 0)
    bat4 = lambda s: (prev_tile(s)[0], 0, 0, 0)
    in_specs = ([pl.BlockSpec((1, tm, D), tok), pl.BlockSpec((1, 6, D), lambda s: (cur_tile(s)[0], 0, 0))]
                + [_vmem_spec()] * 10
                + [pl.BlockSpec((1, ML_HEADS, ML_DIM, 2 * ML_DIM), bat4),
                   pl.BlockSpec((1, 1, LANES), bat3), _vmem_spec(), _vmem_spec()])
    sb_shape = jax.ShapeDtypeStruct((B, T, SB_WIDTH), F32)
    pair_shape = jax.ShapeDtypeStruct((B, 2 * T, SB_WIDTH), BF16)
    out_shape = [jax.ShapeDtypeStruct((B, T, SB_WIDTH), BF16), sb_shape, sb_shape, pair_shape, pair_shape,
                 jax.ShapeDtypeStruct((B, T, ML_WIDTH), BF16),
                 jax.ShapeDtypeStruct((B, ML_HEADS, ML_DIM, ML_DIM), F32),
                 jax.ShapeDtypeStruct((B, ML_HEADS, ML_DIM), F32),
                 jax.ShapeDtypeStruct((B, 1, LANES), F32)]
    sb_spec = pl.BlockSpec((1, tm, SB_WIDTH), tok)
    pair_spec = pl.BlockSpec((1, 2 * tm, SB_WIDTH), tok)
    out_specs = [sb_spec, sb_spec, sb_spec, pair_spec, pair_spec,
                 pl.BlockSpec((1, tm, ML_WIDTH), tok_prev),
                 pl.BlockSpec((1, ML_HEADS, ML_DIM, ML_DIM), bat4),
                 pl.BlockSpec((1, ML_HEADS, ML_DIM), bat3),
                 pl.BlockSpec((1, 1, LANES), bat3)]
    operands = [pltpu.VMEM((1, tm, ML_WIDTH), BF16)] * 3 + [pltpu.VMEM((1, tm, ML_WIDTH), F32),
                                                           pltpu.VMEM((1, tm, LANES), F32),
                                                           pltpu.VMEM((1, 2 * ML_HEADS, tm), F32)]
    return pl.pallas_call(
        functools.partial(_inproj_mlstm_kernel, tm=tm, kblk=kblk, L=L, cps=cps, nt=nt),
        out_shape=out_shape, grid=(n_tiles + 1,), in_specs=in_specs, out_specs=out_specs,
        scratch_shapes=operands + operands + [pltpu.VMEM((ML_HEADS, ML_DIM, 2 * ML_DIM), F32),
                                              pltpu.VMEM((1, LANES), F32)],
        compiler_params=pltpu.CompilerParams(dimension_semantics=("arbitrary",),
                                             vmem_limit_bytes=VMEM_LIMIT),
        name="inproj_mlstm",
    )(x, mod3, wts["n1w"], wts["wm"], wts["wg"], wts["wgt"], wts["ws"], wts["bgc"], wts["bgr"],
      wts["qnw"], wts["knw"], wts["mnw"], c0, m0, jnp.asarray(tri_np, BF16), jnp.asarray(tri_np.T, BF16))


def _suffix_matrix(n):
    u = np.zeros((2 * n, 2 * LANES), np.float32)
    j = np.arange(n)[:, None]
    s = np.arange(n)[None, :]
    u[:n, :n] = (j >= s)
    u[n:, :n] = (j >= s)
    u[:, LANES:] = 1.0
    return jnp.asarray(u, BF16)


def _sb_tiles(jobs, u_ref, n, transposed=False):
    fused = n % LANES == 0
    zs = []
    for q_p, k2, _, _ in jobs:
        if transposed:
            z2 = jnp.dot(q_p, k2, preferred_element_type=F32)
            zs.append([z2[:, :n], z2[:, n:]])
        elif fused:
            z2 = lax.dot_general(q_p, k2, NT_DIMS, preferred_element_type=F32)
            zs.append([z2[:, :n], z2[:, n:]])
        else:
            zs.append([lax.dot_general(q_p, k2[hh * n:(hh + 1) * n], NT_DIMS, preferred_element_type=F32)
                       for hh in range(2)])
    splits = []
    for (_, _, _, mask), zj in zip(jobs, zs):
        sj = []
        for z in zj:
            s = jnp.maximum(z, 0.0) + jnp.log2(1.0 + jnp.exp2(-jnp.abs(z)))
            if mask is not None:
                s = jnp.where(mask, s, 0.0)
            hi = s.astype(BF16)
            sj.append((hi, (s - hi.astype(F32)).astype(BF16)))
        splits.append(sj)
    weights = []
    for (_, _, _, mask), zj, sj in zip(jobs, zs, splits):
        wj = []
        for z, (hi, lo) in zip(zj, sj):
            if fused:
                res = jnp.dot(jnp.concatenate([hi, lo], axis=1), u_ref[...], preferred_element_type=F32)
            else:
                res = (jnp.dot(hi, u_ref[:n, :], preferred_element_type=F32)
                       + jnp.dot(lo, u_ref[n:, :], preferred_element_type=F32))
            a = jnp.exp2(z - res[:, :n])
            if mask is not None:
                a = jnp.where(mask, a, 0.0)
            wj.append((a.astype(BF16), res[:, LANES:]))
        weights.append(wj)
    outs = []
    for (q_p, _, v2, _), wj in zip(jobs, weights):
        if transposed:
            o = lax.dot_general(jnp.concatenate([wj[0][0], wj[1][0]], axis=1), v2, NT_DIMS,
                                preferred_element_type=F32)
        elif fused:
            o = jnp.dot(jnp.concatenate([wj[0][0], wj[1][0]], axis=1), v2, preferred_element_type=F32)
        else:
            o = (jnp.dot(wj[0][0], v2[:n], preferred_element_type=F32)
                 + jnp.dot(wj[1][0], v2[n:], preferred_element_type=F32))
        lane = lax.broadcasted_iota(jnp.int32, (q_p.shape[0], LANES), 1)
        outs.append((o, jnp.where(lane < SB_DIM, wj[0][1], wj[1][1])))
    return outs


def _sb_kernel(q_ref, kd_ref, vd_ref, kp_ref, vp_ref, ud_ref, up_ref, o_ref, acc_ref, used_ref,
               *, tq, tk, nd, kpi, ntrip_static, transposed_cache):
    npairs = SB_WIDTH // LANES
    pss = [slice(p * LANES, (p + 1) * LANES) for p in range(npairs)]
    acc_ref[...] = jnp.zeros_like(acc_ref)
    used_ref[...] = jnp.zeros_like(used_ref)

    def absorb(results, row0s):
        for idx, (o, r) in enumerate(results):
            ps = pss[idx % npairs]
            rs = slice(row0s[idx], tq)
            used = used_ref[rs, ps]
            acc_ref[rs, ps] = acc_ref[rs, ps] + jnp.exp2(-used) * o
            used_ref[rs, ps] = used + r

    jobs, row0s = [], []
    for d in reversed(range(tq // nd)):
        r_i = lax.broadcasted_iota(jnp.int32, (tq - d * nd, nd), 0)
        c_i = lax.broadcasted_iota(jnp.int32, (tq - d * nd, nd), 1)
        for ps in pss:
            jobs.append((q_ref[0, d * nd:, ps], kd_ref[0, 2 * d * nd:2 * (d + 1) * nd, ps],
                         vd_ref[0, 2 * d * nd:2 * (d + 1) * nd, ps], c_i < r_i))
            row0s.append(d * nd)
    absorb(_sb_tiles(jobs, ud_ref, nd), row0s)

    def cache_pair(ref, cols, p):
        even = ref[0, 2 * p, :, cols]
        odd = ref[0, 2 * p + 1, :, cols]
        zero = jnp.zeros_like(even)
        return jnp.concatenate([jnp.concatenate([even, zero], axis=1),
                                jnp.concatenate([zero, odd], axis=1)], axis=0).astype(BF16)

    def past_jobs(j):
        jobs = []
        for kk in reversed(range(kpi)):
            blk = j * kpi + kk
            if transposed_cache:
                cols = pl.ds(pl.multiple_of(blk * tk, tk), tk)
                for p, ps in enumerate(pss):
                    jobs.append((q_ref[0, :, ps], cache_pair(kp_ref, cols, p), cache_pair(vp_ref, cols, p), None))
            else:
                rows = pl.ds(pl.multiple_of(blk * 2 * tk, 2 * tk), 2 * tk)
                for ps in pss:
                    jobs.append((q_ref[0, :, ps], kp_ref[0, rows, ps], vp_ref[0, rows, ps], None))
        return jobs

    def cond(carry):
        j, least_used = carry
        return jnp.logical_and(j >= 0, least_used < SKIP_BITS)

    def body(carry):
        j, _ = carry
        absorb(_sb_tiles(past_jobs(j), up_ref, tk, transposed=transposed_cache), [0] * (kpi * npairs))
        return j - 1, jnp.min(used_ref[...])

    ntrip = pl.program_id(1) if ntrip_static is None else ntrip_static
    lax.while_loop(cond, body, (ntrip - 1, jnp.min(used_ref[...])))
    o_ref[0] = acc_ref[...]


def _sb_band_kernel(q_ref, kd_ref, vd_ref, kp_ref, vp_ref, u_ref, o_ref, acc_ref, used_ref, *, tk, nq):
    npairs = SB_WIDTH // LANES
    pss = [slice(p * LANES, (p + 1) * LANES) for p in range(npairs)]
    tq = 2 * tk
    acc_ref[...] = jnp.zeros_like(acc_ref)
    used_ref[...] = jnp.zeros_like(used_ref)

    def absorb(results, spans, valids):
        for idx, (o, r) in enumerate(results):
            ps = pss[idx % npairs]
            rs = slice(*spans[idx])
            if valids[idx] is not None:
                o, r = o * valids[idx], r * valids[idx]
            used = used_ref[rs, ps]
            acc_ref[rs, ps] = acc_ref[rs, ps] + jnp.exp2(-used) * o
            used_ref[rs, ps] = used + r

    def block(b):
        rows = pl.ds(pl.multiple_of(jnp.maximum(b, 0) * 2 * tk, 2 * tk), 2 * tk)
        return rows, jnp.where(b >= 0, 1.0, 0.0).astype(F32).reshape(1, 1)

    def tiles(row_span, rows, mask):
        lo, hi = row_span
        return [(q_ref[0, lo:hi, ps], kp_ref[0, rows, ps], vp_ref[0, rows, ps], mask) for ps in pss]

    strict = lambda n: (lax.broadcasted_iota(jnp.int32, (n, tk), 1) < lax.broadcasted_iota(jnp.int32, (n, tk), 0))
    jobs, spans, valids = [], [], []
    for g in range(nq):
        i = pl.program_id(1) * nq + g
        r0, k0 = g * tq, g * 2 * tq
        upper, lower, both = (r0 + tk, r0 + tq), (r0, r0 + tk), (r0, r0 + tq)
        rows1, ok1 = block(2 * i - 1)
        rows2, ok2 = block(2 * i - 2)
        jobs += ([(q_ref[0, r0 + tk:r0 + tq, ps], kd_ref[0, k0 + 2 * tk:k0 + 4 * tk, ps],
                   vd_ref[0, k0 + 2 * tk:k0 + 4 * tk, ps], strict(tk)) for ps in pss]
                 + [(q_ref[0, r0:r0 + tq, ps], kd_ref[0, k0:k0 + 2 * tk, ps], vd_ref[0, k0:k0 + 2 * tk, ps],
                     strict(tq)) for ps in pss]
                 + tiles(both, rows1, None) + tiles(lower, rows2, None))
        spans += [upper] * npairs + [both] * npairs + [both] * npairs + [lower] * npairs
        valids += [None] * (2 * npairs) + [ok1] * npairs + [ok2] * npairs
    absorb(_sb_tiles(jobs, u_ref, tk), spans, valids)

    def cond(carry):
        b, least_used = carry
        return jnp.logical_and(b >= 0, least_used < SKIP_BITS)

    for g in range(nq):
        i = pl.program_id(1) * nq + g
        r0 = g * tq
        upper, lower = (r0 + tk, r0 + tq), (r0, r0 + tk)

        def body(carry, upper=upper, lower=lower, r0=r0):
            b, _ = carry
            rows_u, ok_u = block(b)
            rows_l, ok_l = block(b - 1)
            absorb(_sb_tiles(tiles(upper, rows_u, None) + tiles(lower, rows_l, None), u_ref, tk),
                   [upper] * npairs + [lower] * npairs, [ok_u] * npairs + [ok_l] * npairs)
            return b - 1, jnp.min(used_ref[r0:r0 + tq, :])

        lax.while_loop(cond, body, (2 * i - 2, jnp.min(used_ref[r0:r0 + tq, :])))
    o_ref[0] = acc_ref[...]


def _sb_band_call(qb, k2, v2, *, tk, nq):
    B, T, W = qb.shape
    tq = 2 * tk * nq
    tok = lambda b, i: (b, i, 0)
    seq = pl.BlockSpec((1, 2 * T, W), lambda b, i: (b, 0, 0))
    own = pl.BlockSpec((1, 2 * tq, W), tok)
    return pl.pallas_call(
        functools.partial(_sb_band_kernel, tk=tk, nq=nq),
        out_shape=jax.ShapeDtypeStruct((B, T, W), F32),
        grid=(B, T // tq), in_specs=[pl.BlockSpec((1, tq, W), tok), own, own, seq, seq, _vmem_spec()],
        out_specs=pl.BlockSpec((1, tq, W), tok),
        scratch_shapes=[pltpu.VMEM((tq, W), F32), pltpu.VMEM((tq, W), F32)],
        compiler_params=pltpu.CompilerParams(dimension_semantics=("arbitrary", "arbitrary"),
                                             vmem_limit_bytes=VMEM_LIMIT_BAND),
        name="stickbreak_band",
    )(qb, k2, v2, k2, v2, _suffix_matrix(tk))


def _sb_call(qb, kd, vd, kp, vp, *, tq, tk, self_causal):
    B, T, W = qb.shape
    nd = min(tq, tk)
    tok = lambda b, i: (b, i, 0)
    if self_causal:
        Tp = kp.shape[1] // 2
        kpi = tq // tk
        past_spec = pl.BlockSpec((1, 2 * Tp, W), lambda b, i: (b, 0, 0))
    else:
        Tp = kp.shape[3]
        kpi = 2
        past_spec = pl.BlockSpec((1,) + kp.shape[1:], lambda b, i: (b, 0, 0, 0))
    in_specs = [pl.BlockSpec((1, tq, W), tok), pl.BlockSpec((1, 2 * tq, W), tok), pl.BlockSpec((1, 2 * tq, W), tok),
                past_spec, past_spec, _vmem_spec(), _vmem_spec()]
    return pl.pallas_call(
        functools.partial(_sb_kernel, tq=tq, tk=tk, nd=nd, kpi=kpi, transposed_cache=not self_causal,
                          ntrip_static=None if self_causal else Tp // (tk * kpi)),
        out_shape=jax.ShapeDtypeStruct((B, T, W), F32),
        grid=(B, T // tq), in_specs=in_specs, out_specs=pl.BlockSpec((1, tq, W), tok),
        scratch_shapes=[pltpu.VMEM((tq, W), F32), pltpu.VMEM((tq, W), F32)],
        compiler_params=pltpu.CompilerParams(dimension_semantics=("arbitrary", "arbitrary"),
                                             vmem_limit_bytes=VMEM_LIMIT),
        name="stickbreak",
    )(qb, kd, vd, kp, vp, _suffix_matrix(nd), _suffix_matrix(tk))


def _ffn_kernel(x_ref, ya_ref, ob_ref, mod_ref, cst_ref, onw_ref, n2w_ref, wo_ref, wu_ref, wc_ref,
                bc_ref, wd_ref,
                y_ref, cnew_ref, tail_ref, ubuf_ref, *, bb, tm):
    @pl.when(pl.program_id(1) == 0)
    def _():
        tail_ref[...] = cst_ref[...]

    ycat = []
    for bi in range(bb):
        yb = _rms(ob_ref[bi], onw_ref[...]).astype(BF16)
        ycat.append(jnp.concatenate([ya_ref[bi], yb], axis=1))
    ycat = ycat[0] if bb == 1 else jnp.concatenate(ycat, axis=0)
    mix = jnp.dot(ycat, wo_ref[...], preferred_element_type=F32)

    x1s, h2s = [], []
    for bi in range(bb):
        m = mod_ref[bi]
        x1 = x_ref[bi] + m[2:3, :] * mix[bi * tm:(bi + 1) * tm]
        x1s.append(x1)
        h2s.append((_rms(x1, n2w_ref[...]) * (1.0 + m[4:5, :]) + m[3:4, :]).astype(BF16))
    h2 = h2s[0] if bb == 1 else jnp.concatenate(h2s, axis=0)

    def conv(col0, slot):
        cols = slice(col0, col0 + FF_CHUNK)
        u = jnp.dot(h2, wu_ref[:, cols], preferred_element_type=F32)
        w0, w1, w2 = wc_ref[0:1, cols], wc_ref[1:2, cols], wc_ref[2:3, cols]
        outs = []
        for bi in range(bb):
            ub = u[bi * tm:(bi + 1) * tm]
            base = bi * (tm + 8)
            ubuf_ref[slot, base + 6:base + 8, :] = tail_ref[bi, :, cols]
            ubuf_ref[slot, base + 8:base + 8 + tm, :] = ub
            u1 = ubuf_ref[slot, base + 7:base + 7 + tm, :]
            u2 = ubuf_ref[slot, base + 6:base + 6 + tm, :]
            outs.append(u2 * w0 + u1 * w1 + ub * w2 + bc_ref[:, cols])
            tail_ref[bi, :, cols] = ubuf_ref[slot, base + 6 + tm:base + 8 + tm, :]
        return outs[0] if bb == 1 else jnp.concatenate(outs, axis=0)

    def gated(c):
        ua = conv(c * FF_CHUNK, 2 * (c % 2))
        ug = conv(D_FF + c * FF_CHUNK, 2 * (c % 2) + 1)
        return (ua * jax.nn.sigmoid(ua) * ug).astype(BF16)

    acc = None
    pending, first = [gated(0)], 0
    for c in range(N_FF_CHUNKS):
        nxt = gated(c + 1) if c + 1 < N_FF_CHUNKS else None
        if len(pending) == DOWN_GROUP or nxt is None:
            lhs = pending[0] if len(pending) == 1 else jnp.concatenate(pending, axis=1)
            part = jnp.dot(lhs, wd_ref[first * FF_CHUNK:(c + 1) * FF_CHUNK, :], preferred_element_type=F32)
            acc = part if acc is None else acc + part
            pending, first = [], c + 1
        if nxt is not None:
            pending.append(nxt)

    for bi in range(bb):
        y_ref[bi] = x1s[bi] + mod_ref[bi][5:6, :] * acc[bi * tm:(bi + 1) * tm]

    @pl.when(pl.program_id(1) == pl.num_programs(1) - 1)
    def _():
        cnew_ref[...] = tail_ref[...]


def _ffn_call(x, ya, ob, mod3, cstate, wts, *, bb, tm):
    B, T, D = x.shape
    tok = lambda b, t: (b, t, 0)
    bat = lambda b, t: (b, 0, 0)
    in_specs = [pl.BlockSpec((bb, tm, D), tok), pl.BlockSpec((bb, tm, ML_WIDTH), tok),
                pl.BlockSpec((bb, tm, SB_WIDTH), tok), pl.BlockSpec((bb, 6, D), bat),
                pl.BlockSpec((bb, CONV_W - 1, 2 * D_FF), bat)] + [_vmem_spec()] * 7
    out_shape = [jax.ShapeDtypeStruct((B, T, D), F32), jax.ShapeDtypeStruct((B, CONV_W - 1, 2 * D_FF), F32)]
    out_specs = [pl.BlockSpec((bb, tm, D), tok), pl.BlockSpec((bb, CONV_W - 1, 2 * D_FF), bat)]
    return pl.pallas_call(
        functools.partial(_ffn_kernel, bb=bb, tm=tm),
        out_shape=out_shape, grid=(B // bb, T // tm), in_specs=in_specs, out_specs=out_specs,
        scratch_shapes=[pltpu.VMEM((bb, CONV_W - 1, 2 * D_FF), F32),
                        pltpu.VMEM((4, bb * (tm + 8), FF_CHUNK), F32)],
        compiler_params=pltpu.CompilerParams(dimension_semantics=("arbitrary", "arbitrary"),
                                             vmem_limit_bytes=VMEM_LIMIT),
        name="outproj_convffn",
    )(x, ya, ob, mod3, cstate, wts["onw"], wts["n2w"], wts["wo"], wts["wu"], wts["wc"], wts["bc"], wts["wd"])


def _prep_weights(norm1_w, w_in, b_ig, b_fg, ml_norm_w, sb_q_norm_w, sb_k_norm_w, sb_out_norm_w, w_out,
                  norm2_w, w_up, w_conv, b_conv, w_down):
    nm = 4 * ML_WIDTH
    ng = 2 * ML_HEADS
    w_g = w_in[:, nm:nm + ng]
    b_g = jnp.concatenate([b_ig, b_fg])
    return {
        "n1w": norm1_w.reshape(1, D_MODEL),
        "wm": w_in[:, :nm].astype(BF16),
        "wg": jnp.pad(w_g, ((0, 0), (0, LANES - ng))).astype(BF16),
        "wgt": jnp.pad(w_g.T, ((0, 16 - ng), (0, 0))).astype(BF16),
        "ws": w_in[:, nm + ng:].astype(BF16),
        "bgc": jnp.pad(b_g, (0, LANES - ng)).reshape(1, LANES),
        "bgr": jnp.pad(b_g, (0, 16 - ng)).reshape(16, 1),
        "qnw": jnp.tile(sb_q_norm_w, 2).reshape(1, LANES),
        "knw": jnp.tile(sb_k_norm_w, 2).reshape(1, LANES),
        "mnw": ml_norm_w.reshape(1, ML_WIDTH),
        "onw": sb_out_norm_w.reshape(1, SB_WIDTH),
        "n2w": norm2_w.reshape(1, D_MODEL),
        "wo": w_out.astype(BF16),
        "wu": w_up.astype(BF16),
        "wc": jnp.pad(w_conv, ((0, 8 - CONV_W), (0, 0))),
        "bc": b_conv.reshape(1, 2 * D_FF),
        "wd": w_down.astype(BF16),
    }


def _layer(x, mod3, wts, ml_state, k_past, v_past, conv_state, *, bb, tm, chunk, tq, tk):
    B, T, _ = x.shape
    if ml_state is None:
        c0 = jnp.zeros((B, ML_HEADS, ML_DIM, 2 * ML_DIM), F32)
        m0 = jnp.zeros((B, 1, LANES), F32)
    else:
        c_in, n_in, m_in = ml_state
        c0 = jnp.concatenate([c_in, jnp.broadcast_to(n_in[..., None], c_in.shape)], axis=-1)
        m0 = jnp.pad(m_in, ((0, 0), (0, LANES - ML_HEADS))).reshape(B, 1, LANES)
    cps = min(4, T // chunk)
    if bb == 1 and tm == chunk * cps and T > tm:
        qb, kr, vr, k2, v2, ya, c_new, n_new, m_new = _inproj_mlstm_call(
            x, mod3, wts, c0, m0, tm=tm, kblk=min(tq, tk), L=chunk, cps=cps)
    else:
        qa, ka, va, og, gcol, grow, qb, kr, vr, k2, v2 = _inproj_call(x, mod3, wts, bb=bb, tm=tm,
                                                                      kblk=min(tq, tk))
        ya, c_new, n_new, m_new = _mlstm_call(qa, ka, va, og, gcol, grow, wts["mnw"], c0, m0, L=chunk,
                                               cps=cps)

    if k_past is None and tq == 2 * tk:
        ob = _sb_band_call(qb, k2, v2, tk=tk, nq=2)
    elif k_past is None:
        ob = _sb_call(qb, k2, v2, k2, v2, tq=tq, tk=tk, self_causal=True)
    else:
        ob = _sb_call(qb, k2, v2, jnp.transpose(k_past, (0, 2, 3, 1)), jnp.transpose(v_past, (0, 2, 3, 1)),
                      tq=tq, tk=tk, self_causal=False)

    if conv_state is None:
        conv_state = jnp.zeros((B, CONV_W - 1, 2 * D_FF), F32)
    y, conv_new = _ffn_call(x, ya, ob, mod3, conv_state, wts, bb=bb, tm=tm)
    return (y, c_new, n_new, m_new[:, 0, :ML_HEADS],
            kr.reshape(B, T, SB_HEADS, SB_DIM), vr.reshape(B, T, SB_HEADS, SB_DIM), conv_new)


def kernel(x_prompt, x_sample, c_prompt, c_sample, state_mlstm_c, state_mlstm_n, state_mlstm_m, cache_sb_k, cache_sb_v, state_ffn_conv, w_ada, b_ada, norm1_w, w_in, b_ig, b_fg, ml_norm_w, sb_q_norm_w, sb_k_norm_w, sb_out_norm_w, w_out, norm2_w, w_up, w_conv, b_conv, w_down):
    depth = w_ada.shape[0]
    B = x_prompt.shape[0]
    y_p, y_s = x_prompt, x_sample
    outs_p = [[] for _ in range(6)]
    outs_s = [[] for _ in range(6)]
    for l in range(depth):
        mod = _ada_call(jnp.concatenate([c_prompt, c_sample], axis=0), w_ada[l], b_ada[l])
        mod3 = mod.reshape(mod.shape[0], 6, D_MODEL)
        wts = _prep_weights(norm1_w[l], w_in[l], b_ig[l], b_fg[l], ml_norm_w[l], sb_q_norm_w[l],
                            sb_k_norm_w[l], sb_out_norm_w[l], w_out[l], norm2_w[l], w_up[l], w_conv[l],
                            b_conv[l], w_down[l])
        res_p = _layer(y_p, mod3[:B], wts, None, None, None, None,
                       bb=1, tm=512, chunk=128, tq=256, tk=128)
        res_s = _layer(y_s, mod3[B:], wts, (state_mlstm_c[l], state_mlstm_n[l], state_mlstm_m[l]),
                       cache_sb_k[l], cache_sb_v[l], state_ffn_conv[l],
                       bb=x_sample.shape[0], tm=x_sample.shape[1], chunk=x_sample.shape[1],
                       tq=x_sample.shape[1], tk=128)
        y_p, y_s = res_p[0], res_s[0]
        for i in range(6):
            outs_p[i].append(res_p[i + 1])
            outs_s[i].append(res_s[i + 1])
    return (y_p, y_s, *[jnp.stack(o) for o in outs_p], *[jnp.stack(o) for o in outs_s])
```
